```python
import math
import jax
import jax.numpy as jnp
from jax import lax
import numpy as np

D_MODEL = 1024
BATCH = 8
SEQ = 2048
DEPTH = 4
DEC_BATCH = 8
DEC_SEQ = 16
PAST_LEN = 2048

CHUNK = 64
Q_BLOCK = 128
A_HEADS = 8
NOPE_DIM = 64
ROPE_DIM = 32
V_DIM = 64
Q_LORA = 384
KV_LORA = 256
ROPE_THETA = 10000.0
W_A = A_HEADS * V_DIM
SM_SCALE = (NOPE_DIM + ROPE_DIM) ** -0.5
C_B = 256
CONV_B_WIDTH = 31
C_C = 256
C_GROUPS = 4
GMLP_CHUNK = 128
C_D = 256
CONV_D_WIDTH = 3
N_BRANCH = 4
D_FF = 2816
EPS = 1e-6
O_Q = Q_LORA
O_KV = O_Q + KV_LORA
O_KR = O_KV + ROPE_DIM
O_B = O_KR + 2 * C_B
O_C = O_B + 2 * C_C
O_D = O_C + 3 * C_D
D_IN = O_D + N_BRANCH * D_MODEL

kernel_name = 'hybrid_mla_conformer_gmlp_shortconv_stream_step'


def rmsnorm(x, g):
    xf = x.astype(jnp.float32)
    xf = xf * lax.rsqrt(jnp.mean(jnp.square(xf), axis=-1, keepdims=True) + EPS)
    return (xf * g.astype(jnp.float32)).astype(x.dtype)


def layernorm(x, g, b):
    xf = x.astype(jnp.float32)
    mu = jnp.mean(xf, axis=-1, keepdims=True)
    var = jnp.mean(jnp.square(xf - mu), axis=-1, keepdims=True)
    y = (xf - mu) * lax.rsqrt(var + EPS)
    return (y * g.astype(jnp.float32) + b.astype(jnp.float32)).astype(x.dtype)


def rope_tables(pos):
    half = ROPE_DIM // 2
    inv = jnp.exp(-math.log(ROPE_THETA) * jnp.arange(half, dtype=jnp.float32) / half)
    ang = pos.astype(jnp.float32)[:, None] * inv[None, :]
    return jnp.cos(ang), jnp.sin(ang)


def apply_rope(x, cos, sin):
    x1, x2 = jnp.split(x, 2, axis=-1)
    cos = cos.astype(x.dtype)
    sin = sin.astype(x.dtype)
    return jnp.concatenate([x1 * cos - x2 * sin, x2 * cos + x1 * sin], axis=-1)


def causal_dwconv(xpad, w):
    return lax.conv_general_dilated(
        xpad, w[:, None, :].astype(xpad.dtype), window_strides=(1,), padding='VALID',
        dimension_numbers=('NWC', 'WIO', 'NWC'), feature_group_count=xpad.shape[-1])


def swiglu_ffn(x, g_pre, g_post, w_gu, w_down):
    gate, up = jnp.split(rmsnorm(x, g_pre) @ w_gu, 2, axis=-1)
    return rmsnorm((jax.nn.silu(gate) * up) @ w_down, g_post)


def mla_expand(latent, w_ukv):
    b, l = latent.shape[:2]
    kv = (latent @ w_ukv).reshape(b, l, A_HEADS, NOPE_DIM + V_DIM)
    return kv[..., :NOPE_DIM], kv[..., NOPE_DIM:]


def mla_scores(q_nope, q_rope, k_nope, k_rope):
    s = (jnp.einsum('bqhd,bkhd->bhqk', q_nope, k_nope)
         + jnp.einsum('bqhr,bkr->bhqk', q_rope, k_rope))
    return s.astype(jnp.float32) * SM_SCALE


def mla_prompt(q_nope, q_rope, latent, k_rope, w_ukv):
    b, s = q_nope.shape[:2]
    nb = s // Q_BLOCK
    k_nope, v = mla_expand(latent, w_ukv)
    key_chunk = jnp.arange(s) // CHUNK
    qn = q_nope.reshape(b, nb, Q_BLOCK, A_HEADS, NOPE_DIM).swapaxes(0, 1)
    qr = q_rope.reshape(b, nb, Q_BLOCK, A_HEADS, ROPE_DIM).swapaxes(0, 1)

    def block(args):
        qn_b, qr_b, i = args
        q_chunk = (i * Q_BLOCK + jnp.arange(Q_BLOCK)) // CHUNK
        sc = mla_scores(qn_b, qr_b, k_nope, k_rope)
        sc = jnp.where(key_chunk[None, :] <= q_chunk[:, None], sc, -jnp.inf)
        prob = jax.nn.softmax(sc, axis=-1).astype(v.dtype)
        return jnp.einsum('bhqk,bkhd->bqhd', prob, v)

    o = lax.map(block, (qn, qr, jnp.arange(nb)))
    return o.swapaxes(0, 1).reshape(b, s, W_A)


def mla_sample(q_nope, q_rope, lat_all, kr_all, w_ukv):
    b, t = q_nope.shape[:2]
    k_nope, v = mla_expand(lat_all, w_ukv)
    prob = jax.nn.softmax(mla_scores(q_nope, q_rope, k_nope, kr_all), axis=-1).astype(v.dtype)
    return jnp.einsum('bhqk,bkhd->bqhd', prob, v).reshape(b, t, W_A)


def gmlp_mask():
    ar = jnp.arange(GMLP_CHUNK) // CHUNK
    return ar[None, :] <= ar[:, None]


def spatial_mix_prompt(v, w_s, b_s):
    b, s, _ = v.shape
    w = jnp.where(gmlp_mask()[None], w_s, 0.0).astype(v.dtype)
    vr = v.reshape(b, s // GMLP_CHUNK, GMLP_CHUNK, C_GROUPS, C_C // C_GROUPS)
    out = jnp.einsum('gij,bnjgc->bnigc', w, vr) + b_s.T[:, :, None]
    return out.reshape(b, s, C_C)


def spatial_mix_sample(v, w_s, b_s):
    b, t, _ = v.shape
    w = jnp.where(gmlp_mask()[None], w_s, 0.0).astype(v.dtype)[:, :t, :t]
    vr = v.reshape(b, t, C_GROUPS, C_C // C_GROUPS)
    out = jnp.einsum('gij,bjgc->bigc', w, vr) + b_s[:, :t].T[:, :, None]
    return out.reshape(b, t, C_C)


def token_mixer(n, cos, sin, p, cache_lat, cache_kr, buf_b, buf_d):
    b, t, _ = n.shape
    h = n @ p['w_in']
    cq, ckv, kr, glu_in, uv, dproj, g = jnp.split(h, [O_Q, O_KV, O_KR, O_B, O_C, O_D], axis=-1)

    q = (rmsnorm(cq, p['q_norm']) @ p['w_uq']).reshape(b, t, A_HEADS, NOPE_DIM + ROPE_DIM)
    q_nope = q[..., :NOPE_DIM]
    q_rope = apply_rope(q[..., NOPE_DIM:], cos[:, None], sin[:, None])
    latent = rmsnorm(ckv, p['kv_norm'])
    k_rope = apply_rope(kr, cos, sin)
    if cache_lat is None:
        a = mla_prompt(q_nope, q_rope, latent, k_rope, p['w_ukv'])
    else:
        a = mla_sample(q_nope, q_rope,
                       jnp.concatenate([cache_lat.astype(latent.dtype), latent], axis=1),
                       jnp.concatenate([cache_kr.astype(k_rope.dtype), k_rope], axis=1),
                       p['w_ukv'])

    ga, gg = jnp.split(glu_in, 2, axis=-1)
    xb = ga * jax.nn.sigmoid(gg)
    bpad = jnp.concatenate([buf_b.astype(xb.dtype), xb], axis=1)
    yb = causal_dwconv(bpad, p['conv_b_w']) + p['conv_b_bias']
    yb = jax.nn.silu(layernorm(yb, p['conv_b_ln_g'], p['conv_b_ln_b']))
    new_b = bpad[:, -(CONV_B_WIDTH - 1):]

    u, v = jnp.split(uv, 2, axis=-1)
    v = layernorm(v, p['gmlp_vn_g'], p['gmlp_vn_b'])
    if cache_lat is None:
        yc = u * spatial_mix_prompt(v, p['gmlp_w_s'], p['gmlp_b_s'])
    else:
        yc = u * spatial_mix_sample(v, p['gmlp_w_s'], p['gmlp_b_s'])

    bg, cg, hd = jnp.split(dproj, 3, axis=-1)
    xd = cg * hd
    dpad = jnp.concatenate([buf_d.astype(xd.dtype), xd], axis=1)
    yd = bg * causal_dwconv(dpad, p['conv_d_w'])
    new_d = dpad[:, -(CONV_D_WIDTH - 1):]

    gates = jax.nn.sigmoid(g).reshape(b, t, N_BRANCH, D_MODEL)
    merged = (gates[:, :, 0] * (a @ p['w_br_a']) + gates[:, :, 1] * (yb @ p['w_br_b'])
              + gates[:, :, 2] * (yc @ p['w_br_c']) + gates[:, :, 3] * (yd @ p['w_br_d']))
    return merged @ p['w_o'], (latent, k_rope, new_b, v, new_d)


def setup_inputs(seed: int = 0) -> dict:
    key = jax.random.key(seed)
    ks = iter(jax.random.split(key, 64))

    def nrm(shape, scale):
        return scale * jax.random.normal(next(ks), shape, jnp.float32)

    def gain(shape):
        return 1.0 + nrm(shape, 0.05)

    L = DEPTH
    return {
        'x_prompt': nrm((BATCH, SEQ, D_MODEL), 1.0),
        'x_sample': nrm((DEC_BATCH, DEC_SEQ, D_MODEL), 1.0),
        'cache_kv_latent': nrm((L, DEC_BATCH, PAST_LEN, KV_LORA), 1.0),
        'cache_k_rope': nrm((L, DEC_BATCH, PAST_LEN, ROPE_DIM), 1.0),
        'state_conv_b': nrm((L, DEC_BATCH, CONV_B_WIDTH - 1, C_B), 0.5),
        'state_conv_d': nrm((L, DEC_BATCH, CONV_D_WIDTH - 1, C_D), 0.5),
        'ffn1_norm_pre': gain((L, D_MODEL)),
        'ffn1_norm_post': gain((L, D_MODEL)),
        'ffn1_w_gu': nrm((L, D_MODEL, 2 * D_FF), D_MODEL ** -0.5),
        'ffn1_w_down': nrm((L, D_FF, D_MODEL), D_FF ** -0.5),
        'mix_norm_pre': gain((L, D_MODEL)),
        'mix_norm_post': gain((L, D_MODEL)),
        'w_in': nrm((L, D_MODEL, D_IN), D_MODEL ** -0.5),
        'q_norm': gain((L, Q_LORA)),
        'w_uq': nrm((L, Q_LORA, A_HEADS * (NOPE_DIM + ROPE_DIM)), Q_LORA ** -0.5),
        'kv_norm': gain((L, KV_LORA)),
        'w_ukv': nrm((L, KV_LORA, A_HEADS * (NOPE_DIM + V_DIM)), KV_LORA ** -0.5),
        'conv_b_w': nrm((L, CONV_B_WIDTH, C_B), CONV_B_WIDTH ** -0.5),
        'conv_b_bias': nrm((L, C_B), 0.02),
        'conv_b_ln_g': gain((L, C_B)),
        'conv_b_ln_b': nrm((L, C_B), 0.02),
        'gmlp_vn_g': gain((L, C_C)),
        'gmlp_vn_b': nrm((L, C_C), 0.02),
        'gmlp_w_s': nrm((L, C_GROUPS, GMLP_CHUNK, GMLP_CHUNK), GMLP_CHUNK ** -0.5),
        'gmlp_b_s': 1.0 + nrm((L, C_GROUPS, GMLP_CHUNK), 0.1),
        'conv_d_w': nrm((L, CONV_D_WIDTH, C_D), CONV_D_WIDTH ** -0.5),
        'w_br_a': nrm((L, W_A, D_MODEL), W_A ** -0.5),
        'w_br_b': nrm((L, C_B, D_MODEL), C_B ** -0.5),
        'w_br_c': nrm((L, C_C, D_MODEL), C_C ** -0.5),
        'w_br_d': nrm((L, C_D, D_MODEL), C_D ** -0.5),
        'w_o': nrm((L, D_MODEL, D_MODEL), D_MODEL ** -0.5),
        'ffn2_norm_pre': gain((L, D_MODEL)),
        'ffn2_norm_post': gain((L, D_MODEL)),
        'ffn2_w_gu': nrm((L, D_MODEL, 2 * D_FF), D_MODEL ** -0.5),
        'ffn2_w_down': nrm((L, D_FF, D_MODEL), D_FF ** -0.5),
    }


def reference(x_prompt, x_sample, cache_kv_latent, cache_k_rope, state_conv_b, state_conv_d,
              ffn1_norm_pre, ffn1_norm_post, ffn1_w_gu, ffn1_w_down,
              mix_norm_pre, mix_norm_post, w_in, q_norm, w_uq, kv_norm, w_ukv,
              conv_b_w, conv_b_bias, conv_b_ln_g, conv_b_ln_b,
              gmlp_vn_g, gmlp_vn_b, gmlp_w_s, gmlp_b_s, conv_d_w,
              w_br_a, w_br_b, w_br_c, w_br_d, w_o,
              ffn2_norm_pre, ffn2_norm_post, ffn2_w_gu, ffn2_w_down):
    s = x_prompt.shape[1]
    t = x_sample.shape[1]
    past = cache_kv_latent.shape[2]
    cos_p, sin_p = rope_tables(jnp.arange(s))
    cos_s, sin_s = rope_tables(past + jnp.arange(t))
    xp, xs = x_prompt, x_sample
    zb = jnp.zeros((xp.shape[0], CONV_B_WIDTH - 1, C_B), xp.dtype)
    zd = jnp.zeros((xp.shape[0], CONV_D_WIDTH - 1, C_D), xp.dtype)
    lat_p, kr_p, cb_p, cd_p = [], [], [], []
    lat_s, kr_s, cb_s, vc_s, cd_s = [], [], [], [], []
    for l in range(DEPTH):
        p = {'w_in': w_in[l], 'q_norm': q_norm[l], 'w_uq': w_uq[l], 'kv_norm': kv_norm[l],
             'w_ukv': w_ukv[l], 'conv_b_w': conv_b_w[l], 'conv_b_bias': conv_b_bias[l],
             'conv_b_ln_g': conv_b_ln_g[l], 'conv_b_ln_b': conv_b_ln_b[l],
             'gmlp_vn_g': gmlp_vn_g[l], 'gmlp_vn_b': gmlp_vn_b[l], 'gmlp_w_s': gmlp_w_s[l],
             'gmlp_b_s': gmlp_b_s[l], 'conv_d_w': conv_d_w[l], 'w_br_a': w_br_a[l],
             'w_br_b': w_br_b[l], 'w_br_c': w_br_c[l], 'w_br_d': w_br_d[l], 'w_o': w_o[l]}
        xp = xp + 0.5 * swiglu_ffn(xp, ffn1_norm_pre[l], ffn1_norm_post[l], ffn1_w_gu[l], ffn1_w_down[l])
        xs = xs + 0.5 * swiglu_ffn(xs, ffn1_norm_pre[l], ffn1_norm_post[l], ffn1_w_gu[l], ffn1_w_down[l])
        hp, (a1, a2, a3, _, a5) = token_mixer(rmsnorm(xp, mix_norm_pre[l]), cos_p, sin_p, p,
                                              None, None, zb, zd)
        xp = xp + rmsnorm(hp, mix_norm_post[l])
        hs, (b1, b2, b3, b4, b5) = token_mixer(rmsnorm(xs, mix_norm_pre[l]), cos_s, sin_s, p,
                                               cache_kv_latent[l], cache_k_rope[l],
                                               state_conv_b[l], state_conv_d[l])
        xs = xs + rmsnorm(hs, mix_norm_post[l])
        xp = xp + 0.5 * swiglu_ffn(xp, ffn2_norm_pre[l], ffn2_norm_post[l], ffn2_w_gu[l], ffn2_w_down[l])
        xs = xs + 0.5 * swiglu_ffn(xs, ffn2_norm_pre[l], ffn2_norm_post[l], ffn2_w_gu[l], ffn2_w_down[l])
        lat_p.append(a1); kr_p.append(a2); cb_p.append(a3); cd_p.append(a5)
        lat_s.append(b1); kr_s.append(b2); cb_s.append(b3); vc_s.append(b4); cd_s.append(b5)
    return (xp, xs,
            jnp.stack(lat_p), jnp.stack(kr_p), jnp.stack(cb_p), jnp.stack(cd_p),
            jnp.stack(lat_s), jnp.stack(kr_s), jnp.stack(cb_s), jnp.stack(vc_s), jnp.stack(cd_s))
```

```python
import functools
import math

import jax
import jax.numpy as jnp
from jax import lax
from jax.experimental import pallas as pl
from jax.experimental.pallas import tpu as pltpu

F32 = jnp.float32
BF16 = jnp.bfloat16

CHUNK = 64
A_HEADS = 8
NOPE_DIM = 64
ROPE_DIM = 32
V_DIM = 64
ROPE_THETA = 10000.0
SM_SCALE = (NOPE_DIM + ROPE_DIM) ** -0.5
C_GROUPS = 4
GMLP_CHUNK = 128
N_BRANCH = 4
EPS = 1e-6

LANES = 128
HEAD_PAD = 128
CONV_B_PAD = 32
CONV_D_PAD = 8
VMEM_LIMIT_BYTES = 56 * 1024 * 1024
NEG_BIG = -1e30


def _pick_tile(n, candidates):
    for c in candidates:
        if n % c == 0:
            return c
    return n


def _resident(shape):
    nd = len(shape)
    return pl.BlockSpec(shape, lambda *_: (0,) * nd, pipeline_mode=pl.Buffered(1))


def _rms(x, g):
    ms = jnp.mean(x * x, axis=-1, keepdims=True)
    return x * lax.rsqrt(ms + EPS) * g


def _layernorm(x, g, b):
    mu = jnp.mean(x, axis=-1, keepdims=True)
    xc = x - mu
    var = jnp.mean(xc * xc, axis=-1, keepdims=True)
    return xc * lax.rsqrt(var + EPS) * g + b


def _dot(a, b):
    return jnp.dot(a, b, preferred_element_type=F32)


def _dot_nt(a, b):
    return lax.dot_general(a, b, (((1,), (1,)), ((), ())), preferred_element_type=F32)


def _ffn_kernel(x_ref, gpre_ref, gpost_ref, wgu_ref, wd_ref, o_ref, act_ref, *, d_ff, tf):
    x = x_ref[...]
    n = _rms(x, gpre_ref[...]).astype(BF16)
    for c in range(d_ff // tf):
        gate = _dot(n, wgu_ref[:, c * tf:(c + 1) * tf])
        up = _dot(n, wgu_ref[:, d_ff + c * tf:d_ff + (c + 1) * tf])
        act_ref[:, c * tf:(c + 1) * tf] = (gate * jax.nn.sigmoid(gate) * up).astype(BF16)
    y = _dot(act_ref[...], wd_ref[...])
    o_ref[...] = x + 0.5 * _rms(y, gpost_ref[...])


def _ffn(x, gpre, gpost, wgu, wd):
    n, d = x.shape
    d_ff = wd.shape[0]
    tm = _pick_tile(n, (512, 256, 128, 64, 32, 16, 8))
    tf = _pick_tile(d_ff, (256, 128))
    return pl.pallas_call(
        functools.partial(_ffn_kernel, d_ff=d_ff, tf=tf),
        grid=(n // tm,),
        in_specs=[
            pl.BlockSpec((tm, d), lambda i: (i, 0)),
            _resident((1, d)),
            _resident((1, d)),
            _resident((d, 2 * d_ff)),
            _resident((d_ff, d)),
        ],
        out_specs=pl.BlockSpec((tm, d), lambda i: (i, 0)),
        out_shape=jax.ShapeDtypeStruct((n, d), F32),
        scratch_shapes=[pltpu.VMEM((tm, d_ff), BF16)],
        name="ffn",
        compiler_params=pltpu.CompilerParams(
            dimension_semantics=("parallel",), vmem_limit_bytes=VMEM_LIMIT_BYTES),
    )(x, gpre, gpost, wgu, wd)


def _rope128(a, c, sneg, spos):
    return (a * c + pltpu.roll(a, LANES - ROPE_DIM // 2, 1) * sneg
            + pltpu.roll(a, ROPE_DIM // 2, 1) * spos)


def _mix_in_kernel(x_ref, c_ref, sneg_ref, spos_ref, stb_ref, std_ref,
                   gpre_ref, win_ref, qn_ref, wq_ref, kvn_ref, wk_ref, wv_ref,
                   cbw_ref, cbb_ref, lng_ref, lnb_ref, vng_ref, vnb_ref, ws_ref, bs_ref, cdw_ref,
                   q_ref, k_ref, v_ref, y_ref, lat_ref, kr_ref, bt_ref, dt_ref, vg_ref,
                   xpb_ref, xpd_ref, cv_ref, *, tq, q_lora, kv_lora, c_b, c_c, c_d, kb_w, kd_w, rb):
    s = pl.program_id(1)
    x = x_ref[...]
    n = _rms(x, gpre_ref[...]).astype(BF16)
    ctab, sneg, spos = c_ref[...], sneg_ref[...], spos_ref[...]

    o_kv = q_lora
    o_b = o_kv + kv_lora
    o_c = o_b + 2 * c_b
    o_d = o_c + 2 * c_c
    o_kr = o_d + 3 * c_d

    cq = _dot(n, win_ref[:, 0:o_kv])
    qlat = _rms(cq, qn_ref[...]).astype(BF16)
    qa = _dot(qlat, wq_ref[...])
    for h in range(A_HEADS):
        qh = _rope128(qa[:, h * HEAD_PAD:(h + 1) * HEAD_PAD], ctab, sneg, spos) * SM_SCALE
        q_ref[:, h * HEAD_PAD:(h + 1) * HEAD_PAD] = qh.astype(BF16)

    ckv = _dot(n, win_ref[:, o_kv:o_b])
    lat = _rms(ckv, kvn_ref[...])
    lat_ref[...] = lat
    latb = lat.astype(BF16)
    kr = _rope128(_dot(n, win_ref[:, o_kr:o_kr + LANES]), ctab, sneg, spos)
    kr_ref[...] = kr
    kn = _dot(latb, wk_ref[...])
    for h in range(A_HEADS):
        k_ref[:, h * HEAD_PAD:(h + 1) * HEAD_PAD] = (kn[:, h * HEAD_PAD:(h + 1) * HEAD_PAD] + kr).astype(BF16)
    v_ref[...] = _dot(latb, wv_ref[...]).astype(BF16)

    glu = _dot(n, win_ref[:, o_b:o_c])
    xb = glu[:, 0:c_b] * jax.nn.sigmoid(glu[:, c_b:2 * c_b])

    @pl.when(s == 0)
    def _():
        xpb_ref[0:CONV_B_PAD, :] = stb_ref[...]
        xpd_ref[0:CONV_D_PAD, :] = std_ref[...]

    @pl.when(s > 0)
    def _():
        xpb_ref[0:CONV_B_PAD, :] = xpb_ref[tq:tq + CONV_B_PAD, :]
        xpd_ref[0:CONV_D_PAD, :] = xpd_ref[tq:tq + CONV_D_PAD, :]

    xpb_ref[CONV_B_PAD:CONV_B_PAD + tq, :] = xb
    bt_ref[...] = xpb_ref[tq:tq + CONV_B_PAD, :]
    off_b = CONV_B_PAD - (kb_w - 1)
    for r in range(tq // rb):
        acc = jnp.broadcast_to(cbb_ref[...], (rb, c_b))
        for kk in range(kb_w):
            acc = acc + xpb_ref[r * rb + off_b + kk:r * rb + off_b + kk + rb, :] * cbw_ref[kk:kk + 1, :]
        cv_ref[r * rb:(r + 1) * rb, :] = acc
    yb = _layernorm(cv_ref[...], lng_ref[...], lnb_ref[...])
    y_ref[:, 0:c_b] = (yb * jax.nn.sigmoid(yb)).astype(BF16)

    uv = _dot(n, win_ref[:, o_c:o_d])
    vg = _layernorm(uv[:, c_c:2 * c_c], vng_ref[...], vnb_ref[...])
    vg_ref[...] = vg
    vgb = vg.astype(BF16)
    ck = min(tq, GMLP_CHUNK)
    ri = lax.broadcasted_iota(jnp.int32, (ck, ck), 0) // CHUNK
    ci = lax.broadcasted_iota(jnp.int32, (ck, ck), 1) // CHUNK
    lane_grp = lax.broadcasted_iota(jnp.int32, (ck, c_c), 1) // (c_c // C_GROUPS)
    wmix = [jnp.where(ci <= ri, ws_ref[g, 0:ck, 0:ck], 0.0).astype(BF16) for g in range(C_GROUPS)]
    for c in range(tq // ck):
        vc = vgb[c * ck:(c + 1) * ck, :]
        mix = bs_ref[0:ck, :]
        for g in range(C_GROUPS):
            mix = mix + jnp.where(lane_grp == g, _dot(wmix[g], vc), 0.0)
        y_ref[c * ck:(c + 1) * ck, c_b:c_b + c_c] = (uv[c * ck:(c + 1) * ck, 0:c_c] * mix).astype(BF16)

    dp = _dot(n, win_ref[:, o_d:o_kr])
    xd = dp[:, c_d:2 * c_d] * dp[:, 2 * c_d:3 * c_d]
    xpd_ref[CONV_D_PAD:CONV_D_PAD + tq, :] = xd
    dt_ref[...] = xpd_ref[tq:tq + CONV_D_PAD, :]
    off_d = CONV_D_PAD - (kd_w - 1)
    conv = xd * cdw_ref[kd_w - 1:kd_w, :]
    for kk in range(kd_w - 1):
        conv = conv + xpd_ref[off_d + kk:off_d + kk + tq, :] * cdw_ref[kk:kk + 1, :]
    y_ref[:, c_b + c_c:c_b + c_c + c_d] = (dp[:, 0:c_d] * conv).astype(BF16)


def _mix_in(x, tabs, stb, std, lw, *, tq):
    b, s, d = x.shape
    q_lora, kv_lora = lw["q_norm"].shape[1], lw["kv_norm"].shape[1]
    c_b, c_c, c_d = lw["conv_b_bias"].shape[1], lw["gmlp_vn_g"].shape[1], lw["conv_d_w"].shape[1]
    kb_w, kd_w = lw["kb_w"], lw["kd_w"]
    hp = A_HEADS * HEAD_PAD
    rb = min(tq, 64)
    tile = lambda w: pl.BlockSpec((None, tq, w), lambda i, j: (i, j, 0))
    per_b = lambda r, w: pl.BlockSpec((None, r, w), lambda i, j: (i, 0, 0))
    tab = pl.BlockSpec((tq, LANES), lambda i, j: (j, 0))
    weights = [lw["mix_norm_pre"], lw["w_in1"], lw["q_norm"], lw["wq_pad"], lw["kv_norm"], lw["wk_pad"],
               lw["wv"], lw["conv_b_w"], lw["conv_b_bias"], lw["conv_b_ln_g"], lw["conv_b_ln_b"],
               lw["gmlp_vn_g"], lw["gmlp_vn_b"], lw["gmlp_w_s"], lw["bs_full"], lw["conv_d_w"]]
    out_shape = (
        jax.ShapeDtypeStruct((b, s, hp), BF16),
        jax.ShapeDtypeStruct((b, s, hp), BF16),
        jax.ShapeDtypeStruct((b, s, A_HEADS * V_DIM), BF16),
        jax.ShapeDtypeStruct((b, s, c_b + c_c + c_d), BF16),
        jax.ShapeDtypeStruct((b, s, kv_lora), F32),
        jax.ShapeDtypeStruct((b, s, LANES), F32),
        jax.ShapeDtypeStruct((b, CONV_B_PAD, c_b), F32),
        jax.ShapeDtypeStruct((b, CONV_D_PAD, c_d), F32),
        jax.ShapeDtypeStruct((b, s, c_c), F32),
    )
    out_specs = (tile(hp), tile(hp), tile(A_HEADS * V_DIM), tile(c_b + c_c + c_d), tile(kv_lora),
                 tile(LANES), per_b(CONV_B_PAD, c_b), per_b(CONV_D_PAD, c_d), tile(c_c))
    return pl.pallas_call(
        functools.partial(_mix_in_kernel, tq=tq, q_lora=q_lora, kv_lora=kv_lora, c_b=c_b, c_c=c_c, c_d=c_d,
                          kb_w=kb_w, kd_w=kd_w, rb=rb),
        grid=(b, s // tq),
        in_specs=[tile(d), tab, tab, tab, per_b(CONV_B_PAD, c_b), per_b(CONV_D_PAD, c_d)]
        + [_resident(w.shape) for w in weights],
        out_specs=out_specs,
        out_shape=out_shape,
        scratch_shapes=[pltpu.VMEM((CONV_B_PAD + tq, c_b), F32), pltpu.VMEM((CONV_D_PAD + tq, c_d), F32),
                        pltpu.VMEM((tq, c_b), F32)],
        name="mix_in",
        compiler_params=pltpu.CompilerParams(
            dimension_semantics=("arbitrary", "arbitrary"), vmem_limit_bytes=VMEM_LIMIT_BYTES),
    )(x, *tabs, stb, std, *weights)


def _attn_kernel(q_ref, k_ref, v_ref, o_ref, *, tq):
    qi = pl.program_id(1)
    ri = lax.broadcasted_iota(jnp.int32, (tq, tq), 0) // CHUNK
    ci = lax.broadcasted_iota(jnp.int32, (tq, tq), 1) // CHUNK
    diag_ok = ci <= ri
    low_half = lax.broadcasted_iota(jnp.int32, (tq, LANES), 1) < V_DIM

    def head(h):
        q = q_ref[:, h * HEAD_PAD:(h + 1) * HEAD_PAD]
        vcol = (h // 2) * LANES

        def step(j, carry, masked):
            m, l, acc = carry
            k0 = pl.multiple_of(j * tq, tq)
            sc = _dot_nt(q, k_ref[pl.ds(k0, tq), h * HEAD_PAD:(h + 1) * HEAD_PAD])
            if masked:
                sc = jnp.where(diag_ok, sc, NEG_BIG)
            m_new = jnp.maximum(m, jnp.max(sc, axis=-1, keepdims=True))
            alpha = jnp.exp(m - m_new)
            p = jnp.exp(sc - m_new)
            l = alpha * l + jnp.sum(p, axis=-1, keepdims=True)
            acc = alpha * acc + _dot(p.astype(BF16), v_ref[pl.ds(k0, tq), vcol:vcol + LANES])
            return m_new, l, acc

        init = (jnp.full((tq, 1), NEG_BIG, F32), jnp.zeros((tq, 1), F32), jnp.zeros((tq, LANES), F32))
        carry = lax.fori_loop(0, qi, lambda j, c: step(j, c, False), init)
        m, l, acc = step(qi, carry, True)
        return acc / l

    for pair in range(A_HEADS // 2):
        o_ref[:, pair * LANES:(pair + 1) * LANES] = jnp.where(
            low_half, head(2 * pair), head(2 * pair + 1)).astype(BF16)


def _attn_prompt(q, k, v, *, tq):
    b, s, hp = q.shape
    wv = v.shape[2]
    return pl.pallas_call(
        functools.partial(_attn_kernel, tq=tq),
        grid=(b, s // tq),
        in_specs=[
            pl.BlockSpec((None, tq, hp), lambda i, j: (i, j, 0)),
            pl.BlockSpec((None, s, hp), lambda i, j: (i, 0, 0)),
            pl.BlockSpec((None, s, wv), lambda i, j: (i, 0, 0)),
        ],
        out_specs=pl.BlockSpec((None, tq, wv), lambda i, j: (i, j, 0)),
        out_shape=jax.ShapeDtypeStruct((b, s, wv), BF16),
        name="attn_prompt",
        compiler_params=pltpu.CompilerParams(
            dimension_semantics=("parallel", "parallel"), vmem_limit_bytes=VMEM_LIMIT_BYTES),
    )(q, k, v)


def _attn_sample_kernel(q_ref, clat_ref, ckr_ref, lat_ref, kr_ref, wkt_ref, wv_ref, o_ref, kn_ref, *, t):
    q = q_ref[...]
    qh = [q[:, h * HEAD_PAD:(h + 1) * HEAD_PAD] for h in range(A_HEADS)]
    qabs = jnp.concatenate(
        [_dot(qh[h], wkt_ref[h * HEAD_PAD:(h + 1) * HEAD_PAD, :]) for h in range(A_HEADS)], axis=0)
    qcat = jnp.concatenate([qabs.astype(BF16), jnp.concatenate(qh, axis=0)], axis=1)
    clat = clat_ref[...].astype(BF16)
    kv_lora = clat.shape[1]
    kc = jnp.concatenate([clat, ckr_ref[...]], axis=1)
    kn_ref[...] = jnp.zeros(kn_ref.shape, BF16)
    kn_ref[0:t, :] = jnp.concatenate([lat_ref[...], kr_ref[...]], axis=1).astype(BF16)
    kn = kn_ref[...]
    s1 = _dot_nt(qcat, kc)
    s2 = _dot_nt(qcat, kn)
    s2 = jnp.where(lax.broadcasted_iota(jnp.int32, s2.shape, 1) < t, s2, NEG_BIG)
    m =jnp.maximum(jnp.max(s1, axis=-1, keepdims=True), jnp.max(s2, axis=-1, keepdims=True))
    p1 = jnp.exp(s1 - m)
    p2 = jnp.exp(s2 - m)
    l = jnp.sum(p1, axis=-1, keepdims=True) + jnp.sum(p2, axis=-1, keepdims=True)
    olat = ((_dot(p1.astype(BF16), clat) + _dot(p2.astype(BF16), kn[:, 0:kv_lora])) / l).astype(BF16)
    low_half = lax.broadcasted_iota(jnp.int32, (t, LANES), 1) < V_DIM
    for pair in range(A_HEADS // 2):
        wpair = wv_ref[:, pair * LANES:(pair + 1) * LANES]
        lo = _dot(olat[(2 * pair) * t:(2 * pair + 1) * t, :], wpair)
        hi = _dot(olat[(2 * pair + 1) * t:(2 * pair + 2) * t, :], wpair)
        o_ref[:, pair * LANES:(pair + 1) * LANES] = jnp.where(low_half, lo, hi).astype(BF16)


def _attn_sample(q, cache_lat, cache_kr, lat, kr, wkt, wv):
    b, t, hp = q.shape
    past, kv_lora = cache_lat.shape[1], cache_lat.shape[2]
    wvw = wv.shape[1]
    per_b = lambda r, w: pl.BlockSpec((None, r, w), lambda i: (i, 0, 0))
    return pl.pallas_call(
        functools.partial(_attn_sample_kernel, t=t),
        grid=(b,),
        in_specs=[per_b(t, hp), per_b(past, kv_lora), per_b(past, LANES), per_b(t, kv_lora),
                  per_b(t, LANES), _resident(wkt.shape), _resident(wv.shape)],
        out_specs=per_b(t, wvw),
        out_shape=jax.ShapeDtypeStruct((b, t, wvw), BF16),
        scratch_shapes=[pltpu.VMEM((LANES, kv_lora + LANES), BF16)],
        name="attn_sample",
        compiler_params=pltpu.CompilerParams(
            dimension_semantics=("parallel",), vmem_limit_bytes=VMEM_LIMIT_BYTES),
    )(q, cache_lat, cache_kr, lat, kr, wkt, wv)


def _mix_out_kernel(x_ref, a_ref, y_ref, gpre_ref, gpost_ref, wg_ref, wba_ref, wbb_ref, wbc_ref, wbd_ref,
                    wo_ref, o_ref, *, d, c_b, c_c, c_d):
    x = x_ref[...]
    n = _rms(x, gpre_ref[...]).astype(BF16)
    branches = (
        (a_ref[...], wba_ref),
        (y_ref[:, 0:c_b], wbb_ref),
        (y_ref[:, c_b:c_b + c_c], wbc_ref),
        (y_ref[:, c_b + c_c:c_b + c_c + c_d], wbd_ref),
    )
    merged = jnp.zeros(x.shape, F32)
    for i, (br, w_ref) in enumerate(branches):
        gate = jax.nn.sigmoid(_dot(n, wg_ref[:, i * d:(i + 1) * d]))
        merged = merged + gate * _dot(br, w_ref[...])
    out = _dot(merged.astype(BF16), wo_ref[...])
    o_ref[...] = x + _rms(out, gpost_ref[...])


def _mix_out(x, a, y, lw):
    n, d = x.shape
    c_b, c_c, c_d = lw["w_br_b"].shape[0], lw["w_br_c"].shape[0], lw["w_br_d"].shape[0]
    tm = _pick_tile(n, (512, 256, 128, 64, 32, 16, 8))
    weights = [lw["mix_norm_pre"], lw["mix_norm_post"], lw["w_g"], lw["w_br_a"], lw["w_br_b"], lw["w_br_c"],
               lw["w_br_d"], lw["w_o"]]
    row = lambda w: pl.BlockSpec((tm, w), lambda i: (i, 0))
    return pl.pallas_call(
        functools.partial(_mix_out_kernel, d=d, c_b=c_b, c_c=c_c, c_d=c_d),
        grid=(n // tm,),
        in_specs=[row(d), row(a.shape[1]), row(y.shape[1])] + [_resident(w.shape) for w in weights],
        out_specs=row(d),
        out_shape=jax.ShapeDtypeStruct((n, d), F32),
        name="mix_out",
        compiler_params=pltpu.CompilerParams(
            dimension_semantics=("parallel",), vmem_limit_bytes=VMEM_LIMIT_BYTES),
    )(x, a, y, *weights)


def _rope_tabs(pos):
    half = ROPE_DIM // 2
    inv = jnp.exp(-math.log(ROPE_THETA) * jnp.arange(half, dtype=F32) / half)
    ang = pos.astype(F32)[:, None] * inv[None, :]
    cos, sin = jnp.cos(ang), jnp.sin(ang)
    n = pos.shape[0]
    z = lambda w: jnp.zeros((n, w), F32)
    ctab = jnp.concatenate([cos, cos, jnp.ones((n, NOPE_DIM), F32), z(LANES - ROPE_DIM - NOPE_DIM)], axis=1)
    sneg = jnp.concatenate([-sin, z(LANES - half)], axis=1)
    spos = jnp.concatenate([z(half), sin, z(LANES - 2 * half)], axis=1)
    return ctab, sneg, spos


def _pad_rows(w, rows):
    return jnp.pad(w, ((0, 0), (0, rows - w.shape[1]), (0, 0)))


def kernel(x_prompt, x_sample, cache_kv_latent, cache_k_rope, state_conv_b, state_conv_d, ffn1_norm_pre, ffn1_norm_post, ffn1_w_gu, ffn1_w_down, mix_norm_pre, mix_norm_post, w_in, q_norm, w_uq, kv_norm, w_ukv, conv_b_w, conv_b_bias, conv_b_ln_g, conv_b_ln_b, gmlp_vn_g, gmlp_vn_b, gmlp_w_s, gmlp_b_s, conv_d_w, w_br_a, w_br_b, w_br_c, w_br_d, w_o, ffn2_norm_pre, ffn2_norm_post, ffn2_w_gu, ffn2_w_down):
    b, s, d = x_prompt.shape
    bs, t, _ = x_sample.shape
    depth = w_in.shape[0]
    past = cache_kv_latent.shape[2]
    q_lora, kv_lora = q_norm.shape[1], kv_norm.shape[1]
    c_b, c_c, c_d = conv_b_bias.shape[1], gmlp_vn_g.shape[1], conv_d_w.shape[2]
    kb_w, kd_w = conv_b_w.shape[1], conv_d_w.shape[1]
    assert kb_w - 1 <= CONV_B_PAD and kd_w - 1 <= CONV_D_PAD
    assert s % GMLP_CHUNK == 0 and t <= GMLP_CHUNK and t % 16 == 0 and past % CHUNK == 0 and t <= CHUNK
    assert gmlp_w_s.shape[1] == C_GROUPS and w_ukv.shape[2] == A_HEADS * (NOPE_DIM + V_DIM)

    o_q = q_lora
    o_kv = o_q + kv_lora
    o_kr = o_kv + ROPE_DIM
    o_b = o_kr + 2 * c_b
    o_c = o_b + 2 * c_c
    o_d = o_c + 3 * c_d

    vec = lambda p: p[:, None, :]
    w_in1 = jnp.concatenate(
        [w_in[:, :, 0:o_kv], w_in[:, :, o_kr:o_d], w_in[:, :, o_kv:o_kr],
         jnp.zeros((depth, d, LANES - ROPE_DIM), F32)], axis=2).astype(BF16)
    w_g = w_in[:, :, o_d:].astype(BF16)
    uq = w_uq.reshape(depth, q_lora, A_HEADS, NOPE_DIM + ROPE_DIM)
    wq_pad = jnp.concatenate(
        [uq[..., NOPE_DIM:], uq[..., :NOPE_DIM],
         jnp.zeros((depth, q_lora, A_HEADS, HEAD_PAD - NOPE_DIM - ROPE_DIM), F32)],
        axis=3).reshape(depth, q_lora, A_HEADS * HEAD_PAD).astype(BF16)
    ukv = w_ukv.reshape(depth, kv_lora, A_HEADS, NOPE_DIM + V_DIM)
    wk_pad = jnp.concatenate(
        [jnp.zeros((depth, kv_lora, A_HEADS, ROPE_DIM), F32), ukv[..., :NOPE_DIM],
         jnp.zeros((depth, kv_lora, A_HEADS, HEAD_PAD - NOPE_DIM - ROPE_DIM), F32)],
        axis=3).reshape(depth, kv_lora, A_HEADS * HEAD_PAD).astype(BF16)
    wk_t = jnp.swapaxes(wk_pad, 1, 2)
    wv = ukv[..., NOPE_DIM:].reshape(depth, kv_lora, A_HEADS * V_DIM).astype(BF16)
    bs_full = jnp.repeat(jnp.swapaxes(gmlp_b_s, 1, 2), c_c // C_GROUPS, axis=2)
    cbw = _pad_rows(conv_b_w, CONV_B_PAD)
    cdw = _pad_rows(conv_d_w, CONV_D_PAD)
    ffn1_gu, ffn1_dn = ffn1_w_gu.astype(BF16), ffn1_w_down.astype(BF16)
    ffn2_gu, ffn2_dn = ffn2_w_gu.astype(BF16), ffn2_w_down.astype(BF16)
    wba, wbb, wbc, wbd = (w.astype(BF16) for w in (w_br_a, w_br_b, w_br_c, w_br_d))
    wo = w_o.astype(BF16)

    tabs_p = _rope_tabs(jnp.arange(s))
    tabs_s = _rope_tabs(past + jnp.arange(t))
    zero_b = jnp.zeros((b, CONV_B_PAD, c_b), F32)
    zero_d = jnp.zeros((b, CONV_D_PAD, c_d), F32)
    st_b = jnp.pad(state_conv_b, ((0, 0), (0, 0), (CONV_B_PAD - (kb_w - 1), 0), (0, 0)))
    st_d = jnp.pad(state_conv_d, ((0, 0), (0, 0), (CONV_D_PAD - (kd_w - 1), 0), (0, 0)))
    ckr_pad = jnp.pad(cache_k_rope, ((0, 0), (0, 0), (0, 0), (0, LANES - ROPE_DIM))).astype(BF16)
    tq = _pick_tile(s, (256, 128))

    xp = x_prompt.reshape(b * s, d)
    xs = x_sample.reshape(bs * t, d)
    outs = [[] for _ in range(9)]
    for l in range(depth):
        lw = dict(
            mix_norm_pre=vec(mix_norm_pre)[l], mix_norm_post=vec(mix_norm_post)[l], w_in1=w_in1[l], w_g=w_g[l],
            q_norm=vec(q_norm)[l], wq_pad=wq_pad[l], kv_norm=vec(kv_norm)[l], wk_pad=wk_pad[l], wv=wv[l],
            conv_b_w=cbw[l], conv_b_bias=vec(conv_b_bias)[l], conv_b_ln_g=vec(conv_b_ln_g)[l],
            conv_b_ln_b=vec(conv_b_ln_b)[l], gmlp_vn_g=vec(gmlp_vn_g)[l], gmlp_vn_b=vec(gmlp_vn_b)[l],
            gmlp_w_s=gmlp_w_s[l], bs_full=bs_full[l], conv_d_w=cdw[l], kb_w=kb_w, kd_w=kd_w,
            w_br_a=wba[l], w_br_b=wbb[l], w_br_c=wbc[l], w_br_d=wbd[l], w_o=wo[l])
        f1 = (vec(ffn1_norm_pre)[l], vec(ffn1_norm_post)[l], ffn1_gu[l], ffn1_dn[l])
        f2 = (vec(ffn2_norm_pre)[l], vec(ffn2_norm_post)[l], ffn2_gu[l], ffn2_dn[l])

        xp = _ffn(xp, *f1)
        xs = _ffn(xs, *f1)

        qp, kp, vp, yp, latp, krp, btp, dtp, _ = _mix_in(xp.reshape(b, s, d), tabs_p, zero_b, zero_d, lw, tq=tq)
        ap = _attn_prompt(qp, kp, vp, tq=tq)
        xp = _mix_out(xp, ap.reshape(b * s, -1), yp.reshape(b * s, -1), lw)

        qs, _, _, ys, lats, krs, bts, dts, vgs = _mix_in(xs.reshape(bs, t, d), tabs_s, st_b[l], st_d[l], lw, tq=t)
        a_s = _attn_sample(qs, cache_kv_latent[l], ckr_pad[l], lats, krs, wk_t[l], wv[l])
        xs = _mix_out(xs, a_s.reshape(bs * t, -1), ys.reshape(bs * t, -1), lw)

        xp = _ffn(xp, *f2)
        xs = _ffn(xs, *f2)

        tail_b = CONV_B_PAD - (kb_w - 1)
        tail_d = CONV_D_PAD - (kd_w - 1)
        for lst, val in zip(outs, (latp, krp[..., :ROPE_DIM], btp[:, tail_b:], dtp[:, tail_d:],
                                   lats, krs[..., :ROPE_DIM], bts[:, tail_b:], vgs, dts[:, tail_d:])):
            lst.append(val)
    return (xp.reshape(b, s, d), xs.reshape(bs, t, d)) + tuple(jnp.stack(o) for o in outs)
```

```python
import functools
import math

import jax
import jax.numpy as jnp
from jax import lax
from jax.experimental import pallas as pl
from jax.experimental.pallas import tpu as pltpu

F32 = jnp.float32
BF16 = jnp.bfloat16

CHUNK = 64
A_HEADS = 8
NOPE_DIM = 64
ROPE_DIM = 32
V_DIM = 64
ROPE_THETA = 10000.0
SM_SCALE = (NOPE_DIM + ROPE_DIM) ** -0.5
Q_SCALE = SM_SCALE * math.log2(math.e)
C_GROUPS = 4
GMLP_CHUNK = 128
N_BRANCH = 4
EPS = 1e-6

LANES = 128
SUBLANES = 8
HEAD_PAD = 128
CONV_B_PAD = 32
CONV_D_PAD = 8
VMEM_LIMIT_BYTES = 56 * 1024 * 1024
NEG_BIG = -1e30


def _pick_tile(n, candidates):
    for c in candidates:
        if n % c == 0:
            return c
    return n


def _layer(arr, l):
    nd = arr.ndim - 1
    return pl.BlockSpec((None,) + arr.shape[1:], lambda *_: (l,) + (0,) * nd, pipeline_mode=pl.Buffered(1))


def _whole(arr):
    nd = arr.ndim
    return pl.BlockSpec(arr.shape, lambda *_: (0,) * nd, pipeline_mode=pl.Buffered(1))


def _rms(x, g):
    ms = jnp.mean(x * x, axis=-1, keepdims=True)
    return x * lax.rsqrt(ms + EPS) * g


def _layernorm(x, g, b):
    mu = jnp.mean(x, axis=-1, keepdims=True)
    xc = x - mu
    var = jnp.mean(xc * xc, axis=-1, keepdims=True)
    return xc * lax.rsqrt(var + EPS) * g + b


def _dot(a, b):
    return jnp.dot(a, b, preferred_element_type=F32)


def _dot_nt(a, b):
    return lax.dot_general(a, b, (((1,), (1,)), ((), ())), preferred_element_type=F32)


def _params(sem):
    return pltpu.CompilerParams(dimension_semantics=sem, vmem_limit_bytes=VMEM_LIMIT_BYTES)


def _ffn_kernel(x_ref, gpre_ref, gpost_ref, wgu_ref, wd_ref, o_ref, act_ref, *, d_ff, tf):
    x = x_ref[...]
    n = _rms(x, gpre_ref[...]).astype(BF16)
    for c in range(d_ff // tf):
        gate = _dot(n, wgu_ref[:, c * tf:(c + 1) * tf])
        up = _dot(n, wgu_ref[:, d_ff + c * tf:d_ff + (c + 1) * tf])
        act_ref[:, c * tf:(c + 1) * tf] = (gate * jax.nn.sigmoid(gate) * up).astype(BF16)
    y = _dot(act_ref[...], wd_ref[...])
    o_ref[...] = x + 0.5 * _rms(y, gpost_ref[...])


def _ffn(x, gpre, gpost, wgu, wd, l):
    n, d = x.shape
    d_ff = wd.shape[1]
    tm = _pick_tile(n, (512, 256, 128, 64, 32, 16, 8))
    tf = _pick_tile(d_ff, (256, 128))
    weights = (gpre, gpost, wgu, wd)
    return pl.pallas_call(
        functools.partial(_ffn_kernel, d_ff=d_ff, tf=tf),
        grid=(n // tm,),
        in_specs=[pl.BlockSpec((tm, d), lambda i: (i, 0))] + [_layer(w, l) for w in weights],
        out_specs=pl.BlockSpec((tm, d), lambda i: (i, 0)),
        out_shape=jax.ShapeDtypeStruct((n, d), F32),
        scratch_shapes=[pltpu.VMEM((tm, d_ff), BF16)],
        name="ffn",
        compiler_params=_params(("parallel",)),
    )(x, *weights)


def _rope128(a, c, sneg, spos):
    return (a * c + pltpu.roll(a, LANES - ROPE_DIM // 2, 1) * sneg
            + pltpu.roll(a, ROPE_DIM // 2, 1) * spos)


MIX_IN_WEIGHTS = ("mix_norm_pre", "w_in1", "q_norm", "wq_pad", "kv_norm", "wk_pad", "wv", "conv_b_w",
                  "conv_b_bias", "conv_b_ln_g", "conv_b_ln_b", "gmlp_vn_g", "gmlp_vn_b", "gmlp_w_s", "bs_full",
                  "conv_d_w")
N_MIX_IN_INPUTS = 6 + len(MIX_IN_WEIGHTS)


def _mix_in_kernel(*refs, prompt, n_alias, tq, q_lora, kv_lora, c_b, c_c, c_d, kb_w, kd_w, rb):
    (x_ref, c_ref, sneg_ref, spos_ref, stb_ref, std_ref,
     gpre_ref, win_ref, qn_ref, wq_ref, kvn_ref, wk_ref, wv_ref,
     cbw_ref, cbb_ref, lng_ref, lnb_ref, vng_ref, vnb_ref, ws_ref, bs_ref, cdw_ref) = refs[:N_MIX_IN_INPUTS]
    rest = refs[N_MIX_IN_INPUTS + n_alias:]
    if prompt:
        q_ref, k_ref, vt_ref, y_ref, lat_ref, kr_ref, bt_ref, dt_ref, xpb_ref, xsh_ref, xpd_ref, cv_ref = rest
    else:
        q_ref, y_ref, lat_ref, kr_ref, bt_ref, dt_ref, vg_ref, xpb_ref, xsh_ref, xpd_ref, cv_ref = rest
    s = pl.program_id(1)
    x = x_ref[...]
    n = _rms(x, gpre_ref[...]).astype(BF16)
    ctab, sneg, spos = c_ref[...], sneg_ref[...], spos_ref[...]

    o_kv = q_lora
    o_b = o_kv + kv_lora
    o_c = o_b + 2 * c_b
    o_d = o_c + 2 * c_c
    o_kr = o_d + 3 * c_d

    cq = _dot(n, win_ref[:, 0:o_kv])
    qlat = _rms(cq, qn_ref[...]).astype(BF16)
    qa = _dot(qlat, wq_ref[...])
    for h in range(A_HEADS):
        qh = _rope128(qa[:, h * HEAD_PAD:(h + 1) * HEAD_PAD], ctab, sneg, spos) * Q_SCALE
        q_ref[:, h * HEAD_PAD:(h + 1) * HEAD_PAD] = qh.astype(BF16)

    ckv = _dot(n, win_ref[:, o_kv:o_b])
    lat = _rms(ckv, kvn_ref[...])
    lat_ref[...] = lat
    latb = lat.astype(BF16)
    kr = _rope128(_dot(n, win_ref[:, o_kr:o_kr + LANES]), ctab, sneg, spos)
    if prompt:
        kr_ref[...] = kr[:, 0:ROPE_DIM]
        kn = _dot(latb, wk_ref[...])
        for h in range(A_HEADS):
            k_ref[:, h * HEAD_PAD:(h + 1) * HEAD_PAD] = (kn[:, h * HEAD_PAD:(h + 1) * HEAD_PAD] + kr).astype(BF16)
        vt_ref[...] = _dot(latb, wv_ref[...]).T.astype(BF16)
    else:
        kr_ref[...] = kr

    glu = _dot(n, win_ref[:, o_b:o_c])
    xb = glu[:, 0:c_b] * jax.nn.sigmoid(glu[:, c_b:2 * c_b])

    @pl.when(s == 0)
    def _():
        xpb_ref[0:CONV_B_PAD, :] = stb_ref[...]
        xpd_ref[0:CONV_D_PAD, :] = std_ref[...]

    @pl.when(s > 0)
    def _():
        xpb_ref[0:CONV_B_PAD, :] = xpb_ref[tq:tq + CONV_B_PAD, :]
        xpd_ref[0:CONV_D_PAD, :] = xpd_ref[tq:tq + CONV_D_PAD, :]

    xpb_ref[CONV_B_PAD:CONV_B_PAD + tq, :] = xb
    bt_ref[...] = xpb_ref[tq:tq + CONV_B_PAD, :]
    n_sh = tq + CONV_B_PAD - SUBLANES
    for r in range(1, SUBLANES):
        xsh_ref[r, 0:n_sh, :] = xpb_ref[r:r + n_sh, :]
    off_b = CONV_B_PAD - (kb_w - 1)
    for blk in range(tq // rb):
        acc = jnp.broadcast_to(cbb_ref[...], (rb, c_b))
        for kk in range(kb_w):
            a8, r = divmod(off_b + kk, SUBLANES)
            row0 = blk * rb + a8 * SUBLANES
            src = xpb_ref[row0:row0 + rb, :] if r == 0 else xsh_ref[r, row0:row0 + rb, :]
            acc = acc + src * cbw_ref[kk:kk + 1, :]
        cv_ref[blk * rb:(blk + 1) * rb, :] = acc
    yb = _layernorm(cv_ref[...], lng_ref[...], lnb_ref[...])
    y_ref[:, 0:c_b] = (yb * jax.nn.sigmoid(yb)).astype(BF16)

    uv = _dot(n, win_ref[:, o_c:o_d])
    vg = _layernorm(uv[:, c_c:2 * c_c], vng_ref[...], vnb_ref[...])
    if not prompt:
        vg_ref[...] = vg
    vgb = vg.astype(BF16)
    ck = min(tq, GMLP_CHUNK)
    ri = lax.broadcasted_iota(jnp.int32, (ck, ck), 0) // CHUNK
    ci = lax.broadcasted_iota(jnp.int32, (ck, ck), 1) // CHUNK
    lane_grp = lax.broadcasted_iota(jnp.int32, (ck, c_c), 1) // (c_c // C_GROUPS)
    wmix = [jnp.where(ci <= ri, ws_ref[g, 0:ck, 0:ck], 0.0).astype(BF16) for g in range(C_GROUPS)]
    for c in range(tq // ck):
        vc = vgb[c * ck:(c + 1) * ck, :]
        mix = bs_ref[0:ck, :]
        for g in range(C_GROUPS):
            mix = mix + jnp.where(lane_grp == g, _dot(wmix[g], vc), 0.0)
        y_ref[c * ck:(c + 1) * ck, c_b:c_b + c_c] = (uv[c * ck:(c + 1) * ck, 0:c_c] * mix).astype(BF16)

    dp = _dot(n, win_ref[:, o_d:o_kr])
    xd = dp[:, c_d:2 * c_d] * dp[:, 2 * c_d:3 * c_d]
    xpd_ref[CONV_D_PAD:CONV_D_PAD + tq, :] = xd
    dt_ref[...] = xpd_ref[tq:tq + CONV_D_PAD, :]
    off_d = CONV_D_PAD - (kd_w - 1)
    conv = xd * cdw_ref[kd_w - 1:kd_w, :]
    for kk in range(kd_w - 1):
        conv = conv + xpd_ref[off_d + kk:off_d + kk + tq, :] * cdw_ref[kk:kk + 1, :]
    y_ref[:, c_b + c_c:c_b + c_c + c_d] = (dp[:, 0:c_d] * conv).astype(BF16)


def _mix_in(x, tabs, stb, std, st_layer, w, l, *, tq, prompt, stacked=None):
    b, s, d = x.shape
    depth = w["w_in1"].shape[0]
    q_lora, kv_lora = w["q_norm"].shape[2], w["kv_norm"].shape[2]
    c_b, c_c, c_d = w["conv_b_bias"].shape[2], w["gmlp_vn_g"].shape[2], w["conv_d_w"].shape[2]
    hp = A_HEADS * HEAD_PAD
    rb = min(tq, 64)
    tile = lambda wd: pl.BlockSpec((None, tq, wd), lambda i, j: (i, j, 0))
    per_b = lambda r, wd: pl.BlockSpec((None, r, wd), lambda i, j: (i, 0, 0))
    if st_layer is None:
        st_spec = per_b
    else:
        st_spec = lambda r, wd: pl.BlockSpec((None, None, r, wd), lambda i, j: (st_layer, i, 0, 0))
    tab = pl.BlockSpec((tq, LANES), lambda i, j: (j, 0))
    weights = [w[k] for k in MIX_IN_WEIGHTS]
    sds = jax.ShapeDtypeStruct
    small = [
        (sds((b, CONV_B_PAD, c_b), F32), per_b(CONV_B_PAD, c_b)),
        (sds((b, CONV_D_PAD, c_d), F32), per_b(CONV_D_PAD, c_d)),
    ]
    outs = [(sds((b, s, hp), BF16), tile(hp))]
    aliases = {}
    alias_in, alias_specs = [], []
    if prompt:
        vt_spec = pl.BlockSpec((None, None, A_HEADS * V_DIM, tq), lambda i, j: (i, j, 0, 0))
        stk = lambda wd: pl.BlockSpec((None, None, tq, wd), lambda i, j: (l, i, j, 0))
        outs += [(sds((b, s, hp), BF16), tile(hp)),
                 (sds((b, s // tq, A_HEADS * V_DIM, tq), BF16), vt_spec),
                 (sds((b, s, c_b + c_c + c_d), BF16), tile(c_b + c_c + c_d)),
                 (sds((depth, b, s, kv_lora), F32), stk(kv_lora)),
                 (sds((depth, b, s, ROPE_DIM), F32), stk(ROPE_DIM))]
        outs += small
        if stacked is not None:
            alias_in = list(stacked)
            alias_specs = [pl.BlockSpec(memory_space=pl.ANY)] * 2
            aliases = {N_MIX_IN_INPUTS: 4, N_MIX_IN_INPUTS + 1: 5}
    else:
        outs += [(sds((b, s, c_b + c_c + c_d), BF16), tile(c_b + c_c + c_d)),
                 (sds((b, s, kv_lora), F32), tile(kv_lora)),
                 (sds((b, s, LANES), F32), tile(LANES))]
        outs += small + [(sds((b, s, c_c), F32), tile(c_c))]
    out_shape, out_specs = zip(*outs)
    return pl.pallas_call(
        functools.partial(_mix_in_kernel, prompt=prompt, n_alias=len(alias_in), tq=tq, q_lora=q_lora,
                          kv_lora=kv_lora, c_b=c_b, c_c=c_c, c_d=c_d, kb_w=w["kb_w"], kd_w=w["kd_w"], rb=rb),
        grid=(b, s // tq),
        in_specs=[tile(d), tab, tab, tab, st_spec(CONV_B_PAD, c_b), st_spec(CONV_D_PAD, c_d)]
        + [_layer(a, l) for a in weights] + alias_specs,
        out_specs=out_specs,
        out_shape=out_shape,
        input_output_aliases=aliases,
        scratch_shapes=[pltpu.VMEM((CONV_B_PAD + tq, c_b), F32),
                        pltpu.VMEM((SUBLANES, CONV_B_PAD + tq, c_b), F32),
                        pltpu.VMEM((CONV_D_PAD + tq, c_d), F32),
                        pltpu.VMEM((tq, c_b), F32)],
        name="mix_in",
        compiler_params=_params(("arbitrary", "arbitrary")),
    )(x, *tabs, stb, std, *weights, *alias_in)


def _attn_kernel(q_ref, k_ref, vt_ref, o_ref, *, tq):
    qi = pl.program_id(1)
    kc = lax.broadcasted_iota(jnp.int32, (tq, tq), 0) // CHUNK
    qc = lax.broadcasted_iota(jnp.int32, (tq, tq), 1) // CHUNK
    diag_ok = kc <= qc
    heads = tuple(range(A_HEADS))
    qs = [q_ref[:, h * HEAD_PAD:(h + 1) * HEAD_PAD] for h in heads]

    def step(j, carry, masked):
        k0 = pl.multiple_of(j * tq, tq)
        sts = [_dot_nt(k_ref[pl.ds(k0, tq), h * HEAD_PAD:(h + 1) * HEAD_PAD], q)
               for h, q in zip(heads, qs)]
        mid = []
        for (m, l, acc), st in zip(carry, sts):
            if masked:
                st = jnp.where(diag_ok, st, NEG_BIG)
            m_new = jnp.maximum(m, jnp.max(st, axis=0, keepdims=True))
            alpha = jnp.exp2(m - m_new)
            p = jnp.exp2(st - m_new)
            l = alpha * l + jnp.sum(p, axis=0, keepdims=True)
            mid.append((m_new, l, alpha, p.astype(BF16)))
        new = []
        for (m_new, l, alpha, pb), (_, _, acc), h in zip(mid, carry, heads):
            acc = alpha * acc + _dot(vt_ref[j, h * V_DIM:(h + 1) * V_DIM, :], pb)
            new.append((m_new, l, acc))
        return tuple(new)

    init = tuple((jnp.full((1, tq), NEG_BIG, F32), jnp.zeros((1, tq), F32), jnp.zeros((V_DIM, tq), F32))
                 for _ in heads)
    carry = lax.fori_loop(0, qi, functools.partial(step, masked=False), init)
    carry = step(qi, carry, True)
    ot = jnp.concatenate([acc / l for (_, l, acc) in carry], axis=0)
    o_ref[...] = ot.T.astype(BF16)


def _attn_prompt(q, k, vt, *, tq):
    b, s, hp = q.shape
    nblk, wv = vt.shape[1], vt.shape[2]
    return pl.pallas_call(
        functools.partial(_attn_kernel, tq=tq),
        grid=(b, s // tq),
        in_specs=[
            pl.BlockSpec((None, tq, hp), lambda i, j: (i, j, 0)),
            pl.BlockSpec((None, s, hp), lambda i, j: (i, 0, 0)),
            pl.BlockSpec((None, nblk, wv, tq), lambda i, j: (i, 0, 0, 0)),
        ],
        out_specs=pl.BlockSpec((None, tq, wv), lambda i, j: (i, j, 0)),
        out_shape=jax.ShapeDtypeStruct((b, s, wv), BF16),
        name="attn_prompt",
        compiler_params=_params(("parallel", "parallel")),
    )(q, k, vt)


def _attn_sample_kernel(q_ref, clat_ref, ckr_ref, lat_ref, kr_ref, wkt_ref, wv_ref, o_ref, kn_ref, *, t):
    q = q_ref[...]
    qh = [q[:, h * HEAD_PAD:(h + 1) * HEAD_PAD] for h in range(A_HEADS)]
    qabs = jnp.concatenate(
        [_dot(qh[h], wkt_ref[h * HEAD_PAD:(h + 1) * HEAD_PAD, :]) for h in range(A_HEADS)], axis=0)
    qcat = jnp.concatenate([qabs.astype(BF16), jnp.concatenate(qh, axis=0)], axis=1)
    clat = clat_ref[...].astype(BF16)
    kv_lora = clat.shape[1]
    kc = jnp.concatenate([clat, ckr_ref[...]], axis=1)
    kn_ref[...] = jnp.zeros(kn_ref.shape, BF16)
    kn_ref[0:t, :] = jnp.concatenate([lat_ref[...], kr_ref[...]], axis=1).astype(BF16)
    kn = kn_ref[...]
    s1 = _dot_nt(qcat, kc)
    s2 = _dot_nt(qcat, kn)
    s2 = jnp.where(lax.broadcasted_iota(jnp.int32, s2.shape, 1) < t, s2, NEG_BIG)
    m = jnp.maximum(jnp.max(s1, axis=-1, keepdims=True), jnp.max(s2, axis=-1, keepdims=True))
    p1 = jnp.exp2(s1 - m)
    p2 = jnp.exp2(s2 - m)
    l = jnp.sum(p1, axis=-1, keepdims=True) + jnp.sum(p2, axis=-1, keepdims=True)
    olat = ((_dot(p1.astype(BF16), clat) + _dot(p2.astype(BF16), kn[:, 0:kv_lora])) / l).astype(BF16)
    low_half = lax.broadcasted_iota(jnp.int32, (t, LANES), 1) < V_DIM
    for pair in range(A_HEADS // 2):
        wpair = wv_ref[:, pair * LANES:(pair + 1) * LANES]
        lo = _dot(olat[(2 * pair) * t:(2 * pair + 1) * t, :], wpair)
        hi = _dot(olat[(2 * pair + 1) * t:(2 * pair + 2) * t, :], wpair)
        o_ref[:, pair * LANES:(pair + 1) * LANES] = jnp.where(low_half, lo, hi).astype(BF16)


def _attn_sample(q, cache_lat, cache_kr, lat, kr, wkt, wv, l):
    b, t, hp = q.shape
    past, kv_lora = cache_lat.shape[2], cache_lat.shape[3]
    wvw = wv.shape[2]
    per_b = lambda r, w: pl.BlockSpec((None, r, w), lambda i: (i, 0, 0))
    cache = lambda r, w: pl.BlockSpec((None, None, r, w), lambda i: (l, i, 0, 0))
    return pl.pallas_call(
        functools.partial(_attn_sample_kernel, t=t),
        grid=(b,),
        in_specs=[per_b(t, hp), cache(past, kv_lora), cache(past, LANES), per_b(t, kv_lora),
                  per_b(t, LANES), _layer(wkt, l), _layer(wv, l)],
        out_specs=per_b(t, wvw),
        out_shape=jax.ShapeDtypeStruct((b, t, wvw), BF16),
        scratch_shapes=[pltpu.VMEM((LANES, kv_lora + LANES), BF16)],
        name="attn_sample",
        compiler_params=_params(("parallel",)),
    )(q, cache_lat, cache_kr, lat, kr, wkt, wv)


MIX_OUT_WEIGHTS = ("mix_norm_pre", "mix_norm_post", "w_g", "w_br_a", "w_br_b", "w_br_c", "w_br_d", "w_o")


def _mix_out_kernel(x_ref, a_ref, y_ref, gpre_ref, gpost_ref, wg_ref, wba_ref, wbb_ref, wbc_ref, wbd_ref,
                    wo_ref, o_ref, *, d, c_b, c_c, c_d):
    x = x_ref[...]
    n = _rms(x, gpre_ref[...]).astype(BF16)
    branches = (
        (a_ref[...], wba_ref),
        (y_ref[:, 0:c_b], wbb_ref),
        (y_ref[:, c_b:c_b + c_c], wbc_ref),
        (y_ref[:, c_b + c_c:c_b + c_c + c_d], wbd_ref),
    )
    merged = jnp.zeros(x.shape, F32)
    for i, (br, w_ref) in enumerate(branches):
        gate = jax.nn.sigmoid(_dot(n, wg_ref[:, i * d:(i + 1) * d]))
        merged = merged + gate * _dot(br, w_ref[...])
    out = _dot(merged.astype(BF16), wo_ref[...])
    o_ref[...] = x + _rms(out, gpost_ref[...])


def _mix_out(x, a, y, w, l):
    n, d = x.shape
    c_b, c_c, c_d = w["w_br_b"].shape[1], w["w_br_c"].shape[1], w["w_br_d"].shape[1]
    tm = _pick_tile(n, (512, 256, 128, 64, 32, 16, 8))
    weights = [w[k] for k in MIX_OUT_WEIGHTS]
    row = lambda wd: pl.BlockSpec((tm, wd), lambda i: (i, 0))
    return pl.pallas_call(
        functools.partial(_mix_out_kernel, d=d, c_b=c_b, c_c=c_c, c_d=c_d),
        grid=(n // tm,),
        in_specs=[row(d), row(a.shape[1]), row(y.shape[1])] + [_layer(a_, l) for a_ in weights],
        out_specs=row(d),
        out_shape=jax.ShapeDtypeStruct((n, d), F32),
        name="mix_out",
        compiler_params=_params(("parallel",)),
    )(x, a, y, *weights)


def _rope_tabs(pos):
    half = ROPE_DIM // 2
    inv = jnp.exp(-math.log(ROPE_THETA) * jnp.arange(half, dtype=F32) / half)
    ang = pos.astype(F32)[:, None] * inv[None, :]
    cos, sin = jnp.cos(ang), jnp.sin(ang)
    n = pos.shape[0]
    z = lambda w: jnp.zeros((n, w), F32)
    ctab = jnp.concatenate([cos, cos, jnp.ones((n, NOPE_DIM), F32), z(LANES - ROPE_DIM - NOPE_DIM)], axis=1)
    sneg = jnp.concatenate([-sin, z(LANES - half)], axis=1)
    spos = jnp.concatenate([z(half), sin, z(LANES - 2 * half)], axis=1)
    return ctab, sneg, spos


def _pad_rows(w, rows):
    return jnp.pad(w, ((0, 0), (0, rows - w.shape[1]), (0, 0)))


def kernel(x_prompt, x_sample, cache_kv_latent, cache_k_rope, state_conv_b, state_conv_d, ffn1_norm_pre, ffn1_norm_post, ffn1_w_gu, ffn1_w_down, mix_norm_pre, mix_norm_post, w_in, q_norm, w_uq, kv_norm, w_ukv, conv_b_w, conv_b_bias, conv_b_ln_g, conv_b_ln_b, gmlp_vn_g, gmlp_vn_b, gmlp_w_s, gmlp_b_s, conv_d_w, w_br_a, w_br_b, w_br_c, w_br_d, w_o, ffn2_norm_pre, ffn2_norm_post, ffn2_w_gu, ffn2_w_down):
    b, s, d = x_prompt.shape
    bs, t, _ = x_sample.shape
    depth = w_in.shape[0]
    past = cache_kv_latent.shape[2]
    q_lora, kv_lora = q_norm.shape[1], kv_norm.shape[1]
    c_b, c_c, c_d = conv_b_bias.shape[1], gmlp_vn_g.shape[1], conv_d_w.shape[2]
    kb_w, kd_w = conv_b_w.shape[1], conv_d_w.shape[1]
    assert kb_w - 1 <= CONV_B_PAD and kd_w - 1 <= CONV_D_PAD
    assert s % GMLP_CHUNK == 0 and t <= GMLP_CHUNK and t % 16 == 0 and past % CHUNK == 0 and t <= CHUNK
    assert gmlp_w_s.shape[1] == C_GROUPS and w_ukv.shape[2] == A_HEADS * (NOPE_DIM + V_DIM)

    o_q = q_lora
    o_kv = o_q + kv_lora
    o_kr = o_kv + ROPE_DIM
    o_b = o_kr + 2 * c_b
    o_c = o_b + 2 * c_c
    o_d = o_c + 3 * c_d

    vec = lambda p: p[:, None, :]
    uq = w_uq.reshape(depth, q_lora, A_HEADS, NOPE_DIM + ROPE_DIM)
    ukv = w_ukv.reshape(depth, kv_lora, A_HEADS, NOPE_DIM + V_DIM)
    head_zeros = lambda rows, width: jnp.zeros((depth, rows, A_HEADS, width), F32)
    wk_pad = jnp.concatenate(
        [head_zeros(kv_lora, ROPE_DIM), ukv[..., :NOPE_DIM], head_zeros(kv_lora, HEAD_PAD - NOPE_DIM - ROPE_DIM)],
        axis=3).reshape(depth, kv_lora, A_HEADS * HEAD_PAD).astype(BF16)
    w = dict(
        mix_norm_pre=vec(mix_norm_pre), mix_norm_post=vec(mix_norm_post),
        w_in1=jnp.concatenate(
            [w_in[:, :, 0:o_kv].astype(BF16), w_in[:, :, o_kr:o_d].astype(BF16), w_in[:, :, o_kv:o_kr].astype(BF16),
             jnp.zeros((depth, d, LANES - ROPE_DIM), BF16)], axis=2),
        w_g=w_in[:, :, o_d:].astype(BF16),
        q_norm=vec(q_norm),
        wq_pad=jnp.concatenate(
            [uq[..., NOPE_DIM:], uq[..., :NOPE_DIM], head_zeros(q_lora, HEAD_PAD - NOPE_DIM - ROPE_DIM)],
            axis=3).reshape(depth, q_lora, A_HEADS * HEAD_PAD).astype(BF16),
        kv_norm=vec(kv_norm), wk_pad=wk_pad,
        wv=ukv[..., NOPE_DIM:].reshape(depth, kv_lora, A_HEADS * V_DIM).astype(BF16),
        conv_b_w=_pad_rows(conv_b_w, CONV_B_PAD), conv_b_bias=vec(conv_b_bias),
        conv_b_ln_g=vec(conv_b_ln_g), conv_b_ln_b=vec(conv_b_ln_b),
        gmlp_vn_g=vec(gmlp_vn_g), gmlp_vn_b=vec(gmlp_vn_b), gmlp_w_s=gmlp_w_s,
        bs_full=jnp.repeat(jnp.swapaxes(gmlp_b_s, 1, 2), c_c // C_GROUPS, axis=2),
        conv_d_w=_pad_rows(conv_d_w, CONV_D_PAD), kb_w=kb_w, kd_w=kd_w,
        w_br_a=w_br_a.astype(BF16), w_br_b=w_br_b.astype(BF16), w_br_c=w_br_c.astype(BF16),
        w_br_d=w_br_d.astype(BF16), w_o=w_o.astype(BF16))
    wk_t = jnp.swapaxes(wk_pad, 1, 2)
    f1 = (vec(ffn1_norm_pre), vec(ffn1_norm_post), ffn1_w_gu.astype(BF16), ffn1_w_down.astype(BF16))
    f2 = (vec(ffn2_norm_pre), vec(ffn2_norm_post), ffn2_w_gu.astype(BF16), ffn2_w_down.astype(BF16))

    tabs_p = _rope_tabs(jnp.arange(s))
    tabs_s = _rope_tabs(past + jnp.arange(t))
    zero_b = jnp.zeros((b, CONV_B_PAD, c_b), F32)
    zero_d = jnp.zeros((b, CONV_D_PAD, c_d), F32)
    tail_b = CONV_B_PAD - (kb_w - 1)
    tail_d = CONV_D_PAD - (kd_w - 1)
    st_b = jnp.pad(state_conv_b, ((0, 0), (0, 0), (tail_b, 0), (0, 0)))
    st_d = jnp.pad(state_conv_d, ((0, 0), (0, 0), (tail_d, 0), (0, 0)))
    ckr_pad = jnp.pad(cache_k_rope, ((0, 0), (0, 0), (0, 0), (0, LANES - ROPE_DIM))).astype(BF16)
    tq = _pick_tile(s, (256, 128))

    xp = x_prompt.reshape(b * s, d)
    xs = x_sample.reshape(bs * t, d)
    stacked = None
    outs = [[] for _ in range(7)]
    for l in range(depth):
        xp = _ffn(xp, *f1, l)
        xs = _ffn(xs, *f1, l)

        qp, kp, vtp, yp, lat_all, kr_all, btp, dtp = _mix_in(
            xp.reshape(b, s, d), tabs_p, zero_b, zero_d, None, w, l, tq=tq, prompt=True, stacked=stacked)
        stacked = (lat_all, kr_all)
        ap = _attn_prompt(qp, kp, vtp, tq=tq)
        xp = _mix_out(xp, ap.reshape(b * s, -1), yp.reshape(b * s, -1), w, l)

        qs, ys, lats, krs, bts, dts, vgs = _mix_in(
            xs.reshape(bs, t, d), tabs_s, st_b, st_d, l, w, l, tq=t, prompt=False)
        a_s = _attn_sample(qs, cache_kv_latent, ckr_pad, lats, krs, wk_t, w["wv"], l)
        xs = _mix_out(xs, a_s.reshape(bs * t, -1), ys.reshape(bs * t, -1), w, l)

        xp = _ffn(xp, *f2, l)
        xs = _ffn(xs, *f2, l)

        for lst, val in zip(outs, (btp[:, tail_b:], dtp[:, tail_d:], lats, krs[..., :ROPE_DIM],
                                   bts[:, tail_b:], vgs, dts[:, tail_d:])):
            lst.append(val)
    cb_p, cd_p, lat_s, kr_s, cb_s, vc_s, cd_s = (jnp.stack(o) for o in outs)
    return (xp.reshape(b, s, d), xs.reshape(bs, t, d), stacked[0], stacked[1], cb_p, cd_p,
            lat_s, kr_s, cb_s, vc_s, cd_s)
```

```python
import functools
import math

import jax
import jax.numpy as jnp
from jax import lax
from jax.experimental import pallas as pl
from jax.experimental.pallas import tpu as pltpu

F32 = jnp.float32
BF16 = jnp.bfloat16

CHUNK = 64
A_HEADS = 8
NOPE_DIM = 64
ROPE_DIM = 32
V_DIM = 64
ROPE_THETA = 10000.0
SM_SCALE = (NOPE_DIM + ROPE_DIM) ** -0.5
Q_SCALE = SM_SCALE * math.log2(math.e)
C_GROUPS = 4
GMLP_CHUNK = 128
N_BRANCH = 4
EPS = 1e-6

LANES = 128
SUBLANES = 8
BF16_ROWS = 16
HEAD_PAD = 128
CONV_B_PAD = 32
CONV_D_PAD = 8
VMEM_LIMIT_BYTES = 56 * 1024 * 1024
NEG_BIG = -1e30
ROW_TILES = (1024, 512, 256, 128, 64, 32, 16, 8)


def _pick_tile(n, candidates):
    for c in candidates:
        if n % c == 0:
            return c
    return n


def _layer(arr, l):
    nd = arr.ndim - 1
    return pl.BlockSpec((None,) + arr.shape[1:], lambda *_: (l,) + (0,) * nd, pipeline_mode=pl.Buffered(1))


def _whole(arr):
    nd = arr.ndim
    return pl.BlockSpec(arr.shape, lambda *_: (0,) * nd, pipeline_mode=pl.Buffered(1))


def _rms(x, g):
    ms = jnp.mean(x * x, axis=-1, keepdims=True)
    return x * lax.rsqrt(ms + EPS) * g


def _layernorm(x, g, b):
    mu = jnp.mean(x, axis=-1, keepdims=True)
    xc = x - mu
    var = jnp.mean(xc * xc, axis=-1, keepdims=True)
    return xc * lax.rsqrt(var + EPS) * g + b


def _dot(a, b):
    return jnp.dot(a, b, preferred_element_type=F32)


def _dot_nt(a, b):
    return lax.dot_general(a, b, (((1,), (1,)), ((), ())), preferred_element_type=F32)


def _params(sem):
    return pltpu.CompilerParams(dimension_semantics=sem, vmem_limit_bytes=VMEM_LIMIT_BYTES)


def _ffn_kernel(x_ref, gpre_ref, gpost_ref, wgu_ref, wd_ref, o_ref, act_ref, *, d_ff, tf):
    x = x_ref[...]
    n = _rms(x, gpre_ref[...]).astype(BF16)
    for c in range(d_ff // tf):
        gate = _dot(n, wgu_ref[:, c * tf:(c + 1) * tf])
        up = _dot(n, wgu_ref[:, d_ff + c * tf:d_ff + (c + 1) * tf])
        act_ref[:, c * tf:(c + 1) * tf] = (gate * jax.nn.sigmoid(gate) * up).astype(BF16)
    y = _dot(act_ref[...], wd_ref[...])
    o_ref[...] = x + 0.5 * _rms(y, gpost_ref[...])


def _ffn(x, gpre, gpost, wgu, wd, l):
    n, d = x.shape
    d_ff = wd.shape[1]
    tm = _pick_tile(n, ROW_TILES)
    tf = _pick_tile(d_ff, (256, 128))
    weights = (gpre, gpost, wgu, wd)
    return pl.pallas_call(
        functools.partial(_ffn_kernel, d_ff=d_ff, tf=tf),
        grid=(n // tm,),
        in_specs=[pl.BlockSpec((tm, d), lambda i: (i, 0))] + [_layer(w, l) for w in weights],
        out_specs=pl.BlockSpec((tm, d), lambda i: (i, 0)),
        out_shape=jax.ShapeDtypeStruct((n, d), F32),
        scratch_shapes=[pltpu.VMEM((tm, d_ff), BF16)],
        name="ffn",
        compiler_params=_params(("parallel",)),
    )(x, *weights)


def _rope128(a, c, sneg, spos):
    return (a * c + pltpu.roll(a, LANES - ROPE_DIM // 2, 1) * sneg
            + pltpu.roll(a, ROPE_DIM // 2, 1) * spos)


MIX_IN_WEIGHTS = ("mix_norm_pre", "w_in1", "q_norm", "wq_pad", "kv_norm", "wk_pad", "wv", "conv_b_w",
                  "conv_b_bias", "conv_b_ln_g", "conv_b_ln_b", "gmlp_vn_g", "gmlp_vn_b", "gmlp_w_s", "bs_full",
                  "conv_d_w")
N_MIX_IN_INPUTS = 6 + len(MIX_IN_WEIGHTS)


def _mix_in_kernel(*refs, prompt, n_alias, tq, q_lora, kv_lora, c_b, c_c, c_d, kb_w, kd_w, rb):
    (x_ref, c_ref, sneg_ref, spos_ref, stb_ref, std_ref,
     gpre_ref, win_ref, qn_ref, wq_ref, kvn_ref, wk_ref, wv_ref,
     cbw_ref, cbb_ref, lng_ref, lnb_ref, vng_ref, vnb_ref, ws_ref, bs_ref, cdw_ref) = refs[:N_MIX_IN_INPUTS]
    rest = refs[N_MIX_IN_INPUTS + n_alias:]
    if prompt:
        q_ref, k_ref, vt_ref, y_ref, lat_ref, kr_ref, bt_ref, dt_ref, xpb_ref, xsh_ref, xpd_ref = rest
    else:
        q_ref, y_ref, lat_ref, kr_ref, bt_ref, dt_ref, vg_ref, xpb_ref, xsh_ref, xpd_ref = rest
    s = pl.program_id(1)
    x = x_ref[...]
    n = _rms(x, gpre_ref[...]).astype(BF16)
    ctab, sneg, spos = c_ref[...], sneg_ref[...], spos_ref[...]

    o_kv = q_lora
    o_b = o_kv + kv_lora
    o_c = o_b + 2 * c_b
    o_d = o_c + 2 * c_c
    o_kr = o_d + 3 * c_d

    def stage_q():
        cq = _dot(n, win_ref[:, 0:o_kv])
        qlat = _rms(cq, qn_ref[...]).astype(BF16)
        qa = _dot(qlat, wq_ref[...])
        for h in range(A_HEADS):
            qh = _rope128(qa[:, h * HEAD_PAD:(h + 1) * HEAD_PAD], ctab, sneg, spos) * Q_SCALE
            q_ref[:, h * HEAD_PAD:(h + 1) * HEAD_PAD] = qh.astype(BF16)

    def stage_kv():
        ckv = _dot(n, win_ref[:, o_kv:o_b])
        krd = _dot(n, win_ref[:, o_kr:o_kr + LANES])
        lat = _rms(ckv, kvn_ref[...])
        lat_ref[...] = lat
        latb = lat.astype(BF16)
        kr = _rope128(krd, ctab, sneg, spos)
        if prompt:
            kr_ref[...] = kr[:, 0:ROPE_DIM]
            kn = _dot(latb, wk_ref[...])
            for h in range(A_HEADS):
                k_ref[:, h * HEAD_PAD:(h + 1) * HEAD_PAD] = (
                    kn[:, h * HEAD_PAD:(h + 1) * HEAD_PAD] + kr).astype(BF16)
            vt_ref[...] = _dot(latb, wv_ref[...]).T.astype(BF16)
        else:
            kr_ref[...] = kr

    def stage_conv_b_fill():
        glu = _dot(n, win_ref[:, o_b:o_c])
        xb = glu[:, 0:c_b] * jax.nn.sigmoid(glu[:, c_b:2 * c_b])

        @pl.when(s == 0)
        def _():
            xpb_ref[0:CONV_B_PAD, :] = stb_ref[...]
            xpd_ref[0:CONV_D_PAD, :] = std_ref[...]

        @pl.when(s > 0)
        def _():
            xpb_ref[0:CONV_B_PAD, :] = xpb_ref[tq:tq + CONV_B_PAD, :]
            xpd_ref[0:CONV_D_PAD, :] = xpd_ref[tq:tq + CONV_D_PAD, :]

        xpb_ref[CONV_B_PAD:CONV_B_PAD + tq, :] = xb
        bt_ref[...] = xpb_ref[tq:tq + CONV_B_PAD, :]
        n_sh = tq + CONV_B_PAD - SUBLANES
        for r in range(1, SUBLANES):
            xsh_ref[r, 0:n_sh, :] = xpb_ref[r:r + n_sh, :]

    def stage_conv_b_block(blk):
        off_b = CONV_B_PAD - (kb_w - 1)
        acc = jnp.broadcast_to(cbb_ref[...], (rb, c_b))
        for kk in range(kb_w):
            a8, r = divmod(off_b + kk, SUBLANES)
            row0 = blk * rb + a8 * SUBLANES
            src = xpb_ref[row0:row0 + rb, :] if r == 0 else xsh_ref[r, row0:row0 + rb, :]
            acc = acc + src * cbw_ref[kk:kk + 1, :]
        yb = _layernorm(acc, lng_ref[...], lnb_ref[...])
        y_ref[blk * rb:(blk + 1) * rb, 0:c_b] = (yb * jax.nn.sigmoid(yb)).astype(BF16)

    def stage_gmlp():
        uv = _dot(n, win_ref[:, o_c:o_d])
        vg = _layernorm(uv[:, c_c:2 * c_c], vng_ref[...], vnb_ref[...])
        if not prompt:
            vg_ref[...] = vg
        vgb = vg.astype(BF16)
        ck = min(tq, GMLP_CHUNK)
        ri = lax.broadcasted_iota(jnp.int32, (ck, ck), 0) // CHUNK
        ci = lax.broadcasted_iota(jnp.int32, (ck, ck), 1) // CHUNK
        lane_grp = lax.broadcasted_iota(jnp.int32, (ck, c_c), 1) // (c_c // C_GROUPS)
        wmix = [jnp.where(ci <= ri, ws_ref[g, 0:ck, 0:ck], 0.0).astype(BF16) for g in range(C_GROUPS)]
        for c in range(tq // ck):
            vc = vgb[c * ck:(c + 1) * ck, :]
            mix = bs_ref[0:ck, :]
            for g in range(C_GROUPS):
                mix = mix + jnp.where(lane_grp == g, _dot(wmix[g], vc), 0.0)
            y_ref[c * ck:(c + 1) * ck, c_b:c_b + c_c] = (uv[c * ck:(c + 1) * ck, 0:c_c] * mix).astype(BF16)

    def stage_short_conv():
        dp = _dot(n, win_ref[:, o_d:o_kr])
        xd = dp[:, c_d:2 * c_d] * dp[:, 2 * c_d:3 * c_d]
        xpd_ref[CONV_D_PAD:CONV_D_PAD + tq, :] = xd
        dt_ref[...] = xpd_ref[tq:tq + CONV_D_PAD, :]
        off_d = CONV_D_PAD - (kd_w - 1)
        conv = xd * cdw_ref[kd_w - 1:kd_w, :]
        for kk in range(kd_w - 1):
            conv = conv + xpd_ref[off_d + kk:off_d + kk + tq, :] * cdw_ref[kk:kk + 1, :]
        y_ref[:, c_b + c_c:c_b + c_c + c_d] = (dp[:, 0:c_d] * conv).astype(BF16)

    stage_q()
    stage_kv()
    stage_conv_b_fill()
    for blk in range(tq // rb):
        stage_conv_b_block(blk)
    stage_gmlp()
    stage_short_conv()


def _mix_in(x, tabs, stb, std, st_layer, w, l, *, tq, prompt, stacked=None):
    b, s, d = x.shape
    depth = w["w_in1"].shape[0]
    q_lora, kv_lora = w["q_norm"].shape[2], w["kv_norm"].shape[2]
    c_b, c_c, c_d = w["conv_b_bias"].shape[2], w["gmlp_vn_g"].shape[2], w["conv_d_w"].shape[2]
    hp = A_HEADS * HEAD_PAD
    rb = min(tq, 64)
    tile = lambda wd: pl.BlockSpec((None, tq, wd), lambda i, j: (i, j, 0))
    per_b = lambda r, wd: pl.BlockSpec((None, r, wd), lambda i, j: (i, 0, 0))
    if st_layer is None:
        st_spec = per_b
    else:
        st_spec = lambda r, wd: pl.BlockSpec((None, None, r, wd), lambda i, j: (st_layer, i, 0, 0))
    tab = pl.BlockSpec((tq, LANES), lambda i, j: (j, 0))
    weights = [w[k] for k in MIX_IN_WEIGHTS]
    sds = jax.ShapeDtypeStruct
    small = [
        (sds((b, CONV_B_PAD, c_b), F32), per_b(CONV_B_PAD, c_b)),
        (sds((b, CONV_D_PAD, c_d), F32), per_b(CONV_D_PAD, c_d)),
    ]
    outs = [(sds((b, s, hp), BF16), tile(hp))]
    aliases = {}
    alias_in, alias_specs = [], []
    if prompt:
        vt_spec = pl.BlockSpec((None, None, A_HEADS * V_DIM, tq), lambda i, j: (i, j, 0, 0))
        stk = lambda wd: pl.BlockSpec((None, None, tq, wd), lambda i, j: (l, i, j, 0))
        outs += [(sds((b, s, hp), BF16), tile(hp)),
                 (sds((b, s // tq, A_HEADS * V_DIM, tq), BF16), vt_spec),
                 (sds((b, s, c_b + c_c + c_d), BF16), tile(c_b + c_c + c_d)),
                 (sds((depth, b, s, kv_lora), F32), stk(kv_lora)),
                 (sds((depth, b, s, ROPE_DIM), F32), stk(ROPE_DIM))]
        outs += small
        if stacked is not None:
            alias_in = list(stacked)
            alias_specs = [pl.BlockSpec(memory_space=pl.ANY)] * 2
            aliases = {N_MIX_IN_INPUTS: 4, N_MIX_IN_INPUTS + 1: 5}
    else:
        outs += [(sds((b, s, c_b + c_c + c_d), BF16), tile(c_b + c_c + c_d)),
                 (sds((b, s, kv_lora), F32), tile(kv_lora)),
                 (sds((b, s, LANES), F32), tile(LANES))]
        outs += small + [(sds((b, s, c_c), F32), tile(c_c))]
    out_shape, out_specs = zip(*outs)
    return pl.pallas_call(
        functools.partial(_mix_in_kernel, prompt=prompt, n_alias=len(alias_in), tq=tq, q_lora=q_lora,
                          kv_lora=kv_lora, c_b=c_b, c_c=c_c, c_d=c_d, kb_w=w["kb_w"], kd_w=w["kd_w"], rb=rb),
        grid=(b, s // tq),
        in_specs=[tile(d), tab, tab, tab, st_spec(CONV_B_PAD, c_b), st_spec(CONV_D_PAD, c_d)]
        + [_layer(a, l) for a in weights] + alias_specs,
        out_specs=out_specs,
        out_shape=out_shape,
        input_output_aliases=aliases,
        scratch_shapes=[pltpu.VMEM((CONV_B_PAD + tq, c_b), F32),
                        pltpu.VMEM((SUBLANES, CONV_B_PAD + tq, c_b), F32),
                        pltpu.VMEM((CONV_D_PAD + tq, c_d), F32)],
        name="mix_in",
        compiler_params=_params(("arbitrary", "arbitrary")),
    )(x, *tabs, stb, std, *weights, *alias_in)


def _attn_kernel(q_ref, k_ref, vt_ref, o_ref, st0_ref, st1_ref, m_ref, acc_ref, *, tq):
    qi = pl.program_id(1)
    kc = lax.broadcasted_iota(jnp.int32, (tq, tq), 0) // CHUNK
    qc = lax.broadcasted_iota(jnp.int32, (tq, tq), 1) // CHUNK
    diag_ok = kc <= qc
    heads = tuple(range(A_HEADS))
    ones = jnp.ones((BF16_ROWS, tq), BF16)

    def scores(j, st_ref):
        k0 = pl.multiple_of(j * tq, tq)
        for h in heads:
            st_ref[h] = _dot_nt(k_ref[pl.ds(k0, tq), h * HEAD_PAD:(h + 1) * HEAD_PAD],
                                q_ref[:, h * HEAD_PAD:(h + 1) * HEAD_PAD])

    def update(j, st_ref, masked):
        for h in heads:
            st = st_ref[h]
            if masked:
                st = jnp.where(diag_ok, st, NEG_BIG)
            m = m_ref[h]
            m_new = jnp.maximum(m, jnp.max(st, axis=0, keepdims=True))
            alpha = jnp.exp2(m - m_new)
            pb = jnp.exp2((st - m_new).astype(BF16))
            vt1 = jnp.concatenate([vt_ref[j, h * V_DIM:(h + 1) * V_DIM, :], ones], axis=0)
            m_ref[h] = m_new
            acc_ref[h] = alpha * acc_ref[h] + _dot(vt1, pb)

    m_ref[...] = jnp.full(m_ref.shape, NEG_BIG, F32)
    acc_ref[...] = jnp.zeros(acc_ref.shape, F32)
    scores(0, st0_ref)

    def pair(p, _):
        j = 2 * p
        scores(j + 1, st1_ref)
        update(j, st0_ref, False)
        scores(j + 2, st0_ref)
        update(j + 1, st1_ref, False)
        return 0

    lax.fori_loop(0, qi // 2, pair, 0)

    @pl.when(qi % 2 == 0)
    def _():
        update(qi, st0_ref, True)

    @pl.when(qi % 2 == 1)
    def _():
        scores(qi, st1_ref)
        update(qi - 1, st0_ref, False)
        update(qi, st1_ref, True)

    ot = jnp.concatenate([acc_ref[h, 0:V_DIM, :] / acc_ref[h, V_DIM:V_DIM + 1, :] for h in heads], axis=0)
    o_ref[...] = ot.T.astype(BF16)


def _attn_prompt(q, k, vt, *, tq):
    b, s, hp = q.shape
    nblk, wv = vt.shape[1], vt.shape[2]
    return pl.pallas_call(
        functools.partial(_attn_kernel, tq=tq),
        grid=(b, s // tq),
        in_specs=[
            pl.BlockSpec((None, tq, hp), lambda i, j: (i, j, 0)),
            pl.BlockSpec((None, s, hp), lambda i, j: (i, 0, 0)),
            pl.BlockSpec((None, nblk, wv, tq), lambda i, j: (i, 0, 0, 0)),
        ],
        out_specs=pl.BlockSpec((None, tq, wv), lambda i, j: (i, j, 0)),
        out_shape=jax.ShapeDtypeStruct((b, s, wv), BF16),
        scratch_shapes=[pltpu.VMEM((A_HEADS, tq, tq), F32), pltpu.VMEM((A_HEADS, tq, tq), F32),
                        pltpu.VMEM((A_HEADS, 1, tq), F32), pltpu.VMEM((A_HEADS, V_DIM + BF16_ROWS, tq), F32)],
        name="attn_prompt",
        compiler_params=_params(("parallel", "parallel")),
    )(q, k, vt)


def _attn_sample_kernel(q_ref, clat_ref, ckr_ref, lat_ref, kr_ref, wkt_ref, wv_ref, o_ref, kn_ref, *, t):
    q = q_ref[...]
    qh = [q[:, h * HEAD_PAD:(h + 1) * HEAD_PAD] for h in range(A_HEADS)]
    qabs = jnp.concatenate(
        [_dot(qh[h], wkt_ref[h * HEAD_PAD:(h + 1) * HEAD_PAD, :]) for h in range(A_HEADS)], axis=0)
    qcat = jnp.concatenate([qabs.astype(BF16), jnp.concatenate(qh, axis=0)], axis=1)
    clat = clat_ref[...].astype(BF16)
    kv_lora = clat.shape[1]
    kc = jnp.concatenate([clat, ckr_ref[...]], axis=1)
    kn_ref[...] = jnp.zeros(kn_ref.shape, BF16)
    kn_ref[0:t, :] = jnp.concatenate([lat_ref[...], kr_ref[...]], axis=1).astype(BF16)
    kn = kn_ref[...]
    s1 = _dot_nt(qcat, kc)
    s2 = _dot_nt(qcat, kn)
    s2 = jnp.where(lax.broadcasted_iota(jnp.int32, s2.shape, 1) < t, s2, NEG_BIG)
    m = jnp.maximum(jnp.max(s1, axis=-1, keepdims=True), jnp.max(s2, axis=-1, keepdims=True))
    p1 = jnp.exp2(s1 - m)
    p2 = jnp.exp2(s2 - m)
    l = jnp.sum(p1, axis=-1, keepdims=True) + jnp.sum(p2, axis=-1, keepdims=True)
    olat = ((_dot(p1.astype(BF16), clat) + _dot(p2.astype(BF16), kn[:, 0:kv_lora])) / l).astype(BF16)
    low_half = lax.broadcasted_iota(jnp.int32, (t, LANES), 1) < V_DIM
    for pair in range(A_HEADS // 2):
        wpair = wv_ref[:, pair * LANES:(pair + 1) * LANES]
        lo = _dot(olat[(2 * pair) * t:(2 * pair + 1) * t, :], wpair)
        hi = _dot(olat[(2 * pair + 1) * t:(2 * pair + 2) * t, :], wpair)
        o_ref[:, pair * LANES:(pair + 1) * LANES] = jnp.where(low_half, lo, hi).astype(BF16)


def _attn_sample(q, cache_lat, cache_kr, lat, kr, wkt, wv, l):
    b, t, hp = q.shape
    past, kv_lora = cache_lat.shape[2], cache_lat.shape[3]
    wvw = wv.shape[2]
    per_b = lambda r, w: pl.BlockSpec((None, r, w), lambda i: (i, 0, 0))
    cache = lambda r, w: pl.BlockSpec((None, None, r, w), lambda i: (l, i, 0, 0))
    return pl.pallas_call(
        functools.partial(_attn_sample_kernel, t=t),
        grid=(b,),
        in_specs=[per_b(t, hp), cache(past, kv_lora), cache(past, LANES), per_b(t, kv_lora),
                  per_b(t, LANES), _layer(wkt, l), _layer(wv, l)],
        out_specs=per_b(t, wvw),
        out_shape=jax.ShapeDtypeStruct((b, t, wvw), BF16),
        scratch_shapes=[pltpu.VMEM((LANES, kv_lora + LANES), BF16)],
        name="attn_sample",
        compiler_params=_params(("parallel",)),
    )(q, cache_lat, cache_kr, lat, kr, wkt, wv)


MIX_OUT_WEIGHTS = ("mix_norm_pre", "mix_norm_post", "w_g", "w_br_a", "w_br_b", "w_br_c", "w_br_d", "w_o")


def _mix_out_kernel(x_ref, a_ref, y_ref, gpre_ref, gpost_ref, wg_ref, wba_ref, wbb_ref, wbc_ref, wbd_ref,
                    wo_ref, o_ref, *, d, c_b, c_c, c_d):
    x = x_ref[...]
    n = _rms(x, gpre_ref[...]).astype(BF16)
    branches = (
        (a_ref[...], wba_ref),
        (y_ref[:, 0:c_b], wbb_ref),
        (y_ref[:, c_b:c_b + c_c], wbc_ref),
        (y_ref[:, c_b + c_c:c_b + c_c + c_d], wbd_ref),
    )
    merged = jnp.zeros(x.shape, F32)
    for i, (br, w_ref) in enumerate(branches):
        gate = jax.nn.sigmoid(_dot(n, wg_ref[:, i * d:(i + 1) * d]))
        merged = merged + gate * _dot(br, w_ref[...])
    out = _dot(merged.astype(BF16), wo_ref[...])
    o_ref[...] = x + _rms(out, gpost_ref[...])


def _mix_out(x, a, y, w, l):
    n, d = x.shape
    c_b, c_c, c_d = w["w_br_b"].shape[1], w["w_br_c"].shape[1], w["w_br_d"].shape[1]
    tm = _pick_tile(n, ROW_TILES)
    weights = [w[k] for k in MIX_OUT_WEIGHTS]
    row = lambda wd: pl.BlockSpec((tm, wd), lambda i: (i, 0))
    return pl.pallas_call(
        functools.partial(_mix_out_kernel, d=d, c_b=c_b, c_c=c_c, c_d=c_d),
        grid=(n // tm,),
        in_specs=[row(d), row(a.shape[1]), row(y.shape[1])] + [_layer(a_, l) for a_ in weights],
        out_specs=row(d),
        out_shape=jax.ShapeDtypeStruct((n, d), F32),
        name="mix_out",
        compiler_params=_params(("parallel",)),
    )(x, a, y, *weights)


def _rope_tabs(pos):
    half = ROPE_DIM // 2
    inv = jnp.exp(-math.log(ROPE_THETA) * jnp.arange(half, dtype=F32) / half)
    ang = pos.astype(F32)[:, None] * inv[None, :]
    cos, sin = jnp.cos(ang), jnp.sin(ang)
    n = pos.shape[0]
    z = lambda w: jnp.zeros((n, w), F32)
    ctab = jnp.concatenate([cos, cos, jnp.ones((n, NOPE_DIM), F32), z(LANES - ROPE_DIM - NOPE_DIM)], axis=1)
    sneg = jnp.concatenate([-sin, z(LANES - half)], axis=1)
    spos = jnp.concatenate([z(half), sin, z(LANES - 2 * half)], axis=1)
    return ctab, sneg, spos


def _pad_rows(w, rows):
    return jnp.pad(w, ((0, 0), (0, rows - w.shape[1]), (0, 0)))


def kernel(x_prompt, x_sample, cache_kv_latent, cache_k_rope, state_conv_b, state_conv_d, ffn1_norm_pre, ffn1_norm_post, ffn1_w_gu, ffn1_w_down, mix_norm_pre, mix_norm_post, w_in, q_norm, w_uq, kv_norm, w_ukv, conv_b_w, conv_b_bias, conv_b_ln_g, conv_b_ln_b, gmlp_vn_g, gmlp_vn_b, gmlp_w_s, gmlp_b_s, conv_d_w, w_br_a, w_br_b, w_br_c, w_br_d, w_o, ffn2_norm_pre, ffn2_norm_post, ffn2_w_gu, ffn2_w_down):
    b, s, d = x_prompt.shape
    bs, t, _ = x_sample.shape
    depth = w_in.shape[0]
    past = cache_kv_latent.shape[2]
    q_lora, kv_lora = q_norm.shape[1], kv_norm.shape[1]
    c_b, c_c, c_d = conv_b_bias.shape[1], gmlp_vn_g.shape[1], conv_d_w.shape[2]
    kb_w, kd_w = conv_b_w.shape[1], conv_d_w.shape[1]
    assert kb_w - 1 <= CONV_B_PAD and kd_w - 1 <= CONV_D_PAD
    assert s % GMLP_CHUNK == 0 and t <= GMLP_CHUNK and t % 16 == 0 and past % CHUNK == 0 and t <= CHUNK
    assert gmlp_w_s.shape[1] == C_GROUPS and w_ukv.shape[2] == A_HEADS * (NOPE_DIM + V_DIM)

    o_q = q_lora
    o_kv = o_q + kv_lora
    o_kr = o_kv + ROPE_DIM
    o_b = o_kr + 2 * c_b
    o_c = o_b + 2 * c_c
    o_d = o_c + 3 * c_d

    vec = lambda p: p[:, None, :]
    uq = w_uq.reshape(depth, q_lora, A_HEADS, NOPE_DIM + ROPE_DIM)
    ukv = w_ukv.reshape(depth, kv_lora, A_HEADS, NOPE_DIM + V_DIM)
    head_zeros = lambda rows, width: jnp.zeros((depth, rows, A_HEADS, width), F32)
    wk_pad = jnp.concatenate(
        [head_zeros(kv_lora, ROPE_DIM), ukv[..., :NOPE_DIM], head_zeros(kv_lora, HEAD_PAD - NOPE_DIM - ROPE_DIM)],
        axis=3).reshape(depth, kv_lora, A_HEADS * HEAD_PAD).astype(BF16)
    w = dict(
        mix_norm_pre=vec(mix_norm_pre), mix_norm_post=vec(mix_norm_post),
        w_in1=jnp.concatenate(
            [w_in[:, :, 0:o_kv].astype(BF16), w_in[:, :, o_kr:o_d].astype(BF16), w_in[:, :, o_kv:o_kr].astype(BF16),
             jnp.zeros((depth, d, LANES - ROPE_DIM), BF16)], axis=2),
        w_g=w_in[:, :, o_d:].astype(BF16),
        q_norm=vec(q_norm),
        wq_pad=jnp.concatenate(
            [uq[..., NOPE_DIM:], uq[..., :NOPE_DIM], head_zeros(q_lora, HEAD_PAD - NOPE_DIM - ROPE_DIM)],
            axis=3).reshape(depth, q_lora, A_HEADS * HEAD_PAD).astype(BF16),
        kv_norm=vec(kv_norm), wk_pad=wk_pad,
        wv=ukv[..., NOPE_DIM:].reshape(depth, kv_lora, A_HEADS * V_DIM).astype(BF16),
        conv_b_w=_pad_rows(conv_b_w, CONV_B_PAD), conv_b_bias=vec(conv_b_bias),
        conv_b_ln_g=vec(conv_b_ln_g), conv_b_ln_b=vec(conv_b_ln_b),
        gmlp_vn_g=vec(gmlp_vn_g), gmlp_vn_b=vec(gmlp_vn_b), gmlp_w_s=gmlp_w_s,
        bs_full=jnp.repeat(jnp.swapaxes(gmlp_b_s, 1, 2), c_c // C_GROUPS, axis=2),
        conv_d_w=_pad_rows(conv_d_w, CONV_D_PAD), kb_w=kb_w, kd_w=kd_w,
        w_br_a=w_br_a.astype(BF16), w_br_b=w_br_b.astype(BF16), w_br_c=w_br_c.astype(BF16),
        w_br_d=w_br_d.astype(BF16), w_o=w_o.astype(BF16))
    wk_t = jnp.swapaxes(wk_pad, 1, 2)
    f1 = (vec(ffn1_norm_pre), vec(ffn1_norm_post), ffn1_w_gu.astype(BF16), ffn1_w_down.astype(BF16))
    f2 = (vec(ffn2_norm_pre), vec(ffn2_norm_post), ffn2_w_gu.astype(BF16), ffn2_w_down.astype(BF16))

    tabs_p = _rope_tabs(jnp.arange(s))
    tabs_s = _rope_tabs(past + jnp.arange(t))
    zero_b = jnp.zeros((b, CONV_B_PAD, c_b), F32)
    zero_d = jnp.zeros((b, CONV_D_PAD, c_d), F32)
    tail_b = CONV_B_PAD - (kb_w - 1)
    tail_d = CONV_D_PAD - (kd_w - 1)
    st_b = jnp.pad(state_conv_b, ((0, 0), (0, 0), (tail_b, 0), (0, 0)))
    st_d = jnp.pad(state_conv_d, ((0, 0), (0, 0), (tail_d, 0), (0, 0)))
    ckr_pad = jnp.pad(cache_k_rope, ((0, 0), (0, 0), (0, 0), (0, LANES - ROPE_DIM))).astype(BF16)
    tq = _pick_tile(s, (256, 128))

    xp = x_prompt.reshape(b * s, d)
    xs = x_sample.reshape(bs * t, d)
    stacked = None
    outs = [[] for _ in range(7)]
    for l in range(depth):
        xp = _ffn(xp, *f1, l)
        xs = _ffn(xs, *f1, l)

        qp, kp, vtp, yp, lat_all, kr_all, btp, dtp = _mix_in(
            xp.reshape(b, s, d), tabs_p, zero_b, zero_d, None, w, l, tq=tq, prompt=True, stacked=stacked)
        stacked = (lat_all, kr_all)
        ap = _attn_prompt(qp, kp, vtp, tq=tq)
        xp = _mix_out(xp, ap.reshape(b * s, -1), yp.reshape(b * s, -1), w, l)

        qs, ys, lats, krs, bts, dts, vgs = _mix_in(
            xs.reshape(bs, t, d), tabs_s, st_b, st_d, l, w, l, tq=t, prompt=False)
        a_s = _attn_sample(qs, cache_kv_latent, ckr_pad, lats, krs, wk_t, w["wv"], l)
        xs = _mix_out(xs, a_s.reshape(bs * t, -1), ys.reshape(bs * t, -1), w, l)

        xp = _ffn(xp, *f2, l)
        xs = _ffn(xs, *f2, l)

        for lst, val in zip(outs, (btp[:, tail_b:], dtp[:, tail_d:], lats, krs[..., :ROPE_DIM],
                                   bts[:, tail_b:], vgs, dts[:, tail_d:])):
            lst.append(val)
    cb_p, cd_p, lat_s, kr_s, cb_s, vc_s, cd_s = (jnp.stack(o) for o in outs)
    return (xp.reshape(b, s, d), xs.reshape(bs, t, d), stacked[0], stacked[1], cb_p, cd_p,
            lat_s, kr_s, cb_s, vc_s, cd_s)
```

```python
import functools
import math

import jax
import jax.numpy as jnp
from jax import lax
from jax.experimental import pallas as pl
from jax.experimental.pallas import tpu as pltpu

F32 = jnp.float32
BF16 = jnp.bfloat16

CHUNK = 64
A_HEADS = 8
NOPE_DIM = 64
ROPE_DIM = 32
V_DIM = 64
ROPE_THETA = 10000.0
SM_SCALE = (NOPE_DIM + ROPE_DIM) ** -0.5
Q_SCALE = SM_SCALE * math.log2(math.e)
C_GROUPS = 4
GMLP_CHUNK = 128
N_BRANCH = 4
EPS = 1e-6

LANES = 128
SUBLANES = 8
BF16_ROWS = 16
HEAD_PAD = 128
CONV_B_PAD = 32
CONV_D_PAD = 8
VMEM_LIMIT_BYTES = 56 * 1024 * 1024
NEG_BIG = -1e30
ROW_TILES = (1024, 512, 256, 128, 64, 32, 16, 8)


def _pick_tile(n, candidates):
    for c in candidates:
        if n % c == 0:
            return c
    return n


def _layer(arr, l):
    nd = arr.ndim - 1
    return pl.BlockSpec((None,) + arr.shape[1:], lambda *_: (l,) + (0,) * nd, pipeline_mode=pl.Buffered(1))


def _whole(arr):
    nd = arr.ndim
    return pl.BlockSpec(arr.shape, lambda *_: (0,) * nd, pipeline_mode=pl.Buffered(1))


def _rms(x, g):
    ms = jnp.mean(x * x, axis=-1, keepdims=True)
    return x * lax.rsqrt(ms + EPS) * g


def _layernorm(x, g, b):
    mu = jnp.mean(x, axis=-1, keepdims=True)
    xc = x - mu
    var = jnp.mean(xc * xc, axis=-1, keepdims=True)
    return xc * lax.rsqrt(var + EPS) * g + b


def _dot(a, b):
    return jnp.dot(a, b, preferred_element_type=F32)


def _dot_nt(a, b):
    return lax.dot_general(a, b, (((1,), (1,)), ((), ())), preferred_element_type=F32)


def _params(sem):
    return pltpu.CompilerParams(dimension_semantics=sem, vmem_limit_bytes=VMEM_LIMIT_BYTES)


def _ffn_kernel(x_ref, gpre_ref, gpost_ref, wgu_ref, wd_ref, o_ref, act_ref, *, d_ff, tf):
    x = x_ref[...]
    n = _rms(x, gpre_ref[...]).astype(BF16)
    for c in range(d_ff // tf):
        gate = _dot(n, wgu_ref[:, c * tf:(c + 1) * tf])
        up = _dot(n, wgu_ref[:, d_ff + c * tf:d_ff + (c + 1) * tf])
        act_ref[:, c * tf:(c + 1) * tf] = (gate * jax.nn.sigmoid(gate) * up).astype(BF16)
    y = _dot(act_ref[...], wd_ref[...])
    o_ref[...] = x + 0.5 * _rms(y, gpost_ref[...])


def _ffn(x, gpre, gpost, wgu, wd, l):
    n, d = x.shape
    d_ff = wd.shape[1]
    tm = _pick_tile(n, ROW_TILES)
    tf = _pick_tile(d_ff, (256, 128))
    weights = (gpre, gpost, wgu, wd)
    return pl.pallas_call(
        functools.partial(_ffn_kernel, d_ff=d_ff, tf=tf),
        grid=(n // tm,),
        in_specs=[pl.BlockSpec((tm, d), lambda i: (i, 0))] + [_layer(w, l) for w in weights],
        out_specs=pl.BlockSpec((tm, d), lambda i: (i, 0)),
        out_shape=jax.ShapeDtypeStruct((n, d), F32),
        scratch_shapes=[pltpu.VMEM((tm, d_ff), BF16)],
        name="ffn",
        compiler_params=_params(("parallel",)),
    )(x, *weights)


def _rope128(a, c, sneg, spos):
    return (a * c + pltpu.roll(a, LANES - ROPE_DIM // 2, 1) * sneg
            + pltpu.roll(a, ROPE_DIM // 2, 1) * spos)


MIX_IN_WEIGHTS = ("mix_norm_pre", "w_in1", "q_norm", "wq_pad", "kv_norm", "wk_pad", "wv", "conv_b_w",
                  "conv_b_bias", "conv_b_ln_g", "conv_b_ln_b", "gmlp_vn_g", "gmlp_vn_b", "gmlp_w_s", "bs_full",
                  "conv_d_w")
N_MIX_IN_INPUTS = 6 + len(MIX_IN_WEIGHTS)


def _mix_in_kernel(*refs, prompt, n_alias, n_tiles, nt, tq, q_lora, kv_lora, c_b, c_c, c_d, kb_w, kd_w, rb):
    (x_ref, c_ref, sneg_ref, spos_ref, stb_ref, std_ref,
     gpre_ref, win_ref, qn_ref, wq_ref, kvn_ref, wk_ref, wv_ref,
     cbw_ref, cbb_ref, lng_ref, lnb_ref, vng_ref, vnb_ref, ws_ref, bs_ref, cdw_ref) = refs[:N_MIX_IN_INPUTS]
    rest = refs[N_MIX_IN_INPUTS + n_alias:]
    if prompt:
        q_ref, k_ref, vt_ref, y_ref, lat_ref, kr_ref, bt_ref, dt_ref, ha_ref, hb_ref, xpb_ref, xsh_ref, xpd_ref = rest
    else:
        q_ref, y_ref, lat_ref, kr_ref, bt_ref, dt_ref, vg_ref, ha_ref, hb_ref, xpb_ref, xsh_ref, xpd_ref = rest
    i = pl.program_id(0)

    o_kv = q_lora
    o_b = o_kv + kv_lora
    o_c = o_b + 2 * c_b
    o_d = o_c + 2 * c_c
    o_kr = o_d + 3 * c_d
    col_groups = (0, o_kv, o_b, o_c, o_d, o_kr, o_kr + LANES)

    def project(h_ref):
        n = _rms(x_ref[...], gpre_ref[...]).astype(BF16)
        for c0, c1 in zip(col_groups[:-1], col_groups[1:]):
            h_ref[:, c0:c1] = _dot(n, win_ref[:, c0:c1])
            yield

    def consume(h_ref):
        first = (i - 1) % nt == 0
        ctab, sneg, spos = c_ref[...], sneg_ref[...], spos_ref[...]

        qlat = _rms(h_ref[:, 0:o_kv], qn_ref[...]).astype(BF16)
        qa = _dot(qlat, wq_ref[...])
        for h in range(A_HEADS):
            qh = _rope128(qa[:, h * HEAD_PAD:(h + 1) * HEAD_PAD], ctab, sneg, spos) * Q_SCALE
            q_ref[:, h * HEAD_PAD:(h + 1) * HEAD_PAD] = qh.astype(BF16)

        yield
        lat = _rms(h_ref[:, o_kv:o_b], kvn_ref[...])
        lat_ref[...] = lat
        latb = lat.astype(BF16)
        kr = _rope128(h_ref[:, o_kr:o_kr + LANES], ctab, sneg, spos)
        if prompt:
            kr_ref[...] = kr[:, 0:ROPE_DIM]
            kn = _dot(latb, wk_ref[...])
            for h in range(A_HEADS):
                k_ref[:, h * HEAD_PAD:(h + 1) * HEAD_PAD] = (
                    kn[:, h * HEAD_PAD:(h + 1) * HEAD_PAD] + kr).astype(BF16)
            vt_ref[...] = _dot(latb, wv_ref[...]).T.astype(BF16)
        else:
            kr_ref[...] = kr

        yield
        xb = h_ref[:, o_b:o_b + c_b] * jax.nn.sigmoid(h_ref[:, o_b + c_b:o_c])

        @pl.when(first)
        def _():
            xpb_ref[0:CONV_B_PAD, :] = stb_ref[...]
            xpd_ref[0:CONV_D_PAD, :] = std_ref[...]

        @pl.when(jnp.logical_not(first))
        def _():
            xpb_ref[0:CONV_B_PAD, :] = xpb_ref[tq:tq + CONV_B_PAD, :]
            xpd_ref[0:CONV_D_PAD, :] = xpd_ref[tq:tq + CONV_D_PAD, :]

        xpb_ref[CONV_B_PAD:CONV_B_PAD + tq, :] = xb
        bt_ref[...] = xpb_ref[tq:tq + CONV_B_PAD, :]
        n_sh = tq + CONV_B_PAD - SUBLANES
        for r in range(1, SUBLANES):
            xsh_ref[r, 0:n_sh, :] = xpb_ref[r:r + n_sh, :]

        off_b = CONV_B_PAD - (kb_w - 1)
        for blk in range(tq // rb):
            yield
            acc = jnp.broadcast_to(cbb_ref[...], (rb, c_b))
            for kk in range(kb_w):
                a8, r = divmod(off_b + kk, SUBLANES)
                row0 = blk * rb + a8 * SUBLANES
                src = xpb_ref[row0:row0 + rb, :] if r == 0 else xsh_ref[r, row0:row0 + rb, :]
                acc = acc + src * cbw_ref[kk:kk + 1, :]
            yb = _layernorm(acc, lng_ref[...], lnb_ref[...])
            y_ref[blk * rb:(blk + 1) * rb, 0:c_b] = (yb * jax.nn.sigmoid(yb)).astype(BF16)

        yield
        vg = _layernorm(h_ref[:, o_c + c_c:o_d], vng_ref[...], vnb_ref[...])
        if not prompt:
            vg_ref[...] = vg
        vgb = vg.astype(BF16)
        ck = min(tq, GMLP_CHUNK)
        ri = lax.broadcasted_iota(jnp.int32, (ck, ck), 0) // CHUNK
        ci = lax.broadcasted_iota(jnp.int32, (ck, ck), 1) // CHUNK
        lane_grp = lax.broadcasted_iota(jnp.int32, (ck, c_c), 1) // (c_c // C_GROUPS)
        wmix = [jnp.where(ci <= ri, ws_ref[g, 0:ck, 0:ck], 0.0).astype(BF16) for g in range(C_GROUPS)]
        for c in range(tq // ck):
            vc = vgb[c * ck:(c + 1) * ck, :]
            mix = bs_ref[0:ck, :]
            for g in range(C_GROUPS):
                mix = mix + jnp.where(lane_grp == g, _dot(wmix[g], vc), 0.0)
            y_ref[c * ck:(c + 1) * ck, c_b:c_b + c_c] = (
                h_ref[c * ck:(c + 1) * ck, o_c:o_c + c_c] * mix).astype(BF16)

        yield
        xd = h_ref[:, o_d + c_d:o_d + 2 * c_d] * h_ref[:, o_d + 2 * c_d:o_kr]
        xpd_ref[CONV_D_PAD:CONV_D_PAD + tq, :] = xd
        dt_ref[...] = xpd_ref[tq:tq + CONV_D_PAD, :]
        off_d = CONV_D_PAD - (kd_w - 1)
        conv = xd * cdw_ref[kd_w - 1:kd_w, :]
        for kk in range(kd_w - 1):
            conv = conv + xpd_ref[off_d + kk:off_d + kk + tq, :] * cdw_ref[kk:kk + 1, :]
        y_ref[:, c_b + c_c:c_b + c_c + c_d] = (h_ref[:, o_d:o_d + c_d] * conv).astype(BF16)

    bufs = (ha_ref, hb_ref)

    def run(*stages):
        live = list(stages)
        while live:
            for g in list(live):
                if next(g, StopIteration) is StopIteration:
                    live.remove(g)

    @pl.when(i == 0)
    def _():
        run(project(bufs[0]))

    for par in (0, 1):
        @pl.when((i >= 1) & (i < n_tiles) & (i % 2 == par))
        def _(par=par):
            run(project(bufs[par]), consume(bufs[1 - par]))

    @pl.when(i == n_tiles)
    def _():
        run(consume(bufs[(n_tiles - 1) % 2]))


def _mix_in(x, tabs, stb, std, st_layer, w, l, *, b, s, tq, prompt, stacked=None):
    d = x.shape[1]
    depth = w["w_in1"].shape[0]
    q_lora, kv_lora = w["q_norm"].shape[2], w["kv_norm"].shape[2]
    c_b, c_c, c_d = w["conv_b_bias"].shape[2], w["gmlp_vn_g"].shape[2], w["conv_d_w"].shape[2]
    hp = A_HEADS * HEAD_PAD
    rb = min(tq, 64)
    nt = s // tq
    n_tiles = b * nt
    done = lambda i: jnp.maximum(i - 1, 0)
    row_in = pl.BlockSpec((tq, d), lambda i: (jnp.minimum(i, n_tiles - 1), 0))
    row = lambda wd: pl.BlockSpec((tq, wd), lambda i: (done(i), 0))
    per_b = lambda r, wd: pl.BlockSpec((None, r, wd), lambda i: (done(i) // nt, 0, 0))
    if st_layer is None:
        st_spec = per_b
    else:
        st_spec = lambda r, wd: pl.BlockSpec((None, None, r, wd), lambda i: (st_layer, done(i) // nt, 0, 0))
    tab = pl.BlockSpec((tq, LANES), lambda i: (done(i) % nt, 0))
    weights = [w[k] for k in MIX_IN_WEIGHTS]
    sds = jax.ShapeDtypeStruct
    small = [
        (sds((b, CONV_B_PAD, c_b), F32), per_b(CONV_B_PAD, c_b)),
        (sds((b, CONV_D_PAD, c_d), F32), per_b(CONV_D_PAD, c_d)),
    ]
    outs = [(sds((b * s, hp), BF16), row(hp))]
    aliases = {}
    alias_in, alias_specs = [], []
    if prompt:
        vt_spec = pl.BlockSpec((None, None, A_HEADS * V_DIM, tq), lambda i: (done(i) // nt, done(i) % nt, 0, 0))
        stk = lambda wd: pl.BlockSpec((None, None, tq, wd), lambda i: (l, done(i) // nt, done(i) % nt, 0))
        outs += [(sds((b * s, hp), BF16), row(hp)),
                 (sds((b, nt, A_HEADS * V_DIM, tq), BF16), vt_spec),
                 (sds((b * s, c_b + c_c + c_d), BF16), row(c_b + c_c + c_d)),
                 (sds((depth, b, s, kv_lora), F32), stk(kv_lora)),
                 (sds((depth, b, s, ROPE_DIM), F32), stk(ROPE_DIM))]
        outs += small
        if stacked is not None:
            alias_in = list(stacked)
            alias_specs = [pl.BlockSpec(memory_space=pl.ANY)] * 2
            aliases = {N_MIX_IN_INPUTS: 4, N_MIX_IN_INPUTS + 1: 5}
    else:
        outs += [(sds((b * s, c_b + c_c + c_d), BF16), row(c_b + c_c + c_d)),
                 (sds((b * s, kv_lora), F32), row(kv_lora)),
                 (sds((b * s, LANES), F32), row(LANES))]
        outs += small + [(sds((b * s, c_c), F32), row(c_c))]
    out_shape, out_specs = zip(*outs)
    w1 = w["w_in1"].shape[2]
    return pl.pallas_call(
        functools.partial(_mix_in_kernel, prompt=prompt, n_alias=len(alias_in), n_tiles=n_tiles, nt=nt, tq=tq,
                          q_lora=q_lora, kv_lora=kv_lora, c_b=c_b, c_c=c_c, c_d=c_d, kb_w=w["kb_w"],
                          kd_w=w["kd_w"], rb=rb),
        grid=(n_tiles + 1,),
        in_specs=[row_in, tab, tab, tab, st_spec(CONV_B_PAD, c_b), st_spec(CONV_D_PAD, c_d)]
        + [_layer(a, l) for a in weights] + alias_specs,
        out_specs=out_specs,
        out_shape=out_shape,
        input_output_aliases=aliases,
        scratch_shapes=[pltpu.VMEM((tq, w1), F32), pltpu.VMEM((tq, w1), F32),
                        pltpu.VMEM((CONV_B_PAD + tq, c_b), F32),
                        pltpu.VMEM((SUBLANES, CONV_B_PAD + tq, c_b), F32),
                        pltpu.VMEM((CONV_D_PAD + tq, c_d), F32)],
        name="mix_in",
        compiler_params=_params(("arbitrary",)),
    )(x, *tabs, stb, std, *weights, *alias_in)


def _attn_kernel(q_ref, k_ref, vt_ref, o_ref, st0_ref, st1_ref, m_ref, acc_ref, *, tq):
    qi = pl.program_id(1)
    kc = lax.broadcasted_iota(jnp.int32, (tq, tq), 0) // CHUNK
    qc = lax.broadcasted_iota(jnp.int32, (tq, tq), 1) // CHUNK
    diag_ok = kc <= qc
    heads = tuple(range(A_HEADS))
    ones = jnp.ones((BF16_ROWS, tq), BF16)

    def scores(j, st_ref):
        k0 = pl.multiple_of(j * tq, tq)
        for h in heads:
            st_ref[h] = _dot_nt(k_ref[pl.ds(k0, tq), h * HEAD_PAD:(h + 1) * HEAD_PAD],
                                q_ref[:, h * HEAD_PAD:(h + 1) * HEAD_PAD])

    def update_head(j, st_ref, h, masked):
        st = st_ref[h]
        if masked:
            st = jnp.where(diag_ok, st, NEG_BIG)
        m = m_ref[h]
        m_new = jnp.maximum(m, jnp.max(st, axis=0, keepdims=True))
        alpha = jnp.exp2(m - m_new)
        pb = jnp.exp2((st - m_new).astype(BF16))
        vt1 = jnp.concatenate([vt_ref[j, h * V_DIM:(h + 1) * V_DIM, :], ones], axis=0)
        m_ref[h] = m_new
        acc_ref[h] = alpha * acc_ref[h] + _dot(vt1, pb)

    def update(j, st_ref, masked):
        for h in heads:
            update_head(j, st_ref, h, masked)

    m_ref[...] = jnp.full(m_ref.shape, NEG_BIG, F32)
    acc_ref[...] = jnp.zeros(acc_ref.shape, F32)
    scores(0, st0_ref)

    def both(j_next, next_ref, j, cur_ref):
        k0 = pl.multiple_of(j_next * tq, tq)
        for h in heads:
            next_ref[h] = _dot_nt(k_ref[pl.ds(k0, tq), h * HEAD_PAD:(h + 1) * HEAD_PAD],
                                  q_ref[:, h * HEAD_PAD:(h + 1) * HEAD_PAD])
            update_head(j, cur_ref, h, False)

    def pair(p, _):
        j = 2 * p
        both(j + 1, st1_ref, j, st0_ref)
        both(j + 2, st0_ref, j + 1, st1_ref)
        return 0

    lax.fori_loop(0, qi // 2, pair, 0)

    @pl.when(qi % 2 == 0)
    def _():
        update(qi, st0_ref, True)

    @pl.when(qi % 2 == 1)
    def _():
        both(qi, st1_ref, qi - 1, st0_ref)
        update(qi, st1_ref, True)

    ot = jnp.concatenate([acc_ref[h, 0:V_DIM, :] / acc_ref[h, V_DIM:V_DIM + 1, :] for h in heads], axis=0)
    o_ref[...] = ot.T.astype(BF16)


def _attn_prompt(q, k, vt, *, tq):
    b, s, hp = q.shape
    nblk, wv = vt.shape[1], vt.shape[2]
    return pl.pallas_call(
        functools.partial(_attn_kernel, tq=tq),
        grid=(b, s // tq),
        in_specs=[
            pl.BlockSpec((None, tq, hp), lambda i, j: (i, j, 0)),
            pl.BlockSpec((None, s, hp), lambda i, j: (i, 0, 0)),
            pl.BlockSpec((None, nblk, wv, tq), lambda i, j: (i, 0, 0, 0)),
        ],
        out_specs=pl.BlockSpec((None, tq, wv), lambda i, j: (i, j, 0)),
        out_shape=jax.ShapeDtypeStruct((b, s, wv), BF16),
        scratch_shapes=[pltpu.VMEM((A_HEADS, tq, tq), F32), pltpu.VMEM((A_HEADS, tq, tq), F32),
                        pltpu.VMEM((A_HEADS, 1, tq), F32), pltpu.VMEM((A_HEADS, V_DIM + BF16_ROWS, tq), F32)],
        name="attn_prompt",
        compiler_params=_params(("parallel", "parallel")),
    )(q, k, vt)


def _attn_sample_kernel(q_ref, clat_ref, ckr_ref, lat_ref, kr_ref, wkt_ref, wv_ref, o_ref, kn_ref, *, t):
    q = q_ref[...]
    qh = [q[:, h * HEAD_PAD:(h + 1) * HEAD_PAD] for h in range(A_HEADS)]
    qabs = jnp.concatenate(
        [_dot(qh[h], wkt_ref[h * HEAD_PAD:(h + 1) * HEAD_PAD, :]) for h in range(A_HEADS)], axis=0)
    qcat = jnp.concatenate([qabs.astype(BF16), jnp.concatenate(qh, axis=0)], axis=1)
    clat = clat_ref[...].astype(BF16)
    kv_lora = clat.shape[1]
    kc = jnp.concatenate([clat, ckr_ref[...]], axis=1)
    kn_ref[...] = jnp.zeros(kn_ref.shape, BF16)
    kn_ref[0:t, :] = jnp.concatenate([lat_ref[...], kr_ref[...]], axis=1).astype(BF16)
    kn = kn_ref[...]
    s1 = _dot_nt(qcat, kc)
    s2 = _dot_nt(qcat, kn)
    s2 = jnp.where(lax.broadcasted_iota(jnp.int32, s2.shape, 1) < t, s2, NEG_BIG)
    m = jnp.maximum(jnp.max(s1, axis=-1, keepdims=True), jnp.max(s2, axis=-1, keepdims=True))
    p1 = jnp.exp2(s1 - m)
    p2 = jnp.exp2(s2 - m)
    l = jnp.sum(p1, axis=-1, keepdims=True) + jnp.sum(p2, axis=-1, keepdims=True)
    olat = ((_dot(p1.astype(BF16), clat) + _dot(p2.astype(BF16), kn[:, 0:kv_lora])) / l).astype(BF16)
    low_half = lax.broadcasted_iota(jnp.int32, (t, LANES), 1) < V_DIM
    for pair in range(A_HEADS // 2):
        wpair = wv_ref[:, pair * LANES:(pair + 1) * LANES]
        lo = _dot(olat[(2 * pair) * t:(2 * pair + 1) * t, :], wpair)
        hi = _dot(olat[(2 * pair + 1) * t:(2 * pair + 2) * t, :], wpair)
        o_ref[:, pair * LANES:(pair + 1) * LANES] = jnp.where(low_half, lo, hi).astype(BF16)


def _attn_sample(q, cache_lat, cache_kr, lat, kr, wkt, wv, l):
    b, t, hp = q.shape
    past, kv_lora = cache_lat.shape[2], cache_lat.shape[3]
    wvw = wv.shape[2]
    per_b = lambda r, w: pl.BlockSpec((None, r, w), lambda i: (i, 0, 0))
    cache = lambda r, w: pl.BlockSpec((None, None, r, w), lambda i: (l, i, 0, 0))
    return pl.pallas_call(
        functools.partial(_attn_sample_kernel, t=t),
        grid=(b,),
        in_specs=[per_b(t, hp), cache(past, kv_lora), cache(past, LANES), per_b(t, kv_lora),
                  per_b(t, LANES), _layer(wkt, l), _layer(wv, l)],
        out_specs=per_b(t, wvw),
        out_shape=jax.ShapeDtypeStruct((b, t, wvw), BF16),
        scratch_shapes=[pltpu.VMEM((LANES, kv_lora + LANES), BF16)],
        name="attn_sample",
        compiler_params=_params(("parallel",)),
    )(q, cache_lat, cache_kr, lat, kr, wkt, wv)


MIX_OUT_WEIGHTS = ("mix_norm_pre", "mix_norm_post", "w_g", "w_br_a", "w_br_b", "w_br_c", "w_br_d", "w_o")


def _mix_out_kernel(x_ref, a_ref, y_ref, gpre_ref, gpost_ref, wg_ref, wba_ref, wbb_ref, wbc_ref, wbd_ref,
                    wo_ref, o_ref, *, d, c_b, c_c, c_d):
    x = x_ref[...]
    n = _rms(x, gpre_ref[...]).astype(BF16)
    branches = (
        (a_ref[...], wba_ref),
        (y_ref[:, 0:c_b], wbb_ref),
        (y_ref[:, c_b:c_b + c_c], wbc_ref),
        (y_ref[:, c_b + c_c:c_b + c_c + c_d], wbd_ref),
    )
    merged = jnp.zeros(x.shape, F32)
    for i, (br, w_ref) in enumerate(branches):
        gate = jax.nn.sigmoid(_dot(n, wg_ref[:, i * d:(i + 1) * d]))
        merged = merged + gate * _dot(br, w_ref[...])
    out = _dot(merged.astype(BF16), wo_ref[...])
    o_ref[...] = x + _rms(out, gpost_ref[...])


def _mix_out(x, a, y, w, l):
    n, d = x.shape
    c_b, c_c, c_d = w["w_br_b"].shape[1], w["w_br_c"].shape[1], w["w_br_d"].shape[1]
    tm = _pick_tile(n, ROW_TILES)
    weights = [w[k] for k in MIX_OUT_WEIGHTS]
    row = lambda wd: pl.BlockSpec((tm, wd), lambda i: (i, 0))
    return pl.pallas_call(
        functools.partial(_mix_out_kernel, d=d, c_b=c_b, c_c=c_c, c_d=c_d),
        grid=(n // tm,),
        in_specs=[row(d), row(a.shape[1]), row(y.shape[1])] + [_layer(a_, l) for a_ in weights],
        out_specs=row(d),
        out_shape=jax.ShapeDtypeStruct((n, d), F32),
        name="mix_out",
        compiler_params=_params(("parallel",)),
    )(x, a, y, *weights)


def _rope_tabs(pos):
    half = ROPE_DIM // 2
    inv = jnp.exp(-math.log(ROPE_THETA) * jnp.arange(half, dtype=F32) / half)
    ang = pos.astype(F32)[:, None] * inv[None, :]
    cos, sin = jnp.cos(ang), jnp.sin(ang)
    n = pos.shape[0]
    z = lambda w: jnp.zeros((n, w), F32)
    ctab = jnp.concatenate([cos, cos, jnp.ones((n, NOPE_DIM), F32), z(LANES - ROPE_DIM - NOPE_DIM)], axis=1)
    sneg = jnp.concatenate([-sin, z(LANES - half)], axis=1)
    spos = jnp.concatenate([z(half), sin, z(LANES - 2 * half)], axis=1)
    return ctab, sneg, spos


def _pad_rows(w, rows):
    return jnp.pad(w, ((0, 0), (0, rows - w.shape[1]), (0, 0)))


def kernel(x_prompt, x_sample, cache_kv_latent, cache_k_rope, state_conv_b, state_conv_d, ffn1_norm_pre, ffn1_norm_post, ffn1_w_gu, ffn1_w_down, mix_norm_pre, mix_norm_post, w_in, q_norm, w_uq, kv_norm, w_ukv, conv_b_w, conv_b_bias, conv_b_ln_g, conv_b_ln_b, gmlp_vn_g, gmlp_vn_b, gmlp_w_s, gmlp_b_s, conv_d_w, w_br_a, w_br_b, w_br_c, w_br_d, w_o, ffn2_norm_pre, ffn2_norm_post, ffn2_w_gu, ffn2_w_down):
    b, s, d = x_prompt.shape
    bs, t, _ = x_sample.shape
    depth = w_in.shape[0]
    past = cache_kv_latent.shape[2]
    q_lora, kv_lora = q_norm.shape[1], kv_norm.shape[1]
    c_b, c_c, c_d = conv_b_bias.shape[1], gmlp_vn_g.shape[1], conv_d_w.shape[2]
    kb_w, kd_w = conv_b_w.shape[1], conv_d_w.shape[1]
    assert kb_w - 1 <= CONV_B_PAD and kd_w - 1 <= CONV_D_PAD
    assert s % GMLP_CHUNK == 0 and t <= GMLP_CHUNK and t % 16 == 0 and past % CHUNK == 0 and t <= CHUNK
    assert gmlp_w_s.shape[1] == C_GROUPS and w_ukv.shape[2] == A_HEADS * (NOPE_DIM + V_DIM)

    o_q = q_lora
    o_kv = o_q + kv_lora
    o_kr = o_kv + ROPE_DIM
    o_b = o_kr + 2 * c_b
    o_c = o_b + 2 * c_c
    o_d = o_c + 3 * c_d

    vec = lambda p: p[:, None, :]
    uq = w_uq.reshape(depth, q_lora, A_HEADS, NOPE_DIM + ROPE_DIM)
    ukv = w_ukv.reshape(depth, kv_lora, A_HEADS, NOPE_DIM + V_DIM)
    head_zeros = lambda rows, width: jnp.zeros((depth, rows, A_HEADS, width), F32)
    wk_pad = jnp.concatenate(
        [head_zeros(kv_lora, ROPE_DIM), ukv[..., :NOPE_DIM], head_zeros(kv_lora, HEAD_PAD - NOPE_DIM - ROPE_DIM)],
        axis=3).reshape(depth, kv_lora, A_HEADS * HEAD_PAD).astype(BF16)
    w = dict(
        mix_norm_pre=vec(mix_norm_pre), mix_norm_post=vec(mix_norm_post),
        w_in1=jnp.concatenate(
            [w_in[:, :, 0:o_kv].astype(BF16), w_in[:, :, o_kr:o_d].astype(BF16), w_in[:, :, o_kv:o_kr].astype(BF16),
             jnp.zeros((depth, d, LANES - ROPE_DIM), BF16)], axis=2),
        w_g=w_in[:, :, o_d:].astype(BF16),
        q_norm=vec(q_norm),
        wq_pad=jnp.concatenate(
            [uq[..., NOPE_DIM:], uq[..., :NOPE_DIM], head_zeros(q_lora, HEAD_PAD - NOPE_DIM - ROPE_DIM)],
            axis=3).reshape(depth, q_lora, A_HEADS * HEAD_PAD).astype(BF16),
        kv_norm=vec(kv_norm), wk_pad=wk_pad,
        wv=ukv[..., NOPE_DIM:].reshape(depth, kv_lora, A_HEADS * V_DIM).astype(BF16),
        conv_b_w=_pad_rows(conv_b_w, CONV_B_PAD), conv_b_bias=vec(conv_b_bias),
        conv_b_ln_g=vec(conv_b_ln_g), conv_b_ln_b=vec(conv_b_ln_b),
        gmlp_vn_g=vec(gmlp_vn_g), gmlp_vn_b=vec(gmlp_vn_b), gmlp_w_s=gmlp_w_s,
        bs_full=jnp.repeat(jnp.swapaxes(gmlp_b_s, 1, 2), c_c // C_GROUPS, axis=2),
        conv_d_w=_pad_rows(conv_d_w, CONV_D_PAD), kb_w=kb_w, kd_w=kd_w,
        w_br_a=w_br_a.astype(BF16), w_br_b=w_br_b.astype(BF16), w_br_c=w_br_c.astype(BF16),
        w_br_d=w_br_d.astype(BF16), w_o=w_o.astype(BF16))
    wk_t = jnp.swapaxes(wk_pad, 1, 2)
    f1 = (vec(ffn1_norm_pre), vec(ffn1_norm_post), ffn1_w_gu.astype(BF16), ffn1_w_down.astype(BF16))
    f2 = (vec(ffn2_norm_pre), vec(ffn2_norm_post), ffn2_w_gu.astype(BF16), ffn2_w_down.astype(BF16))

    tabs_p = _rope_tabs(jnp.arange(s))
    tabs_s = _rope_tabs(past + jnp.arange(t))
    zero_b = jnp.zeros((b, CONV_B_PAD, c_b), F32)
    zero_d = jnp.zeros((b, CONV_D_PAD, c_d), F32)
    tail_b = CONV_B_PAD - (kb_w - 1)
    tail_d = CONV_D_PAD - (kd_w - 1)
    st_b = jnp.pad(state_conv_b, ((0, 0), (0, 0), (tail_b, 0), (0, 0)))
    st_d = jnp.pad(state_conv_d, ((0, 0), (0, 0), (tail_d, 0), (0, 0)))
    ckr_pad = jnp.pad(cache_k_rope, ((0, 0), (0, 0), (0, 0), (0, LANES - ROPE_DIM))).astype(BF16)
    tq = _pick_tile(s, (256, 128))

    xp = x_prompt.reshape(b * s, d)
    xs = x_sample.reshape(bs * t, d)
    stacked = None
    outs = [[] for _ in range(7)]
    for l in range(depth):
        xp = _ffn(xp, *f1, l)
        xs = _ffn(xs, *f1, l)

        qp, kp, vtp, yp, lat_all, kr_all, btp, dtp = _mix_in(
            xp, tabs_p, zero_b, zero_d, None, w, l, b=b, s=s, tq=tq, prompt=True, stacked=stacked)
        stacked = (lat_all, kr_all)
        ap = _attn_prompt(qp.reshape(b, s, -1), kp.reshape(b, s, -1), vtp, tq=tq)
        xp = _mix_out(xp, ap.reshape(b * s, -1), yp, w, l)

        qs, ys, lats, krs, bts, dts, vgs = _mix_in(
            xs, tabs_s, st_b, st_d, l, w, l, b=bs, s=t, tq=t, prompt=False)
        lats, krs, vgs = (v.reshape(bs, t, -1) for v in (lats, krs, vgs))
        a_s = _attn_sample(qs.reshape(bs, t, -1), cache_kv_latent, ckr_pad, lats, krs, wk_t, w["wv"], l)
        xs = _mix_out(xs, a_s.reshape(bs * t, -1), ys, w, l)

        xp = _ffn(xp, *f2, l)
        xs = _ffn(xs, *f2, l)

        for lst, val in zip(outs, (btp[:, tail_b:], dtp[:, tail_d:], lats, krs[..., :ROPE_DIM],
                                   bts[:, tail_b:], vgs, dts[:, tail_d:])):
            lst.append(val)
    cb_p, cd_p, lat_s, kr_s, cb_s, vc_s, cd_s = (jnp.stack(o) for o in outs)
    return (xp.reshape(b, s, d), xs.reshape(bs, t, d), stacked[0], stacked[1], cb_p, cd_p,
            lat_s, kr_s, cb_s, vc_s, cd_s)
```

```python
import functools
import math

import jax
import jax.numpy as jnp
from jax import lax
from jax.experimental import pallas as pl
from jax.experimental.pallas import tpu as pltpu

F32 = jnp.float32
BF16 = jnp.bfloat16

CHUNK = 64
A_HEADS = 8
NOPE_DIM = 64
ROPE_DIM = 32
V_DIM = 64
ROPE_THETA = 10000.0
SM_SCALE = (NOPE_DIM + ROPE_DIM) ** -0.5
Q_SCALE = SM_SCALE * math.log2(math.e)
C_GROUPS = 4
GMLP_CHUNK = 128
N_BRANCH = 4
EPS = 1e-6

LANES = 128
SUBLANES = 8
BF16_ROWS = 16
HEAD_PAD = 128
CONV_B_PAD = 32
CONV_D_PAD = 8
VMEM_LIMIT_BYTES = 56 * 1024 * 1024
NEG_BIG = -1e30
ROW_TILES = (1024, 512, 256, 128, 64, 32, 16, 8)


def _pick_tile(n, candidates):
    for c in candidates:
        if n % c == 0:
            return c
    return n


def _layer(arr, l):
    nd = arr.ndim - 1
    return pl.BlockSpec((None,) + arr.shape[1:], lambda *_: (l,) + (0,) * nd, pipeline_mode=pl.Buffered(1))


def _layer_half(arr, l, half):
    _, rows, cols = arr.shape
    return pl.BlockSpec((None, rows, cols // 2), lambda *_: (l, 0, half), pipeline_mode=pl.Buffered(1))


def _weight_specs(names, w, l, cat_half):
    return [_layer_half(w[k], l, cat_half) if k == "w_cat" else _layer(w[k], l) for k in names]


def _whole(arr):
    nd = arr.ndim
    return pl.BlockSpec(arr.shape, lambda *_: (0,) * nd, pipeline_mode=pl.Buffered(1))


def _rms(x, g):
    ms = jnp.mean(x * x, axis=-1, keepdims=True)
    return x * lax.rsqrt(ms + EPS) * g


def _layernorm(x, g, b):
    mu = jnp.mean(x, axis=-1, keepdims=True)
    xc = x - mu
    var = jnp.mean(xc * xc, axis=-1, keepdims=True)
    return xc * lax.rsqrt(var + EPS) * g + b


def _dot(a, b):
    return jnp.dot(a, b, preferred_element_type=F32)


def _dot_nt(a, b):
    return lax.dot_general(a, b, (((1,), (1,)), ((), ())), preferred_element_type=F32)


def _params(sem):
    return pltpu.CompilerParams(dimension_semantics=sem, vmem_limit_bytes=VMEM_LIMIT_BYTES)


def _ffn_kernel(xp_ref, xs_ref, gpre_ref, gpost_ref, wgu_ref, wd_ref, op_ref, os_ref, act_ref, *, d_ff, tf, n_p):
    def ffn(x_ref, o_ref):
        rows = x_ref.shape[0]
        x = x_ref[...]
        n = _rms(x, gpre_ref[...]).astype(BF16)
        for c in range(d_ff // tf):
            gate = _dot(n, wgu_ref[:, c * tf:(c + 1) * tf])
            up = _dot(n, wgu_ref[:, d_ff + c * tf:d_ff + (c + 1) * tf])
            act_ref[0:rows, c * tf:(c + 1) * tf] = (gate * jax.nn.sigmoid(gate) * up).astype(BF16)
        y = _dot(act_ref[0:rows, :], wd_ref[...])
        o_ref[...] = x + 0.5 * _rms(y, gpost_ref[...])

    i = pl.program_id(0)

    @pl.when(i < n_p)
    def _():
        ffn(xp_ref, op_ref)

    @pl.when(i == n_p)
    def _():
        ffn(xs_ref, os_ref)


def _two_stream_specs(tm, n_p, widths_p, arrays_s):
    prompt = [pl.BlockSpec((tm, wd), lambda i: (jnp.minimum(i, n_p - 1), 0)) for wd in widths_p]
    sample = [pl.BlockSpec(a.shape, lambda i: (0, 0)) for a in arrays_s]
    return prompt, sample


def _ffn(xp, xs, gpre, gpost, wgu, wd, l):
    n, d = xp.shape
    d_ff = wd.shape[1]
    tm = _pick_tile(n, ROW_TILES)
    assert xs.shape[0] <= tm
    n_p = n // tm
    tf = _pick_tile(d_ff, (256, 128))
    weights = (gpre, gpost, wgu, wd)
    (spec_p,), (spec_s,) = _two_stream_specs(tm, n_p, (d,), (xs,))
    return pl.pallas_call(
        functools.partial(_ffn_kernel, d_ff=d_ff, tf=tf, n_p=n_p),
        grid=(n_p + 1,),
        in_specs=[spec_p, spec_s] + [_layer(w, l) for w in weights],
        out_specs=(spec_p, spec_s),
        out_shape=(jax.ShapeDtypeStruct(xp.shape, F32), jax.ShapeDtypeStruct(xs.shape, F32)),
        scratch_shapes=[pltpu.VMEM((tm, d_ff), BF16)],
        name="ffn",
        compiler_params=_params(("arbitrary",)),
    )(xp, xs, *weights)


def _rope128(a, c, sneg, spos):
    return (a * c + pltpu.roll(a, LANES - ROPE_DIM // 2, 1) * sneg
            + pltpu.roll(a, ROPE_DIM // 2, 1) * spos)


MIX_IN_WEIGHTS = ("mix_norm_pre", "w_cat", "q_norm", "wq_pad", "kv_norm", "wk_pad", "wv", "conv_b_w",
                  "conv_b_bias", "conv_b_ln_g", "conv_b_ln_b", "gmlp_vn_g", "gmlp_vn_b", "gmlp_w_s", "bs_full",
                  "conv_d_w")
N_MIX_IN_INPUTS = 6 + len(MIX_IN_WEIGHTS)


def _mix_in_kernel(*refs, prompt, n_alias, n_tiles, nt, tq, q_lora, kv_lora, c_b, c_c, c_d, kb_w, kd_w, rb):
    (x_ref, c_ref, sneg_ref, spos_ref, stb_ref, std_ref,
     gpre_ref, win_ref, qn_ref, wq_ref, kvn_ref, wk_ref, wv_ref,
     cbw_ref, cbb_ref, lng_ref, lnb_ref, vng_ref, vnb_ref, ws_ref, bs_ref, cdw_ref) = refs[:N_MIX_IN_INPUTS]
    rest = refs[N_MIX_IN_INPUTS + n_alias:]
    if prompt:
        q_ref, k_ref, vt_ref, y_ref, lat_ref, kr_ref, bt_ref, dt_ref, ha_ref, hb_ref, xpb_ref, xsh_ref, xpd_ref = rest
    else:
        q_ref, y_ref, lat_ref, kr_ref, bt_ref, dt_ref, vg_ref, ha_ref, hb_ref, xpb_ref, xsh_ref, xpd_ref = rest
    i = pl.program_id(0)

    o_kv = q_lora
    o_b = o_kv + kv_lora
    o_c = o_b + 2 * c_b
    o_d = o_c + 2 * c_c
    o_kr = o_d + 3 * c_d
    col_groups = (0, o_kv, o_b, o_c, o_d, o_kr, o_kr + LANES)

    def project(h_ref):
        n = _rms(x_ref[...], gpre_ref[...]).astype(BF16)
        for c0, c1 in zip(col_groups[:-1], col_groups[1:]):
            h_ref[:, c0:c1] = _dot(n, win_ref[:, c0:c1])
            yield

    def consume(h_ref):
        first = (i - 1) % nt == 0
        ctab, sneg, spos = c_ref[...], sneg_ref[...], spos_ref[...]

        qlat = _rms(h_ref[:, 0:o_kv], qn_ref[...]).astype(BF16)
        qa = _dot(qlat, wq_ref[...])
        for h in range(A_HEADS):
            qh = _rope128(qa[:, h * HEAD_PAD:(h + 1) * HEAD_PAD], ctab, sneg, spos) * Q_SCALE
            q_ref[:, h * HEAD_PAD:(h + 1) * HEAD_PAD] = qh.astype(BF16)

        yield
        lat = _rms(h_ref[:, o_kv:o_b], kvn_ref[...])
        lat_ref[...] = lat
        latb = lat.astype(BF16)
        kr = _rope128(h_ref[:, o_kr:o_kr + LANES], ctab, sneg, spos)
        if prompt:
            kr_ref[...] = kr[:, 0:ROPE_DIM]
            kn = _dot(latb, wk_ref[...])
            for h in range(A_HEADS):
                k_ref[:, h * HEAD_PAD:(h + 1) * HEAD_PAD] = (
                    kn[:, h * HEAD_PAD:(h + 1) * HEAD_PAD] + kr).astype(BF16)
            vt_ref[...] = _dot(latb, wv_ref[...]).T.astype(BF16)
        else:
            kr_ref[...] = kr

        yield
        xb = h_ref[:, o_b:o_b + c_b] * jax.nn.sigmoid(h_ref[:, o_b + c_b:o_c])

        @pl.when(first)
        def _():
            xpb_ref[0:CONV_B_PAD, :] = stb_ref[...]
            xpd_ref[0:CONV_D_PAD, :] = std_ref[...]

        @pl.when(jnp.logical_not(first))
        def _():
            xpb_ref[0:CONV_B_PAD, :] = xpb_ref[tq:tq + CONV_B_PAD, :]
            xpd_ref[0:CONV_D_PAD, :] = xpd_ref[tq:tq + CONV_D_PAD, :]

        xpb_ref[CONV_B_PAD:CONV_B_PAD + tq, :] = xb
        bt_ref[...] = xpb_ref[tq:tq + CONV_B_PAD, :]
        n_sh = tq + CONV_B_PAD - SUBLANES
        for r in range(1, SUBLANES):
            xsh_ref[r, 0:n_sh, :] = xpb_ref[r:r + n_sh, :]

        off_b = CONV_B_PAD - (kb_w - 1)
        for blk in range(tq // rb):
            yield
            acc = jnp.broadcast_to(cbb_ref[...], (rb, c_b))
            for kk in range(kb_w):
                a8, r = divmod(off_b + kk, SUBLANES)
                row0 = blk * rb + a8 * SUBLANES
                src = xpb_ref[row0:row0 + rb, :] if r == 0 else xsh_ref[r, row0:row0 + rb, :]
                acc = acc + src * cbw_ref[kk:kk + 1, :]
            yb = _layernorm(acc, lng_ref[...], lnb_ref[...])
            y_ref[blk * rb:(blk + 1) * rb, 0:c_b] = (yb * jax.nn.sigmoid(yb)).astype(BF16)

        yield
        vg = _layernorm(h_ref[:, o_c + c_c:o_d], vng_ref[...], vnb_ref[...])
        if not prompt:
            vg_ref[...] = vg
        vgb = vg.astype(BF16)
        ck = min(tq, GMLP_CHUNK)
        ri = lax.broadcasted_iota(jnp.int32, (ck, ck), 0) // CHUNK
        ci = lax.broadcasted_iota(jnp.int32, (ck, ck), 1) // CHUNK
        lane_grp = lax.broadcasted_iota(jnp.int32, (ck, c_c), 1) // (c_c // C_GROUPS)
        wmix = [jnp.where(ci <= ri, ws_ref[g, 0:ck, 0:ck], 0.0).astype(BF16) for g in range(C_GROUPS)]
        for c in range(tq // ck):
            vc = vgb[c * ck:(c + 1) * ck, :]
            mix = bs_ref[0:ck, :]
            for g in range(C_GROUPS):
                mix = mix + jnp.where(lane_grp == g, _dot(wmix[g], vc), 0.0)
            y_ref[c * ck:(c + 1) * ck, c_b:c_b + c_c] = (
                h_ref[c * ck:(c + 1) * ck, o_c:o_c + c_c] * mix).astype(BF16)

        yield
        xd = h_ref[:, o_d + c_d:o_d + 2 * c_d] * h_ref[:, o_d + 2 * c_d:o_kr]
        xpd_ref[CONV_D_PAD:CONV_D_PAD + tq, :] = xd
        dt_ref[...] = xpd_ref[tq:tq + CONV_D_PAD, :]
        off_d = CONV_D_PAD - (kd_w - 1)
        conv = xd * cdw_ref[kd_w - 1:kd_w, :]
        for kk in range(kd_w - 1):
            conv = conv + xpd_ref[off_d + kk:off_d + kk + tq, :] * cdw_ref[kk:kk + 1, :]
        y_ref[:, c_b + c_c:c_b + c_c + c_d] = (h_ref[:, o_d:o_d + c_d] * conv).astype(BF16)

    bufs = (ha_ref, hb_ref)

    def run(*stages):
        live = list(stages)
        while live:
            for g in list(live):
                if next(g, StopIteration) is StopIteration:
                    live.remove(g)

    @pl.when(i == 0)
    def _():
        run(project(bufs[0]))

    for par in (0, 1):
        @pl.when((i >= 1) & (i < n_tiles) & (i % 2 == par))
        def _(par=par):
            run(project(bufs[par]), consume(bufs[1 - par]))

    @pl.when(i == n_tiles)
    def _():
        run(consume(bufs[(n_tiles - 1) % 2]))


def _mix_in(x, tabs, stb, std, st_layer, w, l, *, b, s, tq, prompt, stacked=None):
    d = x.shape[1]
    depth = w["w_cat"].shape[0]
    q_lora, kv_lora = w["q_norm"].shape[2], w["kv_norm"].shape[2]
    c_b, c_c, c_d = w["conv_b_bias"].shape[2], w["gmlp_vn_g"].shape[2], w["conv_d_w"].shape[2]
    hp = A_HEADS * HEAD_PAD
    rb = min(tq, 64)
    nt = s // tq
    n_tiles = b * nt
    done = lambda i: jnp.maximum(i - 1, 0)
    row_in = pl.BlockSpec((tq, d), lambda i: (jnp.minimum(i, n_tiles - 1), 0))
    row = lambda wd: pl.BlockSpec((tq, wd), lambda i: (done(i), 0))
    per_b = lambda r, wd: pl.BlockSpec((None, r, wd), lambda i: (done(i) // nt, 0, 0))
    if st_layer is None:
        st_spec = per_b
    else:
        st_spec = lambda r, wd: pl.BlockSpec((None, None, r, wd), lambda i: (st_layer, done(i) // nt, 0, 0))
    tab = pl.BlockSpec((tq, LANES), lambda i: (done(i) % nt, 0))
    weights = [w[k] for k in MIX_IN_WEIGHTS]
    sds = jax.ShapeDtypeStruct
    small = [
        (sds((b, CONV_B_PAD, c_b), F32), per_b(CONV_B_PAD, c_b)),
        (sds((b, CONV_D_PAD, c_d), F32), per_b(CONV_D_PAD, c_d)),
    ]
    outs = [(sds((b * s, hp), BF16), row(hp))]
    aliases = {}
    alias_in, alias_specs = [], []
    if prompt:
        vt_spec = pl.BlockSpec((None, None, A_HEADS * V_DIM, tq), lambda i: (done(i) // nt, done(i) % nt, 0, 0))
        stk = lambda wd: pl.BlockSpec((None, None, tq, wd), lambda i: (l, done(i) // nt, done(i) % nt, 0))
        outs += [(sds((b * s, hp), BF16), row(hp)),
                 (sds((b, nt, A_HEADS * V_DIM, tq), BF16), vt_spec),
                 (sds((b * s, c_b + c_c + c_d), BF16), row(c_b + c_c + c_d)),
                 (sds((depth, b, s, kv_lora), F32), stk(kv_lora)),
                 (sds((depth, b, s, ROPE_DIM), F32), stk(ROPE_DIM))]
        outs += small
        if stacked is not None:
            alias_in = list(stacked)
            alias_specs = [pl.BlockSpec(memory_space=pl.ANY)] * 2
            aliases = {N_MIX_IN_INPUTS: 4, N_MIX_IN_INPUTS + 1: 5}
    else:
        outs += [(sds((b * s, c_b + c_c + c_d), BF16), row(c_b + c_c + c_d)),
                 (sds((b * s, kv_lora), F32), row(kv_lora)),
                 (sds((b * s, LANES), F32), row(LANES))]
        outs += small + [(sds((b * s, c_c), F32), row(c_c))]
    out_shape, out_specs = zip(*outs)
    w1 = q_lora + kv_lora + 2 * c_b + 2 * c_c + 3 * c_d + LANES
    assert w1 <= w["w_cat"].shape[2] // 2
    return pl.pallas_call(
        functools.partial(_mix_in_kernel, prompt=prompt, n_alias=len(alias_in), n_tiles=n_tiles, nt=nt, tq=tq,
                          q_lora=q_lora, kv_lora=kv_lora, c_b=c_b, c_c=c_c, c_d=c_d, kb_w=w["kb_w"],
                          kd_w=w["kd_w"], rb=rb),
        grid=(n_tiles + 1,),
        in_specs=[row_in, tab, tab, tab, st_spec(CONV_B_PAD, c_b), st_spec(CONV_D_PAD, c_d)]
        + _weight_specs(MIX_IN_WEIGHTS, w, l, 0) + alias_specs,
        out_specs=out_specs,
        out_shape=out_shape,
        input_output_aliases=aliases,
        scratch_shapes=[pltpu.VMEM((tq, w1), F32), pltpu.VMEM((tq, w1), F32),
                        pltpu.VMEM((CONV_B_PAD + tq, c_b), F32),
                        pltpu.VMEM((SUBLANES, CONV_B_PAD + tq, c_b), F32),
                        pltpu.VMEM((CONV_D_PAD + tq, c_d), F32)],
        name="mix_in",
        compiler_params=_params(("arbitrary",)),
    )(x, *tabs, stb, std, *weights, *alias_in)


def _attn_kernel(q_ref, k_ref, vt_ref, o_ref, st0_ref, st1_ref, m_ref, acc_ref, *, tq):
    qi = pl.program_id(1)
    kc = lax.broadcasted_iota(jnp.int32, (tq, tq), 0) // CHUNK
    qc = lax.broadcasted_iota(jnp.int32, (tq, tq), 1) // CHUNK
    diag_ok = kc <= qc
    heads = tuple(range(A_HEADS))
    ones = jnp.ones((BF16_ROWS, tq), BF16)

    def scores(j, st_ref):
        k0 = pl.multiple_of(j * tq, tq)
        for h in heads:
            st_ref[h] = _dot_nt(k_ref[pl.ds(k0, tq), h * HEAD_PAD:(h + 1) * HEAD_PAD],
                                q_ref[:, h * HEAD_PAD:(h + 1) * HEAD_PAD])

    def update_head(j, st_ref, h, masked):
        st = st_ref[h]
        if masked:
            st = jnp.where(diag_ok, st, NEG_BIG)
        m = m_ref[h]
        m_new = jnp.maximum(m, jnp.max(st, axis=0, keepdims=True))
        alpha = jnp.exp2(m - m_new)
        pb = jnp.exp2((st - m_new).astype(BF16))
        vt1 = jnp.concatenate([vt_ref[j, h * V_DIM:(h + 1) * V_DIM, :], ones], axis=0)
        m_ref[h] = m_new
        acc_ref[h] = alpha * acc_ref[h] + _dot(vt1, pb)

    def update(j, st_ref, masked):
        for h in heads:
            update_head(j, st_ref, h, masked)

    m_ref[...] = jnp.full(m_ref.shape, NEG_BIG, F32)
    acc_ref[...] = jnp.zeros(acc_ref.shape, F32)
    scores(0, st0_ref)

    def both(j_next, next_ref, j, cur_ref):
        k0 = pl.multiple_of(j_next * tq, tq)
        for h in heads:
            next_ref[h] = _dot_nt(k_ref[pl.ds(k0, tq), h * HEAD_PAD:(h + 1) * HEAD_PAD],
                                  q_ref[:, h * HEAD_PAD:(h + 1) * HEAD_PAD])
            update_head(j, cur_ref, h, False)

    def pair(p, _):
        j = 2 * p
        both(j + 1, st1_ref, j, st0_ref)
        both(j + 2, st0_ref, j + 1, st1_ref)
        return 0

    lax.fori_loop(0, qi // 2, pair, 0)

    @pl.when(qi % 2 == 0)
    def _():
        update(qi, st0_ref, True)

    @pl.when(qi % 2 == 1)
    def _():
        both(qi, st1_ref, qi - 1, st0_ref)
        update(qi, st1_ref, True)

    ot = jnp.concatenate([acc_ref[h, 0:V_DIM, :] / acc_ref[h, V_DIM:V_DIM + 1, :] for h in heads], axis=0)
    o_ref[...] = ot.T.astype(BF16)


def _attn_prompt(q, k, vt, *, tq):
    b, s, hp = q.shape
    nblk, wv = vt.shape[1], vt.shape[2]
    return pl.pallas_call(
        functools.partial(_attn_kernel, tq=tq),
        grid=(b, s // tq),
        in_specs=[
            pl.BlockSpec((None, tq, hp), lambda i, j: (i, j, 0)),
            pl.BlockSpec((None, s, hp), lambda i, j: (i, 0, 0)),
            pl.BlockSpec((None, nblk, wv, tq), lambda i, j: (i, 0, 0, 0)),
        ],
        out_specs=pl.BlockSpec((None, tq, wv), lambda i, j: (i, j, 0)),
        out_shape=jax.ShapeDtypeStruct((b, s, wv), BF16),
        scratch_shapes=[pltpu.VMEM((A_HEADS, tq, tq), F32), pltpu.VMEM((A_HEADS, tq, tq), F32),
                        pltpu.VMEM((A_HEADS, 1, tq), F32), pltpu.VMEM((A_HEADS, V_DIM + BF16_ROWS, tq), F32)],
        name="attn_prompt",
        compiler_params=_params(("parallel", "parallel")),
    )(q, k, vt)


def _attn_sample_kernel(q_ref, clat_ref, ckr_ref, lat_ref, kr_ref, wkt_ref, wv_ref, o_ref, kn_ref, *, t):
    q = q_ref[...]
    qh = [q[:, h * HEAD_PAD:(h + 1) * HEAD_PAD] for h in range(A_HEADS)]
    qabs = jnp.concatenate(
        [_dot(qh[h], wkt_ref[h * HEAD_PAD:(h + 1) * HEAD_PAD, :]) for h in range(A_HEADS)], axis=0)
    qcat = jnp.concatenate([qabs.astype(BF16), jnp.concatenate(qh, axis=0)], axis=1)
    clat = clat_ref[...].astype(BF16)
    kv_lora = clat.shape[1]
    kc = jnp.concatenate([clat, ckr_ref[...]], axis=1)
    kn_ref[...] = jnp.zeros(kn_ref.shape, BF16)
    kn_ref[0:t, :] = jnp.concatenate([lat_ref[...], kr_ref[...]], axis=1).astype(BF16)
    kn = kn_ref[...]
    s1 = _dot_nt(qcat, kc)
    s2 = _dot_nt(qcat, kn)
    s2 = jnp.where(lax.broadcasted_iota(jnp.int32, s2.shape, 1) < t, s2, NEG_BIG)
    m = jnp.maximum(jnp.max(s1, axis=-1, keepdims=True), jnp.max(s2, axis=-1, keepdims=True))
    p1 = jnp.exp2(s1 - m)
    p2 = jnp.exp2(s2 - m)
    l = jnp.sum(p1, axis=-1, keepdims=True) + jnp.sum(p2, axis=-1, keepdims=True)
    olat = ((_dot(p1.astype(BF16), clat) + _dot(p2.astype(BF16), kn[:, 0:kv_lora])) / l).astype(BF16)
    low_half = lax.broadcasted_iota(jnp.int32, (t, LANES), 1) < V_DIM
    for pair in range(A_HEADS // 2):
        wpair = wv_ref[:, pair * LANES:(pair + 1) * LANES]
        lo = _dot(olat[(2 * pair) * t:(2 * pair + 1) * t, :], wpair)
        hi = _dot(olat[(2 * pair + 1) * t:(2 * pair + 2) * t, :], wpair)
        o_ref[:, pair * LANES:(pair + 1) * LANES] = jnp.where(low_half, lo, hi).astype(BF16)


def _attn_sample(q, cache_lat, cache_kr, lat, kr, wkt, wv, l):
    b, t, hp = q.shape
    past, kv_lora = cache_lat.shape[2], cache_lat.shape[3]
    wvw = wv.shape[2]
    per_b = lambda r, w: pl.BlockSpec((None, r, w), lambda i: (i, 0, 0))
    cache = lambda r, w: pl.BlockSpec((None, None, r, w), lambda i: (l, i, 0, 0))
    return pl.pallas_call(
        functools.partial(_attn_sample_kernel, t=t),
        grid=(b,),
        in_specs=[per_b(t, hp), cache(past, kv_lora), cache(past, LANES), per_b(t, kv_lora),
                  per_b(t, LANES), _layer(wkt, l), _layer(wv, l)],
        out_specs=per_b(t, wvw),
        out_shape=jax.ShapeDtypeStruct((b, t, wvw), BF16),
        scratch_shapes=[pltpu.VMEM((LANES, kv_lora + LANES), BF16)],
        name="attn_sample",
        compiler_params=_params(("parallel",)),
    )(q, cache_lat, cache_kr, lat, kr, wkt, wv)


MIX_OUT_WEIGHTS = ("mix_norm_pre", "mix_norm_post", "w_cat", "w_br_a", "w_br_b", "w_br_c", "w_br_d", "w_o")


def _mix_out_kernel(xp_ref, ap_ref, yp_ref, xs_ref, as_ref, ys_ref, gpre_ref, gpost_ref, wg_ref, wba_ref, wbb_ref,
                    wbc_ref, wbd_ref, wo_ref, op_ref, os_ref, *, d, c_b, c_c, c_d, n_p):
    def merge(x_ref, a_ref, y_ref, o_ref):
        x = x_ref[...]
        n = _rms(x, gpre_ref[...]).astype(BF16)
        branches = (
            (a_ref[...], wba_ref),
            (y_ref[:, 0:c_b], wbb_ref),
            (y_ref[:, c_b:c_b + c_c], wbc_ref),
            (y_ref[:, c_b + c_c:c_b + c_c + c_d], wbd_ref),
        )
        merged = jnp.zeros(x.shape, F32)
        for k, (br, w_ref) in enumerate(branches):
            gate = jax.nn.sigmoid(_dot(n, wg_ref[:, k * d:(k + 1) * d]))
            merged = merged + gate * _dot(br, w_ref[...])
        out = _dot(merged.astype(BF16), wo_ref[...])
        o_ref[...] = x + _rms(out, gpost_ref[...])

    i = pl.program_id(0)

    @pl.when(i < n_p)
    def _():
        merge(xp_ref, ap_ref, yp_ref, op_ref)

    @pl.when(i == n_p)
    def _():
        merge(xs_ref, as_ref, ys_ref, os_ref)


def _mix_out(xp, ap, yp, xs, a_s, ys, w, l):
    n, d = xp.shape
    c_b, c_c, c_d = w["w_br_b"].shape[1], w["w_br_c"].shape[1], w["w_br_d"].shape[1]
    tm = _pick_tile(n, ROW_TILES)
    assert xs.shape[0] <= tm
    n_p = n // tm
    weights = [w[k] for k in MIX_OUT_WEIGHTS]
    specs_p, specs_s = _two_stream_specs(tm, n_p, (d, ap.shape[1], yp.shape[1]), (xs, a_s, ys))
    return pl.pallas_call(
        functools.partial(_mix_out_kernel, d=d, c_b=c_b, c_c=c_c, c_d=c_d, n_p=n_p),
        grid=(n_p + 1,),
        in_specs=specs_p + specs_s + _weight_specs(MIX_OUT_WEIGHTS, w, l, 1),
        out_specs=(specs_p[0], specs_s[0]),
        out_shape=(jax.ShapeDtypeStruct(xp.shape, F32), jax.ShapeDtypeStruct(xs.shape, F32)),
        name="mix_out",
        compiler_params=_params(("arbitrary",)),
    )(xp, ap, yp, xs, a_s, ys, *weights)


def _rope_tabs(pos):
    half = ROPE_DIM // 2
    inv = jnp.exp(-math.log(ROPE_THETA) * jnp.arange(half, dtype=F32) / half)
    ang = pos.astype(F32)[:, None] * inv[None, :]
    cos, sin = jnp.cos(ang), jnp.sin(ang)
    n = pos.shape[0]
    z = lambda w: jnp.zeros((n, w), F32)
    ctab = jnp.concatenate([cos, cos, jnp.ones((n, NOPE_DIM), F32), z(LANES - ROPE_DIM - NOPE_DIM)], axis=1)
    sneg = jnp.concatenate([-sin, z(LANES - half)], axis=1)
    spos = jnp.concatenate([z(half), sin, z(LANES - 2 * half)], axis=1)
    return ctab, sneg, spos


def _pad_rows(w, rows):
    return jnp.pad(w, ((0, 0), (0, rows - w.shape[1]), (0, 0)))


def kernel(x_prompt, x_sample, cache_kv_latent, cache_k_rope, state_conv_b, state_conv_d, ffn1_norm_pre, ffn1_norm_post, ffn1_w_gu, ffn1_w_down, mix_norm_pre, mix_norm_post, w_in, q_norm, w_uq, kv_norm, w_ukv, conv_b_w, conv_b_bias, conv_b_ln_g, conv_b_ln_b, gmlp_vn_g, gmlp_vn_b, gmlp_w_s, gmlp_b_s, conv_d_w, w_br_a, w_br_b, w_br_c, w_br_d, w_o, ffn2_norm_pre, ffn2_norm_post, ffn2_w_gu, ffn2_w_down):
    b, s, d = x_prompt.shape
    bs, t, _ = x_sample.shape
    depth = w_in.shape[0]
    past = cache_kv_latent.shape[2]
    q_lora, kv_lora = q_norm.shape[1], kv_norm.shape[1]
    c_b, c_c, c_d = conv_b_bias.shape[1], gmlp_vn_g.shape[1], conv_d_w.shape[2]
    kb_w, kd_w = conv_b_w.shape[1], conv_d_w.shape[1]
    assert kb_w - 1 <= CONV_B_PAD and kd_w - 1 <= CONV_D_PAD
    assert s % GMLP_CHUNK == 0 and t <= GMLP_CHUNK and t % 16 == 0 and past % CHUNK == 0 and t <= CHUNK
    assert gmlp_w_s.shape[1] == C_GROUPS and w_ukv.shape[2] == A_HEADS * (NOPE_DIM + V_DIM)

    o_q = q_lora
    o_kv = o_q + kv_lora
    o_kr = o_kv + ROPE_DIM
    o_b = o_kr + 2 * c_b
    o_c = o_b + 2 * c_c
    o_d = o_c + 3 * c_d

    vec = lambda p: p[:, None, :]
    uq = w_uq.reshape(depth, q_lora, A_HEADS, NOPE_DIM + ROPE_DIM)
    ukv = w_ukv.reshape(depth, kv_lora, A_HEADS, NOPE_DIM + V_DIM)
    head_zeros = lambda rows, width: jnp.zeros((depth, rows, A_HEADS, width), F32)
    wk_pad = jnp.concatenate(
        [head_zeros(kv_lora, ROPE_DIM), ukv[..., :NOPE_DIM], head_zeros(kv_lora, HEAD_PAD - NOPE_DIM - ROPE_DIM)],
        axis=3).reshape(depth, kv_lora, A_HEADS * HEAD_PAD).astype(BF16)
    w = dict(
        mix_norm_pre=vec(mix_norm_pre), mix_norm_post=vec(mix_norm_post),
        w_cat=jnp.concatenate(
            [w_in[:, :, 0:o_kv], w_in[:, :, o_kr:o_d], w_in[:, :, o_kv:o_kr],
             jnp.zeros((depth, d, N_BRANCH * d - o_d), F32), w_in[:, :, o_d:]], axis=2).astype(BF16),
        q_norm=vec(q_norm),
        wq_pad=jnp.concatenate(
            [uq[..., NOPE_DIM:], uq[..., :NOPE_DIM], head_zeros(q_lora, HEAD_PAD - NOPE_DIM - ROPE_DIM)],
            axis=3).reshape(depth, q_lora, A_HEADS * HEAD_PAD).astype(BF16),
        kv_norm=vec(kv_norm), wk_pad=wk_pad,
        wv=ukv[..., NOPE_DIM:].reshape(depth, kv_lora, A_HEADS * V_DIM).astype(BF16),
        conv_b_w=_pad_rows(conv_b_w, CONV_B_PAD), conv_b_bias=vec(conv_b_bias),
        conv_b_ln_g=vec(conv_b_ln_g), conv_b_ln_b=vec(conv_b_ln_b),
        gmlp_vn_g=vec(gmlp_vn_g), gmlp_vn_b=vec(gmlp_vn_b), gmlp_w_s=gmlp_w_s,
        bs_full=jnp.repeat(jnp.swapaxes(gmlp_b_s, 1, 2), c_c // C_GROUPS, axis=2),
        conv_d_w=_pad_rows(conv_d_w, CONV_D_PAD), kb_w=kb_w, kd_w=kd_w,
        w_br_a=w_br_a.astype(BF16), w_br_b=w_br_b.astype(BF16), w_br_c=w_br_c.astype(BF16),
        w_br_d=w_br_d.astype(BF16), w_o=w_o.astype(BF16))
    wk_t = jnp.swapaxes(wk_pad, 1, 2)
    f1 = (vec(ffn1_norm_pre), vec(ffn1_norm_post), ffn1_w_gu.astype(BF16), ffn1_w_down.astype(BF16))
    f2 = (vec(ffn2_norm_pre), vec(ffn2_norm_post), ffn2_w_gu.astype(BF16), ffn2_w_down.astype(BF16))

    tabs_p = _rope_tabs(jnp.arange(s))
    tabs_s = _rope_tabs(past + jnp.arange(t))
    zero_b = jnp.zeros((b, CONV_B_PAD, c_b), F32)
    zero_d = jnp.zeros((b, CONV_D_PAD, c_d), F32)
    tail_b = CONV_B_PAD - (kb_w - 1)
    tail_d = CONV_D_PAD - (kd_w - 1)
    st_b = jnp.pad(state_conv_b, ((0, 0), (0, 0), (tail_b, 0), (0, 0)))
    st_d = jnp.pad(state_conv_d, ((0, 0), (0, 0), (tail_d, 0), (0, 0)))
    ckr_pad = jnp.pad(cache_k_rope, ((0, 0), (0, 0), (0, 0), (0, LANES - ROPE_DIM))).astype(BF16)
    tq = _pick_tile(s, (256, 128))

    xp = x_prompt.reshape(b * s, d)
    xs = x_sample.reshape(bs * t, d)
    stacked = None
    outs = [[] for _ in range(7)]
    for l in range(depth):
        xp, xs = _ffn(xp, xs, *f1, l)

        qp, kp, vtp, yp, lat_all, kr_all, btp, dtp = _mix_in(
            xp, tabs_p, zero_b, zero_d, None, w, l, b=b, s=s, tq=tq, prompt=True, stacked=stacked)
        stacked = (lat_all, kr_all)
        ap = _attn_prompt(qp.reshape(b, s, -1), kp.reshape(b, s, -1), vtp, tq=tq)

        qs, ys, lats, krs, bts, dts, vgs = _mix_in(
            xs, tabs_s, st_b, st_d, l, w, l, b=bs, s=t, tq=t, prompt=False)
        lats, krs, vgs = (v.reshape(bs, t, -1) for v in (lats, krs, vgs))
        a_s = _attn_sample(qs.reshape(bs, t, -1), cache_kv_latent, ckr_pad, lats, krs, wk_t, w["wv"], l)

        xp, xs = _mix_out(xp, ap.reshape(b * s, -1), yp, xs, a_s.reshape(bs * t, -1), ys, w, l)
        xp, xs = _ffn(xp, xs, *f2, l)

        for lst, val in zip(outs, (btp[:, tail_b:], dtp[:, tail_d:], lats, krs[..., :ROPE_DIM],
                                   bts[:, tail_b:], vgs, dts[:, tail_d:])):
            lst.append(val)
    cb_p, cd_p, lat_s, kr_s, cb_s, vc_s, cd_s = (jnp.stack(o) for o in outs)
    return (xp.reshape(b, s, d), xs.reshape(bs, t, d), stacked[0], stacked[1], cb_p, cd_p,
            lat_s, kr_s, cb_s, vc_s, cd_s)
```

```python
import functools
import math

import jax
import jax.numpy as jnp
from jax import lax
from jax.experimental import pallas as pl
from jax.experimental.pallas import tpu as pltpu

F32 = jnp.float32
BF16 = jnp.bfloat16

CHUNK = 64
A_HEADS = 8
NOPE_DIM = 64
ROPE_DIM = 32
V_DIM = 64
ROPE_THETA = 10000.0
SM_SCALE = (NOPE_DIM + ROPE_DIM) ** -0.5
Q_SCALE = SM_SCALE * math.log2(math.e)
C_GROUPS = 4
GMLP_CHUNK = 128
N_BRANCH = 4
EPS = 1e-6

LANES = 128
SUBLANES = 8
BF16_ROWS = 16
HEAD_PAD = 128
CONV_B_PAD = 32
CONV_D_PAD = 8
VMEM_LIMIT_BYTES = 56 * 1024 * 1024
NEG_BIG = -1e30
ROW_TILES = (1024, 512, 256, 128, 64, 32, 16, 8)


def _pick_tile(n, candidates):
    for c in candidates:
        if n % c == 0:
            return c
    return n


def _layer(arr, l):
    nd = arr.ndim - 1
    return pl.BlockSpec((None,) + arr.shape[1:], lambda *_: (l,) + (0,) * nd, pipeline_mode=pl.Buffered(1))


def _whole(arr):
    nd = arr.ndim
    return pl.BlockSpec(arr.shape, lambda *_: (0,) * nd, pipeline_mode=pl.Buffered(1))


def _rms(x, g):
    ms = jnp.mean(x * x, axis=-1, keepdims=True)
    return x * lax.rsqrt(ms + EPS) * g


def _layernorm(x, g, b):
    mu = jnp.mean(x, axis=-1, keepdims=True)
    xc = x - mu
    var = jnp.mean(xc * xc, axis=-1, keepdims=True)
    return xc * lax.rsqrt(var + EPS) * g + b


def _dot(a, b):
    return jnp.dot(a, b, preferred_element_type=F32)


def _dot_nt(a, b):
    return lax.dot_general(a, b, (((1,), (1,)), ((), ())), preferred_element_type=F32)


def _params(sem):
    return pltpu.CompilerParams(dimension_semantics=sem, vmem_limit_bytes=VMEM_LIMIT_BYTES)


def _ffn_kernel(xp_ref, xs_ref, gpre_ref, gpost_ref, wgu_ref, wd_ref, op_ref, os_ref, act_ref, *, d_ff, tf, n_p):
    def ffn(x_ref, o_ref):
        rows = x_ref.shape[0]
        x = x_ref[...]
        n = _rms(x, gpre_ref[...]).astype(BF16)
        for c in range(d_ff // tf):
            gate = _dot(n, wgu_ref[:, c * tf:(c + 1) * tf])
            up = _dot(n, wgu_ref[:, d_ff + c * tf:d_ff + (c + 1) * tf])
            act_ref[0:rows, c * tf:(c + 1) * tf] = (gate * jax.nn.sigmoid(gate) * up).astype(BF16)
        y = _dot(act_ref[0:rows, :], wd_ref[...])
        o_ref[...] = x + 0.5 * _rms(y, gpost_ref[...])

    i = pl.program_id(0)

    @pl.when(i < n_p)
    def _():
        ffn(xp_ref, op_ref)

    @pl.when(i == n_p)
    def _():
        ffn(xs_ref, os_ref)


def _two_stream_specs(tm, n_p, widths_p, arrays_s):
    prompt = [pl.BlockSpec((tm, wd), lambda i: (jnp.minimum(i, n_p - 1), 0)) for wd in widths_p]
    sample = [pl.BlockSpec(a.shape, lambda i: (0, 0)) for a in arrays_s]
    return prompt, sample


def _ffn(xp, xs, gpre, gpost, wgu, wd, l):
    n, d = xp.shape
    d_ff = wd.shape[1]
    tm = _pick_tile(n, ROW_TILES)
    assert xs.shape[0] <= tm
    n_p = n // tm
    tf = _pick_tile(d_ff, (256, 128))
    weights = (gpre, gpost, wgu, wd)
    (spec_p,), (spec_s,) = _two_stream_specs(tm, n_p, (d,), (xs,))
    return pl.pallas_call(
        functools.partial(_ffn_kernel, d_ff=d_ff, tf=tf, n_p=n_p),
        grid=(n_p + 1,),
        in_specs=[spec_p, spec_s] + [_layer(w, l) for w in weights],
        out_specs=(spec_p, spec_s),
        out_shape=(jax.ShapeDtypeStruct(xp.shape, F32), jax.ShapeDtypeStruct(xs.shape, F32)),
        scratch_shapes=[pltpu.VMEM((tm, d_ff), BF16)],
        name="ffn",
        compiler_params=_params(("arbitrary",)),
    )(xp, xs, *weights)


def _rope128(a, c, sneg, spos):
    return (a * c + pltpu.roll(a, LANES - ROPE_DIM // 2, 1) * sneg
            + pltpu.roll(a, ROPE_DIM // 2, 1) * spos)


MIX_IN_WEIGHTS = ("mix_norm_pre", "w_in1", "q_norm", "wq_pad", "kv_norm", "wk_pad", "wv", "conv_b_w",
                  "conv_b_bias", "conv_b_ln_g", "conv_b_ln_b", "gmlp_vn_g", "gmlp_vn_b", "gmlp_w_s", "bs_full",
                  "conv_d_w")
N_MIX_IN_INPUTS = 6 + len(MIX_IN_WEIGHTS)


def _mix_in_kernel(*refs, prompt, n_alias, n_tiles, nt, tq, q_lora, kv_lora, c_b, c_c, c_d, kb_w, kd_w, rb):
    (x_ref, c_ref, sneg_ref, spos_ref, stb_ref, std_ref,
     gpre_ref, win_ref, qn_ref, wq_ref, kvn_ref, wk_ref, wv_ref,
     cbw_ref, cbb_ref, lng_ref, lnb_ref, vng_ref, vnb_ref, ws_ref, bs_ref, cdw_ref) = refs[:N_MIX_IN_INPUTS]
    rest = refs[N_MIX_IN_INPUTS + n_alias:]
    if prompt:
        q_ref, k_ref, vt_ref, y_ref, lat_ref, kr_ref, bt_ref, dt_ref, ha_ref, hb_ref, xpb_ref, xsh_ref, xpd_ref = rest
    else:
        q_ref, y_ref, lat_ref, kr_ref, bt_ref, dt_ref, vg_ref, ha_ref, hb_ref, xpb_ref, xsh_ref, xpd_ref = rest
    i = pl.program_id(0)

    o_kv = q_lora
    o_b = o_kv + kv_lora
    o_c = o_b + 2 * c_b
    o_d = o_c + 2 * c_c
    o_kr = o_d + 3 * c_d
    col_groups = (0, o_kv, o_b, o_c, o_d, o_kr, o_kr + LANES)

    def project(h_ref):
        n = _rms(x_ref[...], gpre_ref[...]).astype(BF16)
        for c0, c1 in zip(col_groups[:-1], col_groups[1:]):
            h_ref[:, c0:c1] = _dot(n, win_ref[:, c0:c1])
            yield

    def consume(h_ref):
        first = (i - 1) % nt == 0
        ctab, sneg, spos = c_ref[...], sneg_ref[...], spos_ref[...]

        qlat = _rms(h_ref[:, 0:o_kv], qn_ref[...]).astype(BF16)
        qa = _dot(qlat, wq_ref[...])
        for h in range(A_HEADS):
            qh = _rope128(qa[:, h * HEAD_PAD:(h + 1) * HEAD_PAD], ctab, sneg, spos) * Q_SCALE
            q_ref[:, h * HEAD_PAD:(h + 1) * HEAD_PAD] = qh.astype(BF16)

        yield
        lat = _rms(h_ref[:, o_kv:o_b], kvn_ref[...])
        lat_ref[...] = lat
        latb = lat.astype(BF16)
        kr = _rope128(h_ref[:, o_kr:o_kr + LANES], ctab, sneg, spos)
        if prompt:
            kr_ref[...] = kr[:, 0:ROPE_DIM]
            kn = _dot(latb, wk_ref[...])
            for h in range(A_HEADS):
                k_ref[:, h * HEAD_PAD:(h + 1) * HEAD_PAD] = (
                    kn[:, h * HEAD_PAD:(h + 1) * HEAD_PAD] + kr).astype(BF16)
            vt_ref[...] = _dot(latb, wv_ref[...]).T.astype(BF16)
        else:
            kr_ref[...] = kr

        yield
        xb = h_ref[:, o_b:o_b + c_b] * jax.nn.sigmoid(h_ref[:, o_b + c_b:o_c])

        @pl.when(first)
        def _():
            xpb_ref[0:CONV_B_PAD, :] = stb_ref[...]
            xpd_ref[0:CONV_D_PAD, :] = std_ref[...]

        @pl.when(jnp.logical_not(first))
        def _():
            xpb_ref[0:CONV_B_PAD, :] = xpb_ref[tq:tq + CONV_B_PAD, :]
            xpd_ref[0:CONV_D_PAD, :] = xpd_ref[tq:tq + CONV_D_PAD, :]

        xpb_ref[CONV_B_PAD:CONV_B_PAD + tq, :] = xb
        bt_ref[...] = xpb_ref[tq:tq + CONV_B_PAD, :]
        n_sh = tq + CONV_B_PAD - SUBLANES
        for r in range(1, SUBLANES):
            xsh_ref[r, 0:n_sh, :] = xpb_ref[r:r + n_sh, :]

        off_b = CONV_B_PAD - (kb_w - 1)
        for blk in range(tq // rb):
            yield
            acc = jnp.broadcast_to(cbb_ref[...], (rb, c_b))
            for kk in range(kb_w):
                a8, r = divmod(off_b + kk, SUBLANES)
                row0 = blk * rb + a8 * SUBLANES
                src = xpb_ref[row0:row0 + rb, :] if r == 0 else xsh_ref[r, row0:row0 + rb, :]
                acc = acc + src * cbw_ref[kk:kk + 1, :]
            yb = _layernorm(acc, lng_ref[...], lnb_ref[...])
            y_ref[blk * rb:(blk + 1) * rb, 0:c_b] = (yb * jax.nn.sigmoid(yb)).astype(BF16)

        yield
        vg = _layernorm(h_ref[:, o_c + c_c:o_d], vng_ref[...], vnb_ref[...])
        if not prompt:
            vg_ref[...] = vg
        vgb = vg.astype(BF16)
        ck = min(tq, GMLP_CHUNK)
        ri = lax.broadcasted_iota(jnp.int32, (ck, ck), 0) // CHUNK
        ci = lax.broadcasted_iota(jnp.int32, (ck, ck), 1) // CHUNK
        lane_grp = lax.broadcasted_iota(jnp.int32, (ck, c_c), 1) // (c_c // C_GROUPS)
        wmix = [jnp.where(ci <= ri, ws_ref[g, 0:ck, 0:ck], 0.0).astype(BF16) for g in range(C_GROUPS)]
        for c in range(tq // ck):
            vc = vgb[c * ck:(c + 1) * ck, :]
            mix = bs_ref[0:ck, :]
            for g in range(C_GROUPS):
                mix = mix + jnp.where(lane_grp == g, _dot(wmix[g], vc), 0.0)
            y_ref[c * ck:(c + 1) * ck, c_b:c_b + c_c] = (
                h_ref[c * ck:(c + 1) * ck, o_c:o_c + c_c] * mix).astype(BF16)

        yield
        xd = h_ref[:, o_d + c_d:o_d + 2 * c_d] * h_ref[:, o_d + 2 * c_d:o_kr]
        xpd_ref[CONV_D_PAD:CONV_D_PAD + tq, :] = xd
        dt_ref[...] = xpd_ref[tq:tq + CONV_D_PAD, :]
        off_d = CONV_D_PAD - (kd_w - 1)
        conv = xd * cdw_ref[kd_w - 1:kd_w, :]
        for kk in range(kd_w - 1):
            conv = conv + xpd_ref[off_d + kk:off_d + kk + tq, :] * cdw_ref[kk:kk + 1, :]
        y_ref[:, c_b + c_c:c_b + c_c + c_d] = (h_ref[:, o_d:o_d + c_d] * conv).astype(BF16)

    bufs = (ha_ref, hb_ref)

    def run(*stages):
        live = list(stages)
        while live:
            for g in list(live):
                if next(g, StopIteration) is StopIteration:
                    live.remove(g)

    @pl.when(i == 0)
    def _():
        run(project(bufs[0]))

    for par in (0, 1):
        @pl.when((i >= 1) & (i < n_tiles) & (i % 2 == par))
        def _(par=par):
            run(project(bufs[par]), consume(bufs[1 - par]))

    @pl.when(i == n_tiles)
    def _():
        run(consume(bufs[(n_tiles - 1) % 2]))


def _mix_in(x, tabs, stb, std, st_layer, w, l, *, b, s, tq, prompt, stacked=None):
    d = x.shape[1]
    depth = w["w_in1"].shape[0]
    q_lora, kv_lora = w["q_norm"].shape[2], w["kv_norm"].shape[2]
    c_b, c_c, c_d = w["conv_b_bias"].shape[2], w["gmlp_vn_g"].shape[2], w["conv_d_w"].shape[2]
    hp = A_HEADS * HEAD_PAD
    rb = min(tq, 64)
    nt = s // tq
    n_tiles = b * nt
    done = lambda i: jnp.maximum(i - 1, 0)
    row_in = pl.BlockSpec((tq, d), lambda i: (jnp.minimum(i, n_tiles - 1), 0))
    row = lambda wd: pl.BlockSpec((tq, wd), lambda i: (done(i), 0))
    per_b = lambda r, wd: pl.BlockSpec((None, r, wd), lambda i: (done(i) // nt, 0, 0))
    if st_layer is None:
        st_spec = per_b
    else:
        st_spec = lambda r, wd: pl.BlockSpec((None, None, r, wd), lambda i: (st_layer, done(i) // nt, 0, 0))
    tab = pl.BlockSpec((tq, LANES), lambda i: (done(i) % nt, 0))
    weights = [w[k] for k in MIX_IN_WEIGHTS]
    sds = jax.ShapeDtypeStruct
    small = [
        (sds((b, CONV_B_PAD, c_b), F32), per_b(CONV_B_PAD, c_b)),
        (sds((b, CONV_D_PAD, c_d), F32), per_b(CONV_D_PAD, c_d)),
    ]
    outs = [(sds((b * s, hp), BF16), row(hp))]
    aliases = {}
    alias_in, alias_specs = [], []
    if prompt:
        vt_spec = pl.BlockSpec((None, None, A_HEADS * V_DIM, tq), lambda i: (done(i) // nt, done(i) % nt, 0, 0))
        stk = lambda wd: pl.BlockSpec((None, None, tq, wd), lambda i: (l, done(i) // nt, done(i) % nt, 0))
        outs += [(sds((b * s, hp), BF16), row(hp)),
                 (sds((b, nt, A_HEADS * V_DIM, tq), BF16), vt_spec),
                 (sds((b * s, c_b + c_c + c_d), BF16), row(c_b + c_c + c_d)),
                 (sds((depth, b, s, kv_lora), F32), stk(kv_lora)),
                 (sds((depth, b, s, ROPE_DIM), F32), stk(ROPE_DIM))]
        outs += small
        if stacked is not None:
            alias_in = list(stacked)
            alias_specs = [pl.BlockSpec(memory_space=pl.ANY)] * 2
            aliases = {N_MIX_IN_INPUTS: 4, N_MIX_IN_INPUTS + 1: 5}
    else:
        outs += [(sds((b * s, c_b + c_c + c_d), BF16), row(c_b + c_c + c_d)),
                 (sds((b * s, kv_lora), F32), row(kv_lora)),
                 (sds((b * s, LANES), F32), row(LANES))]
        outs += small + [(sds((b * s, c_c), F32), row(c_c))]
    out_shape, out_specs = zip(*outs)
    w1 = w["w_in1"].shape[2]
    return pl.pallas_call(
        functools.partial(_mix_in_kernel, prompt=prompt, n_alias=len(alias_in), n_tiles=n_tiles, nt=nt, tq=tq,
                          q_lora=q_lora, kv_lora=kv_lora, c_b=c_b, c_c=c_c, c_d=c_d, kb_w=w["kb_w"],
                          kd_w=w["kd_w"], rb=rb),
        grid=(n_tiles + 1,),
        in_specs=[row_in, tab, tab, tab, st_spec(CONV_B_PAD, c_b), st_spec(CONV_D_PAD, c_d)]
        + [_layer(a, l) for a in weights] + alias_specs,
        out_specs=out_specs,
        out_shape=out_shape,
        input_output_aliases=aliases,
        scratch_shapes=[pltpu.VMEM((tq, w1), F32), pltpu.VMEM((tq, w1), F32),
                        pltpu.VMEM((CONV_B_PAD + tq, c_b), F32),
                        pltpu.VMEM((SUBLANES, CONV_B_PAD + tq, c_b), F32),
                        pltpu.VMEM((CONV_D_PAD + tq, c_d), F32)],
        name="mix_in",
        compiler_params=_params(("arbitrary",)),
    )(x, *tabs, stb, std, *weights, *alias_in)


def _attn_kernel(q_ref, k_ref, vt_ref, o_ref, st0_ref, st1_ref, m_ref, acc_ref, *, tq):
    qi = pl.program_id(1)
    kc = lax.broadcasted_iota(jnp.int32, (tq, tq), 0) // CHUNK
    qc = lax.broadcasted_iota(jnp.int32, (tq, tq), 1) // CHUNK
    diag_ok = kc <= qc
    heads = tuple(range(A_HEADS))
    ones = jnp.ones((BF16_ROWS, tq), BF16)

    def scores(j, st_ref):
        k0 = pl.multiple_of(j * tq, tq)
        for h in heads:
            st_ref[h] = _dot_nt(k_ref[pl.ds(k0, tq), h * HEAD_PAD:(h + 1) * HEAD_PAD],
                                q_ref[:, h * HEAD_PAD:(h + 1) * HEAD_PAD])

    def update_head(j, st_ref, h, masked):
        st = st_ref[h]
        if masked:
            st = jnp.where(diag_ok, st, NEG_BIG)
        m = m_ref[h]
        m_new = jnp.maximum(m, jnp.max(st, axis=0, keepdims=True))
        alpha = jnp.exp2(m - m_new)
        pb = jnp.exp2((st - m_new).astype(BF16))
        vt1 = jnp.concatenate([vt_ref[j, h * V_DIM:(h + 1) * V_DIM, :], ones], axis=0)
        m_ref[h] = m_new
        acc_ref[h] = alpha * acc_ref[h] + _dot(vt1, pb)

    def update(j, st_ref, masked):
        for h in heads:
            update_head(j, st_ref, h, masked)

    m_ref[...] = jnp.full(m_ref.shape, NEG_BIG, F32)
    acc_ref[...] = jnp.zeros(acc_ref.shape, F32)
    scores(0, st0_ref)

    def both(j_next, next_ref, j, cur_ref):
        k0 = pl.multiple_of(j_next * tq, tq)
        for h in heads:
            next_ref[h] = _dot_nt(k_ref[pl.ds(k0, tq), h * HEAD_PAD:(h + 1) * HEAD_PAD],
                                  q_ref[:, h * HEAD_PAD:(h + 1) * HEAD_PAD])
            update_head(j, cur_ref, h, False)

    def pair(p, _):
        j = 2 * p
        both(j + 1, st1_ref, j, st0_ref)
        both(j + 2, st0_ref, j + 1, st1_ref)
        return 0

    lax.fori_loop(0, qi // 2, pair, 0)

    @pl.when(qi % 2 == 0)
    def _():
        update(qi, st0_ref, True)

    @pl.when(qi % 2 == 1)
    def _():
        both(qi, st1_ref, qi - 1, st0_ref)
        update(qi, st1_ref, True)

    ot = jnp.concatenate([acc_ref[h, 0:V_DIM, :] / acc_ref[h, V_DIM:V_DIM + 1, :] for h in heads], axis=0)
    o_ref[...] = ot.T.astype(BF16)


def _attn_prompt(q, k, vt, *, tq):
    b, s, hp = q.shape
    nblk, wv = vt.shape[1], vt.shape[2]
    return pl.pallas_call(
        functools.partial(_attn_kernel, tq=tq),
        grid=(b, s // tq),
        in_specs=[
            pl.BlockSpec((None, tq, hp), lambda i, j: (i, j, 0)),
            pl.BlockSpec((None, s, hp), lambda i, j: (i, 0, 0)),
            pl.BlockSpec((None, nblk, wv, tq), lambda i, j: (i, 0, 0, 0)),
        ],
        out_specs=pl.BlockSpec((None, tq, wv), lambda i, j: (i, j, 0)),
        out_shape=jax.ShapeDtypeStruct((b, s, wv), BF16),
        scratch_shapes=[pltpu.VMEM((A_HEADS, tq, tq), F32), pltpu.VMEM((A_HEADS, tq, tq), F32),
                        pltpu.VMEM((A_HEADS, 1, tq), F32), pltpu.VMEM((A_HEADS, V_DIM + BF16_ROWS, tq), F32)],
        name="attn_prompt",
        compiler_params=_params(("parallel", "parallel")),
    )(q, k, vt)


def _attn_sample_kernel(q_ref, clat_ref, ckr_ref, lat_ref, kr_ref, wkt_ref, wv_ref, o_ref, kn_ref, *, t):
    q = q_ref[...]
    qh = [q[:, h * HEAD_PAD:(h + 1) * HEAD_PAD] for h in range(A_HEADS)]
    qabs = jnp.concatenate(
        [_dot(qh[h], wkt_ref[h * HEAD_PAD:(h + 1) * HEAD_PAD, :]) for h in range(A_HEADS)], axis=0)
    qcat = jnp.concatenate([qabs.astype(BF16), jnp.concatenate(qh, axis=0)], axis=1)
    clat = clat_ref[...].astype(BF16)
    kv_lora = clat.shape[1]
    kc = jnp.concatenate([clat, ckr_ref[...]], axis=1)
    kn_ref[...] = jnp.zeros(kn_ref.shape, BF16)
    kn_ref[0:t, :] = jnp.concatenate([lat_ref[...], kr_ref[...]], axis=1).astype(BF16)
    kn = kn_ref[...]
    s1 = _dot_nt(qcat, kc)
    s2 = _dot_nt(qcat, kn)
    s2 = jnp.where(lax.broadcasted_iota(jnp.int32, s2.shape, 1) < t, s2, NEG_BIG)
    m = jnp.maximum(jnp.max(s1, axis=-1, keepdims=True), jnp.max(s2, axis=-1, keepdims=True))
    p1 = jnp.exp2(s1 - m)
    p2 = jnp.exp2(s2 - m)
    l = jnp.sum(p1, axis=-1, keepdims=True) + jnp.sum(p2, axis=-1, keepdims=True)
    olat = ((_dot(p1.astype(BF16), clat) + _dot(p2.astype(BF16), kn[:, 0:kv_lora])) / l).astype(BF16)
    low_half = lax.broadcasted_iota(jnp.int32, (t, LANES), 1) < V_DIM
    for pair in range(A_HEADS // 2):
        wpair = wv_ref[:, pair * LANES:(pair + 1) * LANES]
        lo = _dot(olat[(2 * pair) * t:(2 * pair + 1) * t, :], wpair)
        hi = _dot(olat[(2 * pair + 1) * t:(2 * pair + 2) * t, :], wpair)
        o_ref[:, pair * LANES:(pair + 1) * LANES] = jnp.where(low_half, lo, hi).astype(BF16)


def _attn_sample(q, cache_lat, cache_kr, lat, kr, wkt, wv, l):
    b, t, hp = q.shape
    past, kv_lora = cache_lat.shape[2], cache_lat.shape[3]
    wvw = wv.shape[2]
    per_b = lambda r, w: pl.BlockSpec((None, r, w), lambda i: (i, 0, 0))
    cache = lambda r, w: pl.BlockSpec((None, None, r, w), lambda i: (l, i, 0, 0))
    return pl.pallas_call(
        functools.partial(_attn_sample_kernel, t=t),
        grid=(b,),
        in_specs=[per_b(t, hp), cache(past, kv_lora), cache(past, LANES), per_b(t, kv_lora),
                  per_b(t, LANES), _layer(wkt, l), _layer(wv, l)],
        out_specs=per_b(t, wvw),
        out_shape=jax.ShapeDtypeStruct((b, t, wvw), BF16),
        scratch_shapes=[pltpu.VMEM((LANES, kv_lora + LANES), BF16)],
        name="attn_sample",
        compiler_params=_params(("parallel",)),
    )(q, cache_lat, cache_kr, lat, kr, wkt, wv)


MIX_OUT_WEIGHTS = ("mix_norm_pre", "mix_norm_post", "w_g", "w_br_a", "w_br_b", "w_br_c", "w_br_d", "w_o")


def _mix_out_kernel(xp_ref, ap_ref, yp_ref, xs_ref, as_ref, ys_ref, gpre_ref, gpost_ref, wg_ref, wba_ref, wbb_ref,
                    wbc_ref, wbd_ref, wo_ref, op_ref, os_ref, *, d, c_b, c_c, c_d, n_p):
    def merge(x_ref, a_ref, y_ref, o_ref):
        x = x_ref[...]
        n = _rms(x, gpre_ref[...]).astype(BF16)
        branches = (
            (a_ref[...], wba_ref),
            (y_ref[:, 0:c_b], wbb_ref),
            (y_ref[:, c_b:c_b + c_c], wbc_ref),
            (y_ref[:, c_b + c_c:c_b + c_c + c_d], wbd_ref),
        )
        merged = jnp.zeros(x.shape, F32)
        for k, (br, w_ref) in enumerate(branches):
            gate = jax.nn.sigmoid(_dot(n, wg_ref[:, k * d:(k + 1) * d]))
            merged = merged + gate * _dot(br, w_ref[...])
        out = _dot(merged.astype(BF16), wo_ref[...])
        o_ref[...] = x + _rms(out, gpost_ref[...])

    i = pl.program_id(0)

    @pl.when(i < n_p)
    def _():
        merge(xp_ref, ap_ref, yp_ref, op_ref)

    @pl.when(i == n_p)
    def _():
        merge(xs_ref, as_ref, ys_ref, os_ref)


def _mix_out(xp, ap, yp, xs, a_s, ys, w, l):
    n, d = xp.shape
    c_b, c_c, c_d = w["w_br_b"].shape[1], w["w_br_c"].shape[1], w["w_br_d"].shape[1]
    tm = _pick_tile(n, ROW_TILES)
    assert xs.shape[0] <= tm
    n_p = n // tm
    weights = [w[k] for k in MIX_OUT_WEIGHTS]
    specs_p, specs_s = _two_stream_specs(tm, n_p, (d, ap.shape[1], yp.shape[1]), (xs, a_s, ys))
    return pl.pallas_call(
        functools.partial(_mix_out_kernel, d=d, c_b=c_b, c_c=c_c, c_d=c_d, n_p=n_p),
        grid=(n_p + 1,),
        in_specs=specs_p + specs_s + [_layer(a_, l) for a_ in weights],
        out_specs=(specs_p[0], specs_s[0]),
        out_shape=(jax.ShapeDtypeStruct(xp.shape, F32), jax.ShapeDtypeStruct(xs.shape, F32)),
        name="mix_out",
        compiler_params=_params(("arbitrary",)),
    )(xp, ap, yp, xs, a_s, ys, *weights)


def _split_w_in_kernel(w_ref, w1_ref, wg_ref, *, o_kv, o_kr, o_d):
    x = w_ref[...]
    rows = x.shape[0]
    body = o_d - o_kr
    w1_ref[:, 0:o_kv] = x[:, 0:o_kv].astype(BF16)
    w1_ref[:, o_kv:o_kv + body] = x[:, o_kr:o_d].astype(BF16)
    w1_ref[:, o_kv + body:o_kv + body + LANES] = jnp.concatenate(
        [x[:, o_kv:o_kr], jnp.zeros((rows, LANES - (o_kr - o_kv)), F32)], axis=1).astype(BF16)
    wg_ref[...] = x[:, o_d:].astype(BF16)


def _split_w_in(w_in, o_kv, o_kr, o_d):
    depth, d, d_in = w_in.shape
    w1 = o_d - (o_kr - o_kv) + LANES
    tr = _pick_tile(d, (256, 128))
    return pl.pallas_call(
        functools.partial(_split_w_in_kernel, o_kv=o_kv, o_kr=o_kr, o_d=o_d),
        grid=(depth, d // tr),
        in_specs=[pl.BlockSpec((None, tr, d_in), lambda l, i: (l, i, 0))],
        out_specs=(pl.BlockSpec((None, tr, w1), lambda l, i: (l, i, 0)),
                   pl.BlockSpec((None, tr, d_in - o_d), lambda l, i: (l, i, 0))),
        out_shape=(jax.ShapeDtypeStruct((depth, d, w1), BF16), jax.ShapeDtypeStruct((depth, d, d_in - o_d), BF16)),
        name="split_w_in",
        compiler_params=_params(("parallel", "parallel")),
    )(w_in)


def _rope_tabs(pos):
    half = ROPE_DIM // 2
    inv = jnp.exp(-math.log(ROPE_THETA) * jnp.arange(half, dtype=F32) / half)
    ang = pos.astype(F32)[:, None] * inv[None, :]
    cos, sin = jnp.cos(ang), jnp.sin(ang)
    n = pos.shape[0]
    z = lambda w: jnp.zeros((n, w), F32)
    ctab = jnp.concatenate([cos, cos, jnp.ones((n, NOPE_DIM), F32), z(LANES - ROPE_DIM - NOPE_DIM)], axis=1)
    sneg = jnp.concatenate([-sin, z(LANES - half)], axis=1)
    spos = jnp.concatenate([z(half), sin, z(LANES - 2 * half)], axis=1)
    return ctab, sneg, spos


def _pad_rows(w, rows):
    return jnp.pad(w, ((0, 0), (0, rows - w.shape[1]), (0, 0)))


def kernel(x_prompt, x_sample, cache_kv_latent, cache_k_rope, state_conv_b, state_conv_d, ffn1_norm_pre, ffn1_norm_post, ffn1_w_gu, ffn1_w_down, mix_norm_pre, mix_norm_post, w_in, q_norm, w_uq, kv_norm, w_ukv, conv_b_w, conv_b_bias, conv_b_ln_g, conv_b_ln_b, gmlp_vn_g, gmlp_vn_b, gmlp_w_s, gmlp_b_s, conv_d_w, w_br_a, w_br_b, w_br_c, w_br_d, w_o, ffn2_norm_pre, ffn2_norm_post, ffn2_w_gu, ffn2_w_down):
    b, s, d = x_prompt.shape
    bs, t, _ = x_sample.shape
    depth = w_in.shape[0]
    past = cache_kv_latent.shape[2]
    q_lora, kv_lora = q_norm.shape[1], kv_norm.shape[1]
    c_b, c_c, c_d = conv_b_bias.shape[1], gmlp_vn_g.shape[1], conv_d_w.shape[2]
    kb_w, kd_w = conv_b_w.shape[1], conv_d_w.shape[1]
    assert kb_w - 1 <= CONV_B_PAD and kd_w - 1 <= CONV_D_PAD
    assert s % GMLP_CHUNK == 0 and t <= GMLP_CHUNK and t % 16 == 0 and past % CHUNK == 0 and t <= CHUNK
    assert gmlp_w_s.shape[1] == C_GROUPS and w_ukv.shape[2] == A_HEADS * (NOPE_DIM + V_DIM)

    o_q = q_lora
    o_kv = o_q + kv_lora
    o_kr = o_kv + ROPE_DIM
    o_b = o_kr + 2 * c_b
    o_c = o_b + 2 * c_c
    o_d = o_c + 3 * c_d

    vec = lambda p: p[:, None, :]
    uq = w_uq.reshape(depth, q_lora, A_HEADS, NOPE_DIM + ROPE_DIM)
    ukv = w_ukv.reshape(depth, kv_lora, A_HEADS, NOPE_DIM + V_DIM)
    head_zeros = lambda rows, width: jnp.zeros((depth, rows, A_HEADS, width), F32)
    wk_pad = jnp.concatenate(
        [head_zeros(kv_lora, ROPE_DIM), ukv[..., :NOPE_DIM], head_zeros(kv_lora, HEAD_PAD - NOPE_DIM - ROPE_DIM)],
        axis=3).reshape(depth, kv_lora, A_HEADS * HEAD_PAD).astype(BF16)
    w_in1, w_g = _split_w_in(w_in, o_kv, o_kr, o_d)
    w = dict(
        mix_norm_pre=vec(mix_norm_pre), mix_norm_post=vec(mix_norm_post),
        w_in1=w_in1, w_g=w_g,
        q_norm=vec(q_norm),
        wq_pad=jnp.concatenate(
            [uq[..., NOPE_DIM:], uq[..., :NOPE_DIM], head_zeros(q_lora, HEAD_PAD - NOPE_DIM - ROPE_DIM)],
            axis=3).reshape(depth, q_lora, A_HEADS * HEAD_PAD).astype(BF16),
        kv_norm=vec(kv_norm), wk_pad=wk_pad,
        wv=ukv[..., NOPE_DIM:].reshape(depth, kv_lora, A_HEADS * V_DIM).astype(BF16),
        conv_b_w=_pad_rows(conv_b_w, CONV_B_PAD), conv_b_bias=vec(conv_b_bias),
        conv_b_ln_g=vec(conv_b_ln_g), conv_b_ln_b=vec(conv_b_ln_b),
        gmlp_vn_g=vec(gmlp_vn_g), gmlp_vn_b=vec(gmlp_vn_b), gmlp_w_s=gmlp_w_s,
        bs_full=jnp.repeat(jnp.swapaxes(gmlp_b_s, 1, 2), c_c // C_GROUPS, axis=2),
        conv_d_w=_pad_rows(conv_d_w, CONV_D_PAD), kb_w=kb_w, kd_w=kd_w,
        w_br_a=w_br_a.astype(BF16), w_br_b=w_br_b.astype(BF16), w_br_c=w_br_c.astype(BF16),
        w_br_d=w_br_d.astype(BF16), w_o=w_o.astype(BF16))
    wk_t = jnp.swapaxes(wk_pad, 1, 2)
    f1 = (vec(ffn1_norm_pre), vec(ffn1_norm_post), ffn1_w_gu.astype(BF16), ffn1_w_down.astype(BF16))
    f2 = (vec(ffn2_norm_pre), vec(ffn2_norm_post), ffn2_w_gu.astype(BF16), ffn2_w_down.astype(BF16))

    tabs_p = _rope_tabs(jnp.arange(s))
    tabs_s = _rope_tabs(past + jnp.arange(t))
    zero_b = jnp.zeros((b, CONV_B_PAD, c_b), F32)
    zero_d = jnp.zeros((b, CONV_D_PAD, c_d), F32)
    tail_b = CONV_B_PAD - (kb_w - 1)
    tail_d = CONV_D_PAD - (kd_w - 1)
    st_b = jnp.pad(state_conv_b, ((0, 0), (0, 0), (tail_b, 0), (0, 0)))
    st_d = jnp.pad(state_conv_d, ((0, 0), (0, 0), (tail_d, 0), (0, 0)))
    ckr_pad = jnp.pad(cache_k_rope, ((0, 0), (0, 0), (0, 0), (0, LANES - ROPE_DIM))).astype(BF16)
    tq = _pick_tile(s, (256, 128))

    xp = x_prompt.reshape(b * s, d)
    xs = x_sample.reshape(bs * t, d)
    stacked = None
    outs = [[] for _ in range(7)]
    for l in range(depth):
        xp, xs = _ffn(xp, xs, *f1, l)

        qp, kp, vtp, yp, lat_all, kr_all, btp, dtp = _mix_in(
            xp, tabs_p, zero_b, zero_d, None, w, l, b=b, s=s, tq=tq, prompt=True, stacked=stacked)
        stacked = (lat_all, kr_all)
        ap = _attn_prompt(qp.reshape(b, s, -1), kp.reshape(b, s, -1), vtp, tq=tq)

        qs, ys, lats, krs, bts, dts, vgs = _mix_in(
            xs, tabs_s, st_b, st_d, l, w, l, b=bs, s=t, tq=t, prompt=False)
        lats, krs, vgs = (v.reshape(bs, t, -1) for v in (lats, krs, vgs))
        a_s = _attn_sample(qs.reshape(bs, t, -1), cache_kv_latent, ckr_pad, lats, krs, wk_t, w["wv"], l)

        xp, xs = _mix_out(xp, ap.reshape(b * s, -1), yp, xs, a_s.reshape(bs * t, -1), ys, w, l)
        xp, xs = _ffn(xp, xs, *f2, l)

        for lst, val in zip(outs, (btp[:, tail_b:], dtp[:, tail_d:], lats, krs[..., :ROPE_DIM],
                                   bts[:, tail_b:], vgs, dts[:, tail_d:])):
            lst.append(val)
    cb_p, cd_p, lat_s, kr_s, cb_s, vc_s, cd_s = (jnp.stack(o) for o in outs)
    return (xp.reshape(b, s, d), xs.reshape(bs, t, d), stacked[0], stacked[1], cb_p, cd_p,
            lat_s, kr_s, cb_s, vc_s, cd_s)
```

```python
import functools
import math

import jax
import jax.numpy as jnp
from jax import lax
from jax.experimental import pallas as pl
from jax.experimental.pallas import tpu as pltpu

F32 = jnp.float32
BF16 = jnp.bfloat16

CHUNK = 64
A_HEADS = 8
NOPE_DIM = 64
ROPE_DIM = 32
V_DIM = 64
ROPE_THETA = 10000.0
SM_SCALE = (NOPE_DIM + ROPE_DIM) ** -0.5
Q_SCALE = SM_SCALE * math.log2(math.e)
C_GROUPS = 4
GMLP_CHUNK = 128
N_BRANCH = 4
EPS = 1e-6

LANES = 128
SUBLANES = 8
BF16_ROWS = 16
HEAD_PAD = 128
CONV_B_PAD = 32
CONV_D_PAD = 8
VMEM_LIMIT_BYTES = 56 * 1024 * 1024
NEG_BIG = -1e30
ROW_TILES = (1024, 512, 256, 128, 64, 32, 16, 8)


def _pick_tile(n, candidates):
    for c in candidates:
        if n % c == 0:
            return c
    return n


def _layer(arr, l):
    nd = arr.ndim - 1
    return pl.BlockSpec((None,) + arr.shape[1:], lambda *_: (l,) + (0,) * nd, pipeline_mode=pl.Buffered(1))


def _whole(arr):
    nd = arr.ndim
    return pl.BlockSpec(arr.shape, lambda *_: (0,) * nd, pipeline_mode=pl.Buffered(1))


def _rms(x, g):
    ms = jnp.mean(x * x, axis=-1, keepdims=True)
    return x * lax.rsqrt(ms + EPS) * g


def _layernorm(x, g, b):
    mu = jnp.mean(x, axis=-1, keepdims=True)
    xc = x - mu
    var = jnp.mean(xc * xc, axis=-1, keepdims=True)
    return xc * lax.rsqrt(var + EPS) * g + b


def _dot(a, b):
    return jnp.dot(a, b, preferred_element_type=F32)


def _dot_nt(a, b):
    return lax.dot_general(a, b, (((1,), (1,)), ((), ())), preferred_element_type=F32)


def _params(sem):
    return pltpu.CompilerParams(dimension_semantics=sem, vmem_limit_bytes=VMEM_LIMIT_BYTES)


def _ffn_kernel(xp_ref, xs_ref, gpre_ref, gpost_ref, wgu_ref, wd_ref, op_ref, os_ref, act_ref, *, d_ff, tf, n_p):
    def ffn(x_ref, o_ref):
        rows = x_ref.shape[0]
        x = x_ref[...]
        n = _rms(x, gpre_ref[...]).astype(BF16)
        for c in range(d_ff // tf):
            gate = _dot(n, wgu_ref[:, c * tf:(c + 1) * tf])
            up = _dot(n, wgu_ref[:, d_ff + c * tf:d_ff + (c + 1) * tf])
            act_ref[0:rows, c * tf:(c + 1) * tf] = (gate * jax.nn.sigmoid(gate) * up).astype(BF16)
        y = _dot(act_ref[0:rows, :], wd_ref[...])
        o_ref[...] = x + 0.5 * _rms(y, gpost_ref[...])

    i = pl.program_id(0)

    @pl.when(i < n_p)
    def _():
        ffn(xp_ref, op_ref)

    @pl.when(i == n_p)
    def _():
        ffn(xs_ref, os_ref)


def _two_stream_specs(tm, n_p, widths_p, arrays_s):
    prompt = [pl.BlockSpec((tm, wd), lambda i: (jnp.minimum(i, n_p - 1), 0)) for wd in widths_p]
    sample = [pl.BlockSpec(a.shape, lambda i: (0, 0)) for a in arrays_s]
    return prompt, sample


def _ffn(xp, xs, gpre, gpost, wgu, wd, l):
    n, d = xp.shape
    d_ff = wd.shape[1]
    tm = _pick_tile(n, ROW_TILES)
    assert xs.shape[0] <= tm
    n_p = n // tm
    tf = _pick_tile(d_ff, (256, 128))
    weights = (gpre, gpost, wgu, wd)
    (spec_p,), (spec_s,) = _two_stream_specs(tm, n_p, (d,), (xs,))
    return pl.pallas_call(
        functools.partial(_ffn_kernel, d_ff=d_ff, tf=tf, n_p=n_p),
        grid=(n_p + 1,),
        in_specs=[spec_p, spec_s] + [_layer(w, l) for w in weights],
        out_specs=(spec_p, spec_s),
        out_shape=(jax.ShapeDtypeStruct(xp.shape, F32), jax.ShapeDtypeStruct(xs.shape, F32)),
        scratch_shapes=[pltpu.VMEM((tm, d_ff), BF16)],
        name="ffn",
        compiler_params=_params(("arbitrary",)),
    )(xp, xs, *weights)


def _rope128(a, c, sneg, spos):
    return (a * c + pltpu.roll(a, LANES - ROPE_DIM // 2, 1) * sneg
            + pltpu.roll(a, ROPE_DIM // 2, 1) * spos)


MIX_IN_WEIGHTS = ("mix_norm_pre", "w_in1", "q_norm", "wq_pad", "kv_norm", "wk_pad", "wv", "conv_b_w",
                  "conv_b_bias", "conv_b_ln_g", "conv_b_ln_b", "gmlp_vn_g", "gmlp_vn_b", "gmlp_w_s", "bs_full",
                  "conv_d_w")
N_MIX_IN_INPUTS = 6 + len(MIX_IN_WEIGHTS)


def _mix_in_kernel(*refs, prompt, n_alias, n_tiles, nt, tq, q_lora, kv_lora, c_b, c_c, c_d, kb_w, kd_w, rb):
    (x_ref, c_ref, sneg_ref, spos_ref, stb_ref, std_ref,
     gpre_ref, win_ref, qn_ref, wq_ref, kvn_ref, wk_ref, wv_ref,
     cbw_ref, cbb_ref, lng_ref, lnb_ref, vng_ref, vnb_ref, ws_ref, bs_ref, cdw_ref) = refs[:N_MIX_IN_INPUTS]
    rest = refs[N_MIX_IN_INPUTS + n_alias:]
    if prompt:
        q_ref, k_ref, vt_ref, y_ref, lat_ref, kr_ref, bt_ref, dt_ref, ha_ref, hb_ref, xpb_ref, xsh_ref, xpd_ref = rest
    else:
        q_ref, y_ref, lat_ref, kr_ref, bt_ref, dt_ref, vg_ref, ha_ref, hb_ref, xpb_ref, xsh_ref, xpd_ref = rest
    i = pl.program_id(0)

    o_kv = q_lora
    o_b = o_kv + kv_lora
    o_c = o_b + 2 * c_b
    o_d = o_c + 2 * c_c
    o_kr = o_d + 3 * c_d
    col_groups = (0, o_kv, o_b, o_c, o_d, o_kr, o_kr + LANES)

    def project(h_ref):
        n = _rms(x_ref[...], gpre_ref[...]).astype(BF16)
        for c0, c1 in zip(col_groups[:-1], col_groups[1:]):
            h_ref[:, c0:c1] = _dot(n, win_ref[:, c0:c1])
            yield

    def consume(h_ref):
        first = (i - 1) % nt == 0
        ctab, sneg, spos = c_ref[...], sneg_ref[...], spos_ref[...]

        qlat = _rms(h_ref[:, 0:o_kv], qn_ref[...]).astype(BF16)
        qa = _dot(qlat, wq_ref[...])
        for h in range(A_HEADS):
            qh = _rope128(qa[:, h * HEAD_PAD:(h + 1) * HEAD_PAD], ctab, sneg, spos) * Q_SCALE
            q_ref[:, h * HEAD_PAD:(h + 1) * HEAD_PAD] = qh.astype(BF16)

        yield
        lat = _rms(h_ref[:, o_kv:o_b], kvn_ref[...])
        lat_ref[...] = lat
        latb = lat.astype(BF16)
        kr = _rope128(h_ref[:, o_kr:o_kr + LANES], ctab, sneg, spos)
        if prompt:
            kr_ref[...] = kr[:, 0:ROPE_DIM]
            kn = _dot(latb, wk_ref[...])
            for h in range(A_HEADS):
                k_ref[:, h * HEAD_PAD:(h + 1) * HEAD_PAD] = (
                    kn[:, h * HEAD_PAD:(h + 1) * HEAD_PAD] + kr).astype(BF16)
            vt_ref[...] = _dot(latb, wv_ref[...]).T.astype(BF16)
        else:
            kr_ref[...] = kr

        yield
        xb = h_ref[:, o_b:o_b + c_b] * jax.nn.sigmoid(h_ref[:, o_b + c_b:o_c])

        @pl.when(first)
        def _():
            xpb_ref[0:CONV_B_PAD, :] = stb_ref[...]
            xpd_ref[0:CONV_D_PAD, :] = std_ref[...]

        @pl.when(jnp.logical_not(first))
        def _():
            xpb_ref[0:CONV_B_PAD, :] = xpb_ref[tq:tq + CONV_B_PAD, :]
            xpd_ref[0:CONV_D_PAD, :] = xpd_ref[tq:tq + CONV_D_PAD, :]

        xpb_ref[CONV_B_PAD:CONV_B_PAD + tq, :] = xb
        bt_ref[...] = xpb_ref[tq:tq + CONV_B_PAD, :]
        n_sh = tq + CONV_B_PAD - SUBLANES
        for r in range(1, SUBLANES):
            xsh_ref[r, 0:n_sh, :] = xpb_ref[r:r + n_sh, :]

        off_b = CONV_B_PAD - (kb_w - 1)
        for blk in range(tq // rb):
            yield
            acc = jnp.broadcast_to(cbb_ref[...], (rb, c_b))
            for kk in range(kb_w):
                a8, r = divmod(off_b + kk, SUBLANES)
                row0 = blk * rb + a8 * SUBLANES
                src = xpb_ref[row0:row0 + rb, :] if r == 0 else xsh_ref[r, row0:row0 + rb, :]
                acc = acc + src * cbw_ref[kk:kk + 1, :]
            yb = _layernorm(acc, lng_ref[...], lnb_ref[...])
            y_ref[blk * rb:(blk + 1) * rb, 0:c_b] = (yb * jax.nn.sigmoid(yb)).astype(BF16)

        yield
        vg = _layernorm(h_ref[:, o_c + c_c:o_d], vng_ref[...], vnb_ref[...])
        if not prompt:
            vg_ref[...] = vg
        vgb = vg.astype(BF16)
        ck = min(tq, GMLP_CHUNK)
        ri = lax.broadcasted_iota(jnp.int32, (ck, ck), 0) // CHUNK
        ci = lax.broadcasted_iota(jnp.int32, (ck, ck), 1) // CHUNK
        lane_grp = lax.broadcasted_iota(jnp.int32, (ck, c_c), 1) // (c_c // C_GROUPS)
        wmix = [jnp.where(ci <= ri, ws_ref[g, 0:ck, 0:ck], 0.0).astype(BF16) for g in range(C_GROUPS)]
        for c in range(tq // ck):
            vc = vgb[c * ck:(c + 1) * ck, :]
            mix = bs_ref[0:ck, :]
            for g in range(C_GROUPS):
                mix = mix + jnp.where(lane_grp == g, _dot(wmix[g], vc), 0.0)
            y_ref[c * ck:(c + 1) * ck, c_b:c_b + c_c] = (
                h_ref[c * ck:(c + 1) * ck, o_c:o_c + c_c] * mix).astype(BF16)

        yield
        xd = h_ref[:, o_d + c_d:o_d + 2 * c_d] * h_ref[:, o_d + 2 * c_d:o_kr]
        xpd_ref[CONV_D_PAD:CONV_D_PAD + tq, :] = xd
        dt_ref[...] = xpd_ref[tq:tq + CONV_D_PAD, :]
        off_d = CONV_D_PAD - (kd_w - 1)
        conv = xd * cdw_ref[kd_w - 1:kd_w, :]
        for kk in range(kd_w - 1):
            conv = conv + xpd_ref[off_d + kk:off_d + kk + tq, :] * cdw_ref[kk:kk + 1, :]
        y_ref[:, c_b + c_c:c_b + c_c + c_d] = (h_ref[:, o_d:o_d + c_d] * conv).astype(BF16)

    bufs = (ha_ref, hb_ref)

    def run(*stages):
        live = list(stages)
        while live:
            for g in list(live):
                if next(g, StopIteration) is StopIteration:
                    live.remove(g)

    @pl.when(i == 0)
    def _():
        run(project(bufs[0]))

    for par in (0, 1):
        @pl.when((i >= 1) & (i < n_tiles) & (i % 2 == par))
        def _(par=par):
            run(project(bufs[par]), consume(bufs[1 - par]))

    @pl.when(i == n_tiles)
    def _():
        run(consume(bufs[(n_tiles - 1) % 2]))


def _mix_in(x, tabs, stb, std, st_layer, w, l, *, b, s, tq, prompt, stacked=None):
    d = x.shape[1]
    depth = w["w_in1"].shape[0]
    q_lora, kv_lora = w["q_norm"].shape[2], w["kv_norm"].shape[2]
    c_b, c_c, c_d = w["conv_b_bias"].shape[2], w["gmlp_vn_g"].shape[2], w["conv_d_w"].shape[2]
    hp = A_HEADS * HEAD_PAD
    rb = min(tq, 64)
    nt = s // tq
    n_tiles = b * nt
    done = lambda i: jnp.maximum(i - 1, 0)
    row_in = pl.BlockSpec((tq, d), lambda i: (jnp.minimum(i, n_tiles - 1), 0))
    row = lambda wd: pl.BlockSpec((tq, wd), lambda i: (done(i), 0))
    per_b = lambda r, wd: pl.BlockSpec((None, r, wd), lambda i: (done(i) // nt, 0, 0))
    if st_layer is None:
        st_spec = per_b
    else:
        st_spec = lambda r, wd: pl.BlockSpec((None, None, r, wd), lambda i: (st_layer, done(i) // nt, 0, 0))
    tab = pl.BlockSpec((tq, LANES), lambda i: (done(i) % nt, 0))
    weights = [w[k] for k in MIX_IN_WEIGHTS]
    sds = jax.ShapeDtypeStruct
    small = [
        (sds((b, CONV_B_PAD, c_b), F32), per_b(CONV_B_PAD, c_b)),
        (sds((b, CONV_D_PAD, c_d), F32), per_b(CONV_D_PAD, c_d)),
    ]
    outs = [(sds((b * s, hp), BF16), row(hp))]
    aliases = {}
    alias_in, alias_specs = [], []
    if prompt:
        vt_spec = pl.BlockSpec((None, None, A_HEADS * V_DIM, tq), lambda i: (done(i) // nt, done(i) % nt, 0, 0))
        stk = lambda wd: pl.BlockSpec((None, None, tq, wd), lambda i: (l, done(i) // nt, done(i) % nt, 0))
        outs += [(sds((b * s, hp), BF16), row(hp)),
                 (sds((b, nt, A_HEADS * V_DIM, tq), BF16), vt_spec),
                 (sds((b * s, c_b + c_c + c_d), BF16), row(c_b + c_c + c_d)),
                 (sds((depth, b, s, kv_lora), F32), stk(kv_lora)),
                 (sds((depth, b, s, ROPE_DIM), F32), stk(ROPE_DIM))]
        outs += small
        if stacked is not None:
            alias_in = list(stacked)
            alias_specs = [pl.BlockSpec(memory_space=pl.ANY)] * 2
            aliases = {N_MIX_IN_INPUTS: 4, N_MIX_IN_INPUTS + 1: 5}
    else:
        outs += [(sds((b * s, c_b + c_c + c_d), BF16), row(c_b + c_c + c_d)),
                 (sds((b * s, kv_lora), F32), row(kv_lora)),
                 (sds((b * s, LANES), F32), row(LANES))]
        outs += small + [(sds((b * s, c_c), F32), row(c_c))]
    out_shape, out_specs = zip(*outs)
    w1 = w["w_in1"].shape[2]
    return pl.pallas_call(
        functools.partial(_mix_in_kernel, prompt=prompt, n_alias=len(alias_in), n_tiles=n_tiles, nt=nt, tq=tq,
                          q_lora=q_lora, kv_lora=kv_lora, c_b=c_b, c_c=c_c, c_d=c_d, kb_w=w["kb_w"],
                          kd_w=w["kd_w"], rb=rb),
        grid=(n_tiles + 1,),
        in_specs=[row_in, tab, tab, tab, st_spec(CONV_B_PAD, c_b), st_spec(CONV_D_PAD, c_d)]
        + [_layer(a, l) for a in weights] + alias_specs,
        out_specs=out_specs,
        out_shape=out_shape,
        input_output_aliases=aliases,
        scratch_shapes=[pltpu.VMEM((tq, w1), F32), pltpu.VMEM((tq, w1), F32),
                        pltpu.VMEM((CONV_B_PAD + tq, c_b), F32),
                        pltpu.VMEM((SUBLANES, CONV_B_PAD + tq, c_b), F32),
                        pltpu.VMEM((CONV_D_PAD + tq, c_d), F32)],
        name="mix_in",
        compiler_params=_params(("arbitrary",)),
    )(x, *tabs, stb, std, *weights, *alias_in)


def _attn_kernel(q_ref, k_ref, vt_ref, o_ref, st0_ref, st1_ref, m_ref, acc_ref, *, tq):
    qi = pl.program_id(1)
    kc = lax.broadcasted_iota(jnp.int32, (tq, tq), 0) // CHUNK
    qc = lax.broadcasted_iota(jnp.int32, (tq, tq), 1) // CHUNK
    diag_ok = kc <= qc
    heads = tuple(range(A_HEADS))
    ones = jnp.ones((BF16_ROWS, tq), BF16)

    def scores(j, st_ref):
        k0 = pl.multiple_of(j * tq, tq)
        for h in heads:
            st_ref[h] = _dot_nt(k_ref[pl.ds(k0, tq), h * HEAD_PAD:(h + 1) * HEAD_PAD],
                                q_ref[:, h * HEAD_PAD:(h + 1) * HEAD_PAD])

    def update_head(j, st_ref, h, masked):
        st = st_ref[h]
        if masked:
            st = jnp.where(diag_ok, st, NEG_BIG)
        m = m_ref[h]
        m_new = jnp.maximum(m, jnp.max(st, axis=0, keepdims=True))
        alpha = jnp.exp2(m - m_new)
        pb = jnp.exp2((st - m_new).astype(BF16))
        vt1 = jnp.concatenate([vt_ref[j, h * V_DIM:(h + 1) * V_DIM, :], ones], axis=0)
        m_ref[h] = m_new
        acc_ref[h] = alpha * acc_ref[h] + _dot(vt1, pb)

    def update(j, st_ref, masked):
        for h in heads:
            update_head(j, st_ref, h, masked)

    m_ref[...] = jnp.full(m_ref.shape, NEG_BIG, F32)
    acc_ref[...] = jnp.zeros(acc_ref.shape, F32)
    scores(0, st0_ref)

    def both(j_next, next_ref, j, cur_ref):
        k0 = pl.multiple_of(j_next * tq, tq)
        for h in heads:
            next_ref[h] = _dot_nt(k_ref[pl.ds(k0, tq), h * HEAD_PAD:(h + 1) * HEAD_PAD],
                                  q_ref[:, h * HEAD_PAD:(h + 1) * HEAD_PAD])
            update_head(j, cur_ref, h, False)

    def pair(p, _):
        j = 2 * p
        both(j + 1, st1_ref, j, st0_ref)
        both(j + 2, st0_ref, j + 1, st1_ref)
        return 0

    lax.fori_loop(0, qi // 2, pair, 0)

    @pl.when(qi % 2 == 0)
    def _():
        update(qi, st0_ref, True)

    @pl.when(qi % 2 == 1)
    def _():
        both(qi, st1_ref, qi - 1, st0_ref)
        update(qi, st1_ref, True)

    ot = jnp.concatenate([acc_ref[h, 0:V_DIM, :] / acc_ref[h, V_DIM:V_DIM + 1, :] for h in heads], axis=0)
    o_ref[...] = ot.T.astype(BF16)


def _attn_prompt(q, k, vt, *, tq):
    b, s, hp = q.shape
    nblk, wv = vt.shape[1], vt.shape[2]
    return pl.pallas_call(
        functools.partial(_attn_kernel, tq=tq),
        grid=(b, s // tq),
        in_specs=[
            pl.BlockSpec((None, tq, hp), lambda i, j: (i, j, 0)),
            pl.BlockSpec((None, s, hp), lambda i, j: (i, 0, 0)),
            pl.BlockSpec((None, nblk, wv, tq), lambda i, j: (i, 0, 0, 0)),
        ],
        out_specs=pl.BlockSpec((None, tq, wv), lambda i, j: (i, j, 0)),
        out_shape=jax.ShapeDtypeStruct((b, s, wv), BF16),
        scratch_shapes=[pltpu.VMEM((A_HEADS, tq, tq), F32), pltpu.VMEM((A_HEADS, tq, tq), F32),
                        pltpu.VMEM((A_HEADS, 1, tq), F32), pltpu.VMEM((A_HEADS, V_DIM + BF16_ROWS, tq), F32)],
        name="attn_prompt",
        compiler_params=_params(("parallel", "parallel")),
    )(q, k, vt)


def _attn_sample_kernel(q_ref, clat_ref, ckr_ref, lat_ref, kr_ref, wkt_ref, wv_ref, o_ref, kn_ref, *, t):
    q = q_ref[...]
    qh = [q[:, h * HEAD_PAD:(h + 1) * HEAD_PAD] for h in range(A_HEADS)]
    qabs = jnp.concatenate(
        [_dot(qh[h], wkt_ref[h * HEAD_PAD:(h + 1) * HEAD_PAD, :]) for h in range(A_HEADS)], axis=0)
    qcat = jnp.concatenate([qabs.astype(BF16), jnp.concatenate(qh, axis=0)], axis=1)
    clat = clat_ref[...].astype(BF16)
    kv_lora = clat.shape[1]
    kc = jnp.concatenate([clat, ckr_ref[...]], axis=1)
    kn_ref[...] = jnp.zeros(kn_ref.shape, BF16)
    kn_ref[0:t, :] = jnp.concatenate([lat_ref[...], kr_ref[...]], axis=1).astype(BF16)
    kn = kn_ref[...]
    s1 = _dot_nt(qcat, kc)
    s2 = _dot_nt(qcat, kn)
    s2 = jnp.where(lax.broadcasted_iota(jnp.int32, s2.shape, 1) < t, s2, NEG_BIG)
    m = jnp.maximum(jnp.max(s1, axis=-1, keepdims=True), jnp.max(s2, axis=-1, keepdims=True))
    p1 = jnp.exp2(s1 - m)
    p2 = jnp.exp2(s2 - m)
    l = jnp.sum(p1, axis=-1, keepdims=True) + jnp.sum(p2, axis=-1, keepdims=True)
    olat = ((_dot(p1.astype(BF16), clat) + _dot(p2.astype(BF16), kn[:, 0:kv_lora])) / l).astype(BF16)
    low_half = lax.broadcasted_iota(jnp.int32, (t, LANES), 1) < V_DIM
    for pair in range(A_HEADS // 2):
        wpair = wv_ref[:, pair * LANES:(pair + 1) * LANES]
        lo = _dot(olat[(2 * pair) * t:(2 * pair + 1) * t, :], wpair)
        hi = _dot(olat[(2 * pair + 1) * t:(2 * pair + 2) * t, :], wpair)
        o_ref[:, pair * LANES:(pair + 1) * LANES] = jnp.where(low_half, lo, hi).astype(BF16)


def _attn_sample(q, cache_lat, cache_kr, lat, kr, wkt, wv, l):
    b, t, hp = q.shape
    past, kv_lora = cache_lat.shape[2], cache_lat.shape[3]
    wvw = wv.shape[2]
    per_b = lambda r, w: pl.BlockSpec((None, r, w), lambda i: (i, 0, 0))
    cache = lambda r, w: pl.BlockSpec((None, None, r, w), lambda i: (l, i, 0, 0))
    return pl.pallas_call(
        functools.partial(_attn_sample_kernel, t=t),
        grid=(b,),
        in_specs=[per_b(t, hp), cache(past, kv_lora), cache(past, LANES), per_b(t, kv_lora),
                  per_b(t, LANES), _layer(wkt, l), _layer(wv, l)],
        out_specs=per_b(t, wvw),
        out_shape=jax.ShapeDtypeStruct((b, t, wvw), BF16),
        scratch_shapes=[pltpu.VMEM((LANES, kv_lora + LANES), BF16)],
        name="attn_sample",
        compiler_params=_params(("parallel",)),
    )(q, cache_lat, cache_kr, lat, kr, wkt, wv)


MIX_OUT_WEIGHTS = ("mix_norm_pre", "mix_norm_post", "w_g", "w_br_a", "w_br_b", "w_br_c", "w_br_d", "w_o")


def _mix_out_kernel(xp_ref, ap_ref, yp_ref, xs_ref, as_ref, ys_ref, gpre_ref, gpost_ref, wg_ref, wba_ref, wbb_ref,
                    wbc_ref, wbd_ref, wo_ref, op_ref, os_ref, *, d, c_b, c_c, c_d, n_p):
    def merge(x_ref, a_ref, y_ref, o_ref):
        x = x_ref[...]
        n = _rms(x, gpre_ref[...]).astype(BF16)
        branches = (
            (a_ref[...], wba_ref),
            (y_ref[:, 0:c_b], wbb_ref),
            (y_ref[:, c_b:c_b + c_c], wbc_ref),
            (y_ref[:, c_b + c_c:c_b + c_c + c_d], wbd_ref),
        )
        merged = jnp.zeros(x.shape, F32)
        for k, (br, w_ref) in enumerate(branches):
            gate = jax.nn.sigmoid(_dot(n, wg_ref[:, k * d:(k + 1) * d]))
            merged = merged + gate * _dot(br, w_ref[...])
        out = _dot(merged.astype(BF16), wo_ref[...])
        o_ref[...] = x + _rms(out, gpost_ref[...])

    i = pl.program_id(0)

    @pl.when(i < n_p)
    def _():
        merge(xp_ref, ap_ref, yp_ref, op_ref)

    @pl.when(i == n_p)
    def _():
        merge(xs_ref, as_ref, ys_ref, os_ref)


def _mix_out(xp, ap, yp, xs, a_s, ys, w, l):
    n, d = xp.shape
    c_b, c_c, c_d = w["w_br_b"].shape[1], w["w_br_c"].shape[1], w["w_br_d"].shape[1]
    tm = _pick_tile(n, ROW_TILES)
    assert xs.shape[0] <= tm
    n_p = n // tm
    weights = [w[k] for k in MIX_OUT_WEIGHTS]
    specs_p, specs_s = _two_stream_specs(tm, n_p, (d, ap.shape[1], yp.shape[1]), (xs, a_s, ys))
    return pl.pallas_call(
        functools.partial(_mix_out_kernel, d=d, c_b=c_b, c_c=c_c, c_d=c_d, n_p=n_p),
        grid=(n_p + 1,),
        in_specs=specs_p + specs_s + [_layer(a_, l) for a_ in weights],
        out_specs=(specs_p[0], specs_s[0]),
        out_shape=(jax.ShapeDtypeStruct(xp.shape, F32), jax.ShapeDtypeStruct(xs.shape, F32)),
        name="mix_out",
        compiler_params=_params(("arbitrary",)),
    )(xp, ap, yp, xs, a_s, ys, *weights)


def _split_w_in_kernel(wt_ref, w1_ref, wg_ref, *, o_kv, o_kr, o_d):
    cols = wt_ref.shape[1]
    body = o_d - o_kr
    w1_ref[:, 0:o_kv] = wt_ref[0:o_kv, :].T.astype(BF16)
    w1_ref[:, o_kv:o_kv + body] = wt_ref[o_kr:o_d, :].T.astype(BF16)
    kr_rows = jnp.concatenate([wt_ref[o_kv:o_kr, :], jnp.zeros((LANES - (o_kr - o_kv), cols), F32)], axis=0)
    w1_ref[:, o_kv + body:o_kv + body + LANES] = kr_rows.T.astype(BF16)
    wg_ref[...] = wt_ref[o_d:, :].T.astype(BF16)


def _split_w_in(w_in, o_kv, o_kr, o_d):
    depth, d, d_in = w_in.shape
    w1 = o_d - (o_kr - o_kv) + LANES
    tr = _pick_tile(d, (256, 128))
    assert o_kv % LANES == 0 and (o_d - o_kr) % LANES == 0 and (d_in - o_d) % LANES == 0 and o_kr % SUBLANES == 0
    w_in_t = jnp.swapaxes(w_in, 1, 2)
    return pl.pallas_call(
        functools.partial(_split_w_in_kernel, o_kv=o_kv, o_kr=o_kr, o_d=o_d),
        grid=(depth, d // tr),
        in_specs=[pl.BlockSpec((None, d_in, tr), lambda l, i: (l, 0, i))],
        out_specs=(pl.BlockSpec((None, tr, w1), lambda l, i: (l, i, 0)),
                   pl.BlockSpec((None, tr, d_in - o_d), lambda l, i: (l, i, 0))),
        out_shape=(jax.ShapeDtypeStruct((depth, d, w1), BF16), jax.ShapeDtypeStruct((depth, d, d_in - o_d), BF16)),
        name="split_w_in",
        compiler_params=_params(("parallel", "parallel")),
    )(w_in_t)


def _rope_tabs(pos):
    half = ROPE_DIM // 2
    inv = jnp.exp(-math.log(ROPE_THETA) * jnp.arange(half, dtype=F32) / half)
    ang = pos.astype(F32)[:, None] * inv[None, :]
    cos, sin = jnp.cos(ang), jnp.sin(ang)
    n = pos.shape[0]
    z = lambda w: jnp.zeros((n, w), F32)
    ctab = jnp.concatenate([cos, cos, jnp.ones((n, NOPE_DIM), F32), z(LANES - ROPE_DIM - NOPE_DIM)], axis=1)
    sneg = jnp.concatenate([-sin, z(LANES - half)], axis=1)
    spos = jnp.concatenate([z(half), sin, z(LANES - 2 * half)], axis=1)
    return ctab, sneg, spos


def _pad_rows(w, rows):
    return jnp.pad(w, ((0, 0), (0, rows - w.shape[1]), (0, 0)))


def kernel(x_prompt, x_sample, cache_kv_latent, cache_k_rope, state_conv_b, state_conv_d, ffn1_norm_pre, ffn1_norm_post, ffn1_w_gu, ffn1_w_down, mix_norm_pre, mix_norm_post, w_in, q_norm, w_uq, kv_norm, w_ukv, conv_b_w, conv_b_bias, conv_b_ln_g, conv_b_ln_b, gmlp_vn_g, gmlp_vn_b, gmlp_w_s, gmlp_b_s, conv_d_w, w_br_a, w_br_b, w_br_c, w_br_d, w_o, ffn2_norm_pre, ffn2_norm_post, ffn2_w_gu, ffn2_w_down):
    b, s, d = x_prompt.shape
    bs, t, _ = x_sample.shape
    depth = w_in.shape[0]
    past = cache_kv_latent.shape[2]
    q_lora, kv_lora = q_norm.shape[1], kv_norm.shape[1]
    c_b, c_c, c_d = conv_b_bias.shape[1], gmlp_vn_g.shape[1], conv_d_w.shape[2]
    kb_w, kd_w = conv_b_w.shape[1], conv_d_w.shape[1]
    assert kb_w - 1 <= CONV_B_PAD and kd_w - 1 <= CONV_D_PAD
    assert s % GMLP_CHUNK == 0 and t <= GMLP_CHUNK and t % 16 == 0 and past % CHUNK == 0 and t <= CHUNK
    assert gmlp_w_s.shape[1] == C_GROUPS and w_ukv.shape[2] == A_HEADS * (NOPE_DIM + V_DIM)

    o_q = q_lora
    o_kv = o_q + kv_lora
    o_kr = o_kv + ROPE_DIM
    o_b = o_kr + 2 * c_b
    o_c = o_b + 2 * c_c
    o_d = o_c + 3 * c_d

    vec = lambda p: p[:, None, :]
    uq = w_uq.reshape(depth, q_lora, A_HEADS, NOPE_DIM + ROPE_DIM)
    ukv = w_ukv.reshape(depth, kv_lora, A_HEADS, NOPE_DIM + V_DIM)
    head_zeros = lambda rows, width: jnp.zeros((depth, rows, A_HEADS, width), F32)
    wk_pad = jnp.concatenate(
        [head_zeros(kv_lora, ROPE_DIM), ukv[..., :NOPE_DIM], head_zeros(kv_lora, HEAD_PAD - NOPE_DIM - ROPE_DIM)],
        axis=3).reshape(depth, kv_lora, A_HEADS * HEAD_PAD).astype(BF16)
    w_in1, w_g = _split_w_in(w_in, o_kv, o_kr, o_d)
    w = dict(
        mix_norm_pre=vec(mix_norm_pre), mix_norm_post=vec(mix_norm_post),
        w_in1=w_in1, w_g=w_g,
        q_norm=vec(q_norm),
        wq_pad=jnp.concatenate(
            [uq[..., NOPE_DIM:], uq[..., :NOPE_DIM], head_zeros(q_lora, HEAD_PAD - NOPE_DIM - ROPE_DIM)],
            axis=3).reshape(depth, q_lora, A_HEADS * HEAD_PAD).astype(BF16),
        kv_norm=vec(kv_norm), wk_pad=wk_pad,
        wv=ukv[..., NOPE_DIM:].reshape(depth, kv_lora, A_HEADS * V_DIM).astype(BF16),
        conv_b_w=_pad_rows(conv_b_w, CONV_B_PAD), conv_b_bias=vec(conv_b_bias),
        conv_b_ln_g=vec(conv_b_ln_g), conv_b_ln_b=vec(conv_b_ln_b),
        gmlp_vn_g=vec(gmlp_vn_g), gmlp_vn_b=vec(gmlp_vn_b), gmlp_w_s=gmlp_w_s,
        bs_full=jnp.repeat(jnp.swapaxes(gmlp_b_s, 1, 2), c_c // C_GROUPS, axis=2),
        conv_d_w=_pad_rows(conv_d_w, CONV_D_PAD), kb_w=kb_w, kd_w=kd_w,
        w_br_a=w_br_a.astype(BF16), w_br_b=w_br_b.astype(BF16), w_br_c=w_br_c.astype(BF16),
        w_br_d=w_br_d.astype(BF16), w_o=w_o.astype(BF16))
    wk_t = jnp.swapaxes(wk_pad, 1, 2)
    f1 = (vec(ffn1_norm_pre), vec(ffn1_norm_post), ffn1_w_gu.astype(BF16), ffn1_w_down.astype(BF16))
    f2 = (vec(ffn2_norm_pre), vec(ffn2_norm_post), ffn2_w_gu.astype(BF16), ffn2_w_down.astype(BF16))

    tabs_p = _rope_tabs(jnp.arange(s))
    tabs_s = _rope_tabs(past + jnp.arange(t))
    zero_b = jnp.zeros((b, CONV_B_PAD, c_b), F32)
    zero_d = jnp.zeros((b, CONV_D_PAD, c_d), F32)
    tail_b = CONV_B_PAD - (kb_w - 1)
    tail_d = CONV_D_PAD - (kd_w - 1)
    st_b = jnp.pad(state_conv_b, ((0, 0), (0, 0), (tail_b, 0), (0, 0)))
    st_d = jnp.pad(state_conv_d, ((0, 0), (0, 0), (tail_d, 0), (0, 0)))
    ckr_pad = jnp.pad(cache_k_rope, ((0, 0), (0, 0), (0, 0), (0, LANES - ROPE_DIM))).astype(BF16)
    tq = _pick_tile(s, (256, 128))

    xp = x_prompt.reshape(b * s, d)
    xs = x_sample.reshape(bs * t, d)
    stacked = None
    outs = [[] for _ in range(7)]
    for l in range(depth):
        xp, xs = _ffn(xp, xs, *f1, l)

        qp, kp, vtp, yp, lat_all, kr_all, btp, dtp = _mix_in(
            xp, tabs_p, zero_b, zero_d, None, w, l, b=b, s=s, tq=tq, prompt=True, stacked=stacked)
        stacked = (lat_all, kr_all)
        ap = _attn_prompt(qp.reshape(b, s, -1), kp.reshape(b, s, -1), vtp, tq=tq)

        qs, ys, lats, krs, bts, dts, vgs = _mix_in(
            xs, tabs_s, st_b, st_d, l, w, l, b=bs, s=t, tq=t, prompt=False)
        lats, krs, vgs = (v.reshape(bs, t, -1) for v in (lats, krs, vgs))
        a_s = _attn_sample(qs.reshape(bs, t, -1), cache_kv_latent, ckr_pad, lats, krs, wk_t, w["wv"], l)

        xp, xs = _mix_out(xp, ap.reshape(b * s, -1), yp, xs, a_s.reshape(bs * t, -1), ys, w, l)
        xp, xs = _ffn(xp, xs, *f2, l)

        for lst, val in zip(outs, (btp[:, tail_b:], dtp[:, tail_d:], lats, krs[..., :ROPE_DIM],
                                   bts[:, tail_b:], vgs, dts[:, tail_d:])):
            lst.append(val)
    cb_p, cd_p, lat_s, kr_s, cb_s, vc_s, cd_s = (jnp.stack(o) for o in outs)
    return (xp.reshape(b, s, d), xs.reshape(bs, t, d), stacked[0], stacked[1], cb_p, cd_p,
            lat_s, kr_s, cb_s, vc_s, cd_s)
```

```python
import functools
import math

import jax
import jax.numpy as jnp
from jax import lax
from jax.experimental import pallas as pl
from jax.experimental.pallas import tpu as pltpu

F32 = jnp.float32
BF16 = jnp.bfloat16

CHUNK = 64
A_HEADS = 8
NOPE_DIM = 64
ROPE_DIM = 32
V_DIM = 64
ROPE_THETA = 10000.0
SM_SCALE = (NOPE_DIM + ROPE_DIM) ** -0.5
Q_SCALE = SM_SCALE * math.log2(math.e)
C_GROUPS = 4
GMLP_CHUNK = 128
N_BRANCH = 4
EPS = 1e-6

LANES = 128
SUBLANES = 8
BF16_ROWS = 16
HEAD_PAD = 128
CONV_B_PAD = 32
CONV_D_PAD = 8
VMEM_LIMIT_BYTES = 56 * 1024 * 1024
NEG_BIG = -1e30
ROW_TILES = (1024, 512, 256, 128, 64, 32, 16, 8)


def _pick_tile(n, candidates):
    for c in candidates:
        if n % c == 0:
            return c
    return n


def _layer(arr, l):
    nd = arr.ndim - 1
    return pl.BlockSpec((None,) + arr.shape[1:], lambda *_: (l,) + (0,) * nd, pipeline_mode=pl.Buffered(1))


def _whole(arr):
    nd = arr.ndim
    return pl.BlockSpec(arr.shape, lambda *_: (0,) * nd, pipeline_mode=pl.Buffered(1))


def _rms(x, g):
    ms = jnp.mean(x * x, axis=-1, keepdims=True)
    return x * lax.rsqrt(ms + EPS) * g


def _layernorm(x, g, b):
    mu = jnp.mean(x, axis=-1, keepdims=True)
    xc = x - mu
    var = jnp.mean(xc * xc, axis=-1, keepdims=True)
    return xc * lax.rsqrt(var + EPS) * g + b


def _dot(a, b):
    return jnp.dot(a, b, preferred_element_type=F32)


def _dot_nt(a, b):
    return lax.dot_general(a, b, (((1,), (1,)), ((), ())), preferred_element_type=F32)


def _params(sem):
    return pltpu.CompilerParams(dimension_semantics=sem, vmem_limit_bytes=VMEM_LIMIT_BYTES)


def _ffn_kernel(xp_ref, xs_ref, gpre_ref, gpost_ref, wgu_ref, wd_ref, op_ref, os_ref, act_ref, *, d_ff, tf, n_p):
    def ffn(x_ref, o_ref):
        rows = x_ref.shape[0]
        x = x_ref[...]
        n = _rms(x, gpre_ref[...]).astype(BF16)
        for c in range(d_ff // tf):
            gate = _dot(n, wgu_ref[:, c * tf:(c + 1) * tf])
            up = _dot(n, wgu_ref[:, d_ff + c * tf:d_ff + (c + 1) * tf])
            act_ref[0:rows, c * tf:(c + 1) * tf] = (gate * jax.nn.sigmoid(gate) * up).astype(BF16)
        y = _dot(act_ref[0:rows, :], wd_ref[...])
        o_ref[...] = x + 0.5 * _rms(y, gpost_ref[...])

    i = pl.program_id(0)

    @pl.when(i < n_p)
    def _():
        ffn(xp_ref, op_ref)

    @pl.when(i == n_p)
    def _():
        ffn(xs_ref, os_ref)


def _two_stream_specs(tm, n_p, widths_p, arrays_s):
    prompt = [pl.BlockSpec((tm, wd), lambda i: (jnp.minimum(i, n_p - 1), 0)) for wd in widths_p]
    sample = [pl.BlockSpec(a.shape, lambda i: (0, 0)) for a in arrays_s]
    return prompt, sample


def _ffn(xp, xs, gpre, gpost, wgu, wd, l):
    n, d = xp.shape
    d_ff = wd.shape[1]
    tm = _pick_tile(n, ROW_TILES)
    assert xs.shape[0] <= tm
    n_p = n // tm
    tf = _pick_tile(d_ff, (256, 128))
    weights = (gpre, gpost, wgu, wd)
    (spec_p,), (spec_s,) = _two_stream_specs(tm, n_p, (d,), (xs,))
    return pl.pallas_call(
        functools.partial(_ffn_kernel, d_ff=d_ff, tf=tf, n_p=n_p),
        grid=(n_p + 1,),
        in_specs=[spec_p, spec_s] + [_layer(w, l) for w in weights],
        out_specs=(spec_p, spec_s),
        out_shape=(jax.ShapeDtypeStruct(xp.shape, F32), jax.ShapeDtypeStruct(xs.shape, F32)),
        scratch_shapes=[pltpu.VMEM((tm, d_ff), BF16)],
        name="ffn",
        compiler_params=_params(("arbitrary",)),
    )(xp, xs, *weights)


def _rope128(a, c, sneg, spos):
    return (a * c + pltpu.roll(a, LANES - ROPE_DIM // 2, 1) * sneg
            + pltpu.roll(a, ROPE_DIM // 2, 1) * spos)


MIX_IN_WEIGHTS = ("mix_norm_pre", "w_in1", "q_norm", "wq_pad", "kv_norm", "wk_pad", "wv", "conv_b_w",
                  "conv_b_bias", "conv_b_ln_g", "conv_b_ln_b", "gmlp_vn_g", "gmlp_vn_b", "gmlp_w_s", "bs_full",
                  "conv_d_w")
N_MIX_IN_INPUTS = 6 + len(MIX_IN_WEIGHTS)


def _mix_in_kernel(*refs, prompt, n_alias, n_tiles, nt, tq, q_lora, kv_lora, c_b, c_c, c_d, kb_w, kd_w, rb):
    (x_ref, c_ref, sneg_ref, spos_ref, stb_ref, std_ref,
     gpre_ref, win_ref, qn_ref, wq_ref, kvn_ref, wk_ref, wv_ref,
     cbw_ref, cbb_ref, lng_ref, lnb_ref, vng_ref, vnb_ref, ws_ref, bs_ref, cdw_ref) = refs[:N_MIX_IN_INPUTS]
    rest = refs[N_MIX_IN_INPUTS + n_alias:]
    if prompt:
        q_ref, k_ref, vt_ref, y_ref, lat_ref, kr_ref, bt_ref, dt_ref, ha_ref, hb_ref, xpb_ref, xsh_ref, xpd_ref = rest
    else:
        q_ref, y_ref, lat_ref, kr_ref, bt_ref, dt_ref, vg_ref, ha_ref, hb_ref, xpb_ref, xsh_ref, xpd_ref = rest
    i = pl.program_id(0)

    o_kv = q_lora
    o_b = o_kv + kv_lora
    o_c = o_b + 2 * c_b
    o_d = o_c + 2 * c_c
    o_kr = o_d + 3 * c_d
    col_groups = (0, o_kv, o_b, o_c, o_d, o_kr, o_kr + LANES)

    def project(h_ref):
        n = _rms(x_ref[...], gpre_ref[...]).astype(BF16)
        for c0, c1 in zip(col_groups[:-1], col_groups[1:]):
            h_ref[:, c0:c1] = _dot(n, win_ref[:, c0:c1])
            yield

    def consume(h_ref):
        first = (i - 1) % nt == 0
        ctab, sneg, spos = c_ref[...], sneg_ref[...], spos_ref[...]

        qlat = _rms(h_ref[:, 0:o_kv], qn_ref[...]).astype(BF16)
        qa = _dot(qlat, wq_ref[...])
        for h in range(A_HEADS):
            qh = _rope128(qa[:, h * HEAD_PAD:(h + 1) * HEAD_PAD], ctab, sneg, spos) * Q_SCALE
            q_ref[:, h * HEAD_PAD:(h + 1) * HEAD_PAD] = qh.astype(BF16)

        yield
        lat = _rms(h_ref[:, o_kv:o_b], kvn_ref[...])
        lat_ref[...] = lat
        latb = lat.astype(BF16)
        kr = _rope128(h_ref[:, o_kr:o_kr + LANES], ctab, sneg, spos)
        if prompt:
            kr_ref[...] = kr[:, 0:ROPE_DIM]
            kn = _dot(latb, wk_ref[...])
            for h in range(A_HEADS):
                k_ref[:, h * HEAD_PAD:(h + 1) * HEAD_PAD] = (
                    kn[:, h * HEAD_PAD:(h + 1) * HEAD_PAD] + kr).astype(BF16)
            vt_ref[...] = _dot(latb, wv_ref[...]).T.astype(BF16)
        else:
            kr_ref[...] = kr

        yield
        xb = h_ref[:, o_b:o_b + c_b] * jax.nn.sigmoid(h_ref[:, o_b + c_b:o_c])

        @pl.when(first)
        def _():
            xpb_ref[0:CONV_B_PAD, :] = stb_ref[...]
            xpd_ref[0:CONV_D_PAD, :] = std_ref[...]

        @pl.when(jnp.logical_not(first))
        def _():
            xpb_ref[0:CONV_B_PAD, :] = xpb_ref[tq:tq + CONV_B_PAD, :]
            xpd_ref[0:CONV_D_PAD, :] = xpd_ref[tq:tq + CONV_D_PAD, :]

        xpb_ref[CONV_B_PAD:CONV_B_PAD + tq, :] = xb
        bt_ref[...] = xpb_ref[tq:tq + CONV_B_PAD, :]
        n_sh = tq + CONV_B_PAD - SUBLANES
        for r in range(1, SUBLANES):
            xsh_ref[r, 0:n_sh, :] = xpb_ref[r:r + n_sh, :]

        off_b = CONV_B_PAD - (kb_w - 1)
        for blk in range(tq // rb):
            yield
            acc = jnp.broadcast_to(cbb_ref[...], (rb, c_b))
            for kk in range(kb_w):
                a8, r = divmod(off_b + kk, SUBLANES)
                row0 = blk * rb + a8 * SUBLANES
                src = xpb_ref[row0:row0 + rb, :] if r == 0 else xsh_ref[r, row0:row0 + rb, :]
                acc = acc + src * cbw_ref[kk:kk + 1, :]
            yb = _layernorm(acc, lng_ref[...], lnb_ref[...])
            y_ref[blk * rb:(blk + 1) * rb, 0:c_b] = (yb * jax.nn.sigmoid(yb)).astype(BF16)

        yield
        vg = _layernorm(h_ref[:, o_c + c_c:o_d], vng_ref[...], vnb_ref[...])
        if not prompt:
            vg_ref[...] = vg
        vgb = vg.astype(BF16)
        ck = min(tq, GMLP_CHUNK)
        ri = lax.broadcasted_iota(jnp.int32, (ck, ck), 0) // CHUNK
        ci = lax.broadcasted_iota(jnp.int32, (ck, ck), 1) // CHUNK
        lane_grp = lax.broadcasted_iota(jnp.int32, (ck, c_c), 1) // (c_c // C_GROUPS)
        wmix = [jnp.where(ci <= ri, ws_ref[g, 0:ck, 0:ck], 0.0).astype(BF16) for g in range(C_GROUPS)]
        for c in range(tq // ck):
            vc = vgb[c * ck:(c + 1) * ck, :]
            mix = bs_ref[0:ck, :]
            for g in range(C_GROUPS):
                mix = mix + jnp.where(lane_grp == g, _dot(wmix[g], vc), 0.0)
            y_ref[c * ck:(c + 1) * ck, c_b:c_b + c_c] = (
                h_ref[c * ck:(c + 1) * ck, o_c:o_c + c_c] * mix).astype(BF16)

        yield
        xd = h_ref[:, o_d + c_d:o_d + 2 * c_d] * h_ref[:, o_d + 2 * c_d:o_kr]
        xpd_ref[CONV_D_PAD:CONV_D_PAD + tq, :] = xd
        dt_ref[...] = xpd_ref[tq:tq + CONV_D_PAD, :]
        off_d = CONV_D_PAD - (kd_w - 1)
        conv = xd * cdw_ref[kd_w - 1:kd_w, :]
        for kk in range(kd_w - 1):
            conv = conv + xpd_ref[off_d + kk:off_d + kk + tq, :] * cdw_ref[kk:kk + 1, :]
        y_ref[:, c_b + c_c:c_b + c_c + c_d] = (h_ref[:, o_d:o_d + c_d] * conv).astype(BF16)

    bufs = (ha_ref, hb_ref)

    def run(*stages):
        live = list(stages)
        while live:
            for g in list(live):
                if next(g, StopIteration) is StopIteration:
                    live.remove(g)

    @pl.when(i == 0)
    def _():
        run(project(bufs[0]))

    for par in (0, 1):
        @pl.when((i >= 1) & (i < n_tiles) & (i % 2 == par))
        def _(par=par):
            run(project(bufs[par]), consume(bufs[1 - par]))

    @pl.when(i == n_tiles)
    def _():
        run(consume(bufs[(n_tiles - 1) % 2]))


def _mix_in(x, tabs, stb, std, st_layer, w, l, *, b, s, tq, prompt, stacked=None):
    d = x.shape[1]
    depth = w["w_in1"].shape[0]
    q_lora, kv_lora = w["q_norm"].shape[2], w["kv_norm"].shape[2]
    c_b, c_c, c_d = w["conv_b_bias"].shape[2], w["gmlp_vn_g"].shape[2], w["conv_d_w"].shape[2]
    hp = A_HEADS * HEAD_PAD
    rb = min(tq, 64)
    nt = s // tq
    n_tiles = b * nt
    done = lambda i: jnp.maximum(i - 1, 0)
    row_in = pl.BlockSpec((tq, d), lambda i: (jnp.minimum(i, n_tiles - 1), 0))
    row = lambda wd: pl.BlockSpec((tq, wd), lambda i: (done(i), 0))
    per_b = lambda r, wd: pl.BlockSpec((None, r, wd), lambda i: (done(i) // nt, 0, 0))
    if st_layer is None:
        st_spec = per_b
    else:
        st_spec = lambda r, wd: pl.BlockSpec((None, None, r, wd), lambda i: (st_layer, done(i) // nt, 0, 0))
    tab = pl.BlockSpec((tq, LANES), lambda i: (done(i) % nt, 0))
    weights = [w[k] for k in MIX_IN_WEIGHTS]
    sds = jax.ShapeDtypeStruct
    small = [
        (sds((b, CONV_B_PAD, c_b), F32), per_b(CONV_B_PAD, c_b)),
        (sds((b, CONV_D_PAD, c_d), F32), per_b(CONV_D_PAD, c_d)),
    ]
    outs = [(sds((b * s, hp), BF16), row(hp))]
    aliases = {}
    alias_in, alias_specs = [], []
    if prompt:
        vt_spec = pl.BlockSpec((None, None, A_HEADS * V_DIM, tq), lambda i: (done(i) // nt, done(i) % nt, 0, 0))
        stk = lambda wd: pl.BlockSpec((None, None, tq, wd), lambda i: (l, done(i) // nt, done(i) % nt, 0))
        outs += [(sds((b * s, hp), BF16), row(hp)),
                 (sds((b, nt, A_HEADS * V_DIM, tq), BF16), vt_spec),
                 (sds((b * s, c_b + c_c + c_d), BF16), row(c_b + c_c + c_d)),
                 (sds((depth, b, s, kv_lora), F32), stk(kv_lora)),
                 (sds((depth, b, s, ROPE_DIM), F32), stk(ROPE_DIM))]
        outs += small
        if stacked is not None:
            alias_in = list(stacked)
            alias_specs = [pl.BlockSpec(memory_space=pl.ANY)] * 2
            aliases = {N_MIX_IN_INPUTS: 4, N_MIX_IN_INPUTS + 1: 5}
    else:
        outs += [(sds((b * s, c_b + c_c + c_d), BF16), row(c_b + c_c + c_d)),
                 (sds((b * s, kv_lora), F32), row(kv_lora)),
                 (sds((b * s, LANES), F32), row(LANES))]
        outs += small + [(sds((b * s, c_c), F32), row(c_c))]
    out_shape, out_specs = zip(*outs)
    w1 = w["w_in1"].shape[2]
    return pl.pallas_call(
        functools.partial(_mix_in_kernel, prompt=prompt, n_alias=len(alias_in), n_tiles=n_tiles, nt=nt, tq=tq,
                          q_lora=q_lora, kv_lora=kv_lora, c_b=c_b, c_c=c_c, c_d=c_d, kb_w=w["kb_w"],
                          kd_w=w["kd_w"], rb=rb),
        grid=(n_tiles + 1,),
        in_specs=[row_in, tab, tab, tab, st_spec(CONV_B_PAD, c_b), st_spec(CONV_D_PAD, c_d)]
        + [_layer(a, l) for a in weights] + alias_specs,
        out_specs=out_specs,
        out_shape=out_shape,
        input_output_aliases=aliases,
        scratch_shapes=[pltpu.VMEM((tq, w1), F32), pltpu.VMEM((tq, w1), F32),
                        pltpu.VMEM((CONV_B_PAD + tq, c_b), F32),
                        pltpu.VMEM((SUBLANES, CONV_B_PAD + tq, c_b), F32),
                        pltpu.VMEM((CONV_D_PAD + tq, c_d), F32)],
        name="mix_in",
        compiler_params=_params(("arbitrary",)),
    )(x, *tabs, stb, std, *weights, *alias_in)


def _attn_kernel(q_ref, k_ref, vt_ref, o_ref, st0_ref, st1_ref, m_ref, acc_ref, *, tq):
    g = pl.program_id(1)
    kc = lax.broadcasted_iota(jnp.int32, (tq, tq), 0) // CHUNK
    qc = lax.broadcasted_iota(jnp.int32, (tq, tq), 1) // CHUNK
    diag_ok = kc <= qc
    heads = tuple(range(A_HEADS))
    ones = jnp.ones((BF16_ROWS, tq), BF16)
    tile_a, tile_b = 0, 1

    def score_head(tile, j, st_ref, h):
        k0 = pl.multiple_of(j * tq, tq)
        st_ref[h] = _dot_nt(k_ref[pl.ds(k0, tq), h * HEAD_PAD:(h + 1) * HEAD_PAD],
                            q_ref[tile * tq:(tile + 1) * tq, h * HEAD_PAD:(h + 1) * HEAD_PAD])

    def update_head(tile, j, st_ref, h, masked):
        st = st_ref[h]
        if masked:
            st = jnp.where(diag_ok, st, NEG_BIG)
        m = m_ref[tile, h]
        m_new = jnp.maximum(m, jnp.max(st, axis=0, keepdims=True))
        alpha = jnp.exp2(m - m_new)
        pb = jnp.exp2((st - m_new).astype(BF16))
        vt1 = jnp.concatenate([vt_ref[j, h * V_DIM:(h + 1) * V_DIM, :], ones], axis=0)
        m_ref[tile, h] = m_new
        acc_ref[tile, h] = alpha * acc_ref[tile, h] + _dot(vt1, pb)

    def both(tile_n, j_next, next_ref, tile_c, j, cur_ref, masked=False):
        for h in heads:
            score_head(tile_n, j_next, next_ref, h)
            update_head(tile_c, j, cur_ref, h, masked)

    def pairs(tile, first_ref, second_ref):
        def pair(p, _):
            j = 2 * p
            both(tile, j + 1, second_ref, tile, j, first_ref)
            both(tile, j + 2, first_ref, tile, j + 1, second_ref)
            return 0
        lax.fori_loop(0, g, pair, 0)

    def finish(tile):
        ot = jnp.concatenate([acc_ref[tile, h, 0:V_DIM, :] / acc_ref[tile, h, V_DIM:V_DIM + 1, :] for h in heads],
                             axis=0)
        o_ref[tile * tq:(tile + 1) * tq, :] = ot.T.astype(BF16)

    m_ref[...] = jnp.full(m_ref.shape, NEG_BIG, F32)
    acc_ref[...] = jnp.zeros(acc_ref.shape, F32)
    for h in heads:
        score_head(tile_a, 0, st0_ref, h)
    pairs(tile_a, st0_ref, st1_ref)
    both(tile_b, 0, st1_ref, tile_a, 2 * g, st0_ref, masked=True)
    finish(tile_a)
    pairs(tile_b, st1_ref, st0_ref)
    both(tile_b, 2 * g + 1, st0_ref, tile_b, 2 * g, st1_ref)
    for h in heads:
        update_head(tile_b, 2 * g + 1, st0_ref, h, True)
    finish(tile_b)


def _attn_prompt(q, k, vt, *, tq):
    b, s, hp = q.shape
    nblk, wv = vt.shape[1], vt.shape[2]
    assert s % (2 * tq) == 0
    return pl.pallas_call(
        functools.partial(_attn_kernel, tq=tq),
        grid=(b, s // (2 * tq)),
        in_specs=[
            pl.BlockSpec((None, 2 * tq, hp), lambda i, j: (i, j, 0)),
            pl.BlockSpec((None, s, hp), lambda i, j: (i, 0, 0)),
            pl.BlockSpec((None, nblk, wv, tq), lambda i, j: (i, 0, 0, 0)),
        ],
        out_specs=pl.BlockSpec((None, 2 * tq, wv), lambda i, j: (i, j, 0)),
        out_shape=jax.ShapeDtypeStruct((b, s, wv), BF16),
        scratch_shapes=[pltpu.VMEM((A_HEADS, tq, tq), F32), pltpu.VMEM((A_HEADS, tq, tq), F32),
                        pltpu.VMEM((2, A_HEADS, 1, tq), F32),
                        pltpu.VMEM((2, A_HEADS, V_DIM + BF16_ROWS, tq), F32)],
        name="attn_prompt",
        compiler_params=_params(("parallel", "parallel")),
    )(q, k, vt)


def _attn_sample_kernel(q_ref, clat_ref, ckr_ref, lat_ref, kr_ref, wkt_ref, wv_ref, o_ref, kn_ref, *, t):
    q = q_ref[...]
    qh = [q[:, h * HEAD_PAD:(h + 1) * HEAD_PAD] for h in range(A_HEADS)]
    qabs = jnp.concatenate(
        [_dot(qh[h], wkt_ref[h * HEAD_PAD:(h + 1) * HEAD_PAD, :]) for h in range(A_HEADS)], axis=0)
    qcat = jnp.concatenate([qabs.astype(BF16), jnp.concatenate(qh, axis=0)], axis=1)
    clat = clat_ref[...].astype(BF16)
    kv_lora = clat.shape[1]
    kc = jnp.concatenate([clat, ckr_ref[...]], axis=1)
    kn_ref[...] = jnp.zeros(kn_ref.shape, BF16)
    kn_ref[0:t, :] = jnp.concatenate([lat_ref[...], kr_ref[...]], axis=1).astype(BF16)
    kn = kn_ref[...]
    s1 = _dot_nt(qcat, kc)
    s2 = _dot_nt(qcat, kn)
    s2 = jnp.where(lax.broadcasted_iota(jnp.int32, s2.shape, 1) < t, s2, NEG_BIG)
    m = jnp.maximum(jnp.max(s1, axis=-1, keepdims=True), jnp.max(s2, axis=-1, keepdims=True))
    p1 = jnp.exp2(s1 - m)
    p2 = jnp.exp2(s2 - m)
    l = jnp.sum(p1, axis=-1, keepdims=True) + jnp.sum(p2, axis=-1, keepdims=True)
    olat = ((_dot(p1.astype(BF16), clat) + _dot(p2.astype(BF16), kn[:, 0:kv_lora])) / l).astype(BF16)
    low_half = lax.broadcasted_iota(jnp.int32, (t, LANES), 1) < V_DIM
    for pair in range(A_HEADS // 2):
        wpair = wv_ref[:, pair * LANES:(pair + 1) * LANES]
        lo = _dot(olat[(2 * pair) * t:(2 * pair + 1) * t, :], wpair)
        hi = _dot(olat[(2 * pair + 1) * t:(2 * pair + 2) * t, :], wpair)
        o_ref[:, pair * LANES:(pair + 1) * LANES] = jnp.where(low_half, lo, hi).astype(BF16)


def _attn_sample(q, cache_lat, cache_kr, lat, kr, wkt, wv, l):
    b, t, hp = q.shape
    past, kv_lora = cache_lat.shape[2], cache_lat.shape[3]
    wvw = wv.shape[2]
    per_b = lambda r, w: pl.BlockSpec((None, r, w), lambda i: (i, 0, 0))
    cache = lambda r, w: pl.BlockSpec((None, None, r, w), lambda i: (l, i, 0, 0))
    return pl.pallas_call(
        functools.partial(_attn_sample_kernel, t=t),
        grid=(b,),
        in_specs=[per_b(t, hp), cache(past, kv_lora), cache(past, LANES), per_b(t, kv_lora),
                  per_b(t, LANES), _layer(wkt, l), _layer(wv, l)],
        out_specs=per_b(t, wvw),
        out_shape=jax.ShapeDtypeStruct((b, t, wvw), BF16),
        scratch_shapes=[pltpu.VMEM((LANES, kv_lora + LANES), BF16)],
        name="attn_sample",
        compiler_params=_params(("parallel",)),
    )(q, cache_lat, cache_kr, lat, kr, wkt, wv)


MIX_OUT_WEIGHTS = ("mix_norm_pre", "mix_norm_post", "w_g", "w_br_a", "w_br_b", "w_br_c", "w_br_d", "w_o")


def _mix_out_kernel(xp_ref, ap_ref, yp_ref, xs_ref, as_ref, ys_ref, gpre_ref, gpost_ref, wg_ref, wba_ref, wbb_ref,
                    wbc_ref, wbd_ref, wo_ref, op_ref, os_ref, *, d, c_b, c_c, c_d, n_p):
    def merge(x_ref, a_ref, y_ref, o_ref):
        x = x_ref[...]
        n = _rms(x, gpre_ref[...]).astype(BF16)
        branches = (
            (a_ref[...], wba_ref),
            (y_ref[:, 0:c_b], wbb_ref),
            (y_ref[:, c_b:c_b + c_c], wbc_ref),
            (y_ref[:, c_b + c_c:c_b + c_c + c_d], wbd_ref),
        )
        merged = jnp.zeros(x.shape, F32)
        for k, (br, w_ref) in enumerate(branches):
            gate = jax.nn.sigmoid(_dot(n, wg_ref[:, k * d:(k + 1) * d]))
            merged = merged + gate * _dot(br, w_ref[...])
        out = _dot(merged.astype(BF16), wo_ref[...])
        o_ref[...] = x + _rms(out, gpost_ref[...])

    i = pl.program_id(0)

    @pl.when(i < n_p)
    def _():
        merge(xp_ref, ap_ref, yp_ref, op_ref)

    @pl.when(i == n_p)
    def _():
        merge(xs_ref, as_ref, ys_ref, os_ref)


def _mix_out(xp, ap, yp, xs, a_s, ys, w, l):
    n, d = xp.shape
    c_b, c_c, c_d = w["w_br_b"].shape[1], w["w_br_c"].shape[1], w["w_br_d"].shape[1]
    tm = _pick_tile(n, ROW_TILES)
    assert xs.shape[0] <= tm
    n_p = n // tm
    weights = [w[k] for k in MIX_OUT_WEIGHTS]
    specs_p, specs_s = _two_stream_specs(tm, n_p, (d, ap.shape[1], yp.shape[1]), (xs, a_s, ys))
    return pl.pallas_call(
        functools.partial(_mix_out_kernel, d=d, c_b=c_b, c_c=c_c, c_d=c_d, n_p=n_p),
        grid=(n_p + 1,),
        in_specs=specs_p + specs_s + [_layer(a_, l) for a_ in weights],
        out_specs=(specs_p[0], specs_s[0]),
        out_shape=(jax.ShapeDtypeStruct(xp.shape, F32), jax.ShapeDtypeStruct(xs.shape, F32)),
        name="mix_out",
        compiler_params=_params(("arbitrary",)),
    )(xp, ap, yp, xs, a_s, ys, *weights)


def _split_w_in_kernel(wt_ref, w1_ref, wg_ref, *, o_kv, o_kr, o_d):
    cols = wt_ref.shape[1]
    body = o_d - o_kr
    w1_ref[:, 0:o_kv] = wt_ref[0:o_kv, :].T.astype(BF16)
    w1_ref[:, o_kv:o_kv + body] = wt_ref[o_kr:o_d, :].T.astype(BF16)
    kr_rows = jnp.concatenate([wt_ref[o_kv:o_kr, :], jnp.zeros((LANES - (o_kr - o_kv), cols), F32)], axis=0)
    w1_ref[:, o_kv + body:o_kv + body + LANES] = kr_rows.T.astype(BF16)
    wg_ref[...] = wt_ref[o_d:, :].T.astype(BF16)


def _split_w_in(w_in, o_kv, o_kr, o_d):
    depth, d, d_in = w_in.shape
    w1 = o_d - (o_kr - o_kv) + LANES
    tr = _pick_tile(d, (256, 128))
    assert o_kv % LANES == 0 and (o_d - o_kr) % LANES == 0 and (d_in - o_d) % LANES == 0 and o_kr % SUBLANES == 0
    w_in_t = jnp.swapaxes(w_in, 1, 2)
    return pl.pallas_call(
        functools.partial(_split_w_in_kernel, o_kv=o_kv, o_kr=o_kr, o_d=o_d),
        grid=(depth, d // tr),
        in_specs=[pl.BlockSpec((None, d_in, tr), lambda l, i: (l, 0, i))],
        out_specs=(pl.BlockSpec((None, tr, w1), lambda l, i: (l, i, 0)),
                   pl.BlockSpec((None, tr, d_in - o_d), lambda l, i: (l, i, 0))),
        out_shape=(jax.ShapeDtypeStruct((depth, d, w1), BF16), jax.ShapeDtypeStruct((depth, d, d_in - o_d), BF16)),
        name="split_w_in",
        compiler_params=_params(("parallel", "parallel")),
    )(w_in_t)


def _rope_tabs(pos):
    half = ROPE_DIM // 2
    inv = jnp.exp(-math.log(ROPE_THETA) * jnp.arange(half, dtype=F32) / half)
    ang = pos.astype(F32)[:, None] * inv[None, :]
    cos, sin = jnp.cos(ang), jnp.sin(ang)
    n = pos.shape[0]
    z = lambda w: jnp.zeros((n, w), F32)
    ctab = jnp.concatenate([cos, cos, jnp.ones((n, NOPE_DIM), F32), z(LANES - ROPE_DIM - NOPE_DIM)], axis=1)
    sneg = jnp.concatenate([-sin, z(LANES - half)], axis=1)
    spos = jnp.concatenate([z(half), sin, z(LANES - 2 * half)], axis=1)
    return ctab, sneg, spos


def _pad_rows(w, rows):
    return jnp.pad(w, ((0, 0), (0, rows - w.shape[1]), (0, 0)))


def kernel(x_prompt, x_sample, cache_kv_latent, cache_k_rope, state_conv_b, state_conv_d, ffn1_norm_pre, ffn1_norm_post, ffn1_w_gu, ffn1_w_down, mix_norm_pre, mix_norm_post, w_in, q_norm, w_uq, kv_norm, w_ukv, conv_b_w, conv_b_bias, conv_b_ln_g, conv_b_ln_b, gmlp_vn_g, gmlp_vn_b, gmlp_w_s, gmlp_b_s, conv_d_w, w_br_a, w_br_b, w_br_c, w_br_d, w_o, ffn2_norm_pre, ffn2_norm_post, ffn2_w_gu, ffn2_w_down):
    b, s, d = x_prompt.shape
    bs, t, _ = x_sample.shape
    depth = w_in.shape[0]
    past = cache_kv_latent.shape[2]
    q_lora, kv_lora = q_norm.shape[1], kv_norm.shape[1]
    c_b, c_c, c_d = conv_b_bias.shape[1], gmlp_vn_g.shape[1], conv_d_w.shape[2]
    kb_w, kd_w = conv_b_w.shape[1], conv_d_w.shape[1]
    assert kb_w - 1 <= CONV_B_PAD and kd_w - 1 <= CONV_D_PAD
    assert s % GMLP_CHUNK == 0 and t <= GMLP_CHUNK and t % 16 == 0 and past % CHUNK == 0 and t <= CHUNK
    assert gmlp_w_s.shape[1] == C_GROUPS and w_ukv.shape[2] == A_HEADS * (NOPE_DIM + V_DIM)

    o_q = q_lora
    o_kv = o_q + kv_lora
    o_kr = o_kv + ROPE_DIM
    o_b = o_kr + 2 * c_b
    o_c = o_b + 2 * c_c
    o_d = o_c + 3 * c_d

    vec = lambda p: p[:, None, :]
    uq = w_uq.reshape(depth, q_lora, A_HEADS, NOPE_DIM + ROPE_DIM)
    ukv = w_ukv.reshape(depth, kv_lora, A_HEADS, NOPE_DIM + V_DIM)
    head_zeros = lambda rows, width: jnp.zeros((depth, rows, A_HEADS, width), F32)
    wk_pad = jnp.concatenate(
        [head_zeros(kv_lora, ROPE_DIM), ukv[..., :NOPE_DIM], head_zeros(kv_lora, HEAD_PAD - NOPE_DIM - ROPE_DIM)],
        axis=3).reshape(depth, kv_lora, A_HEADS * HEAD_PAD).astype(BF16)
    w_in1, w_g = _split_w_in(w_in, o_kv, o_kr, o_d)
    w = dict(
        mix_norm_pre=vec(mix_norm_pre), mix_norm_post=vec(mix_norm_post),
        w_in1=w_in1, w_g=w_g,
        q_norm=vec(q_norm),
        wq_pad=jnp.concatenate(
            [uq[..., NOPE_DIM:], uq[..., :NOPE_DIM], head_zeros(q_lora, HEAD_PAD - NOPE_DIM - ROPE_DIM)],
            axis=3).reshape(depth, q_lora, A_HEADS * HEAD_PAD).astype(BF16),
        kv_norm=vec(kv_norm), wk_pad=wk_pad,
        wv=ukv[..., NOPE_DIM:].reshape(depth, kv_lora, A_HEADS * V_DIM).astype(BF16),
        conv_b_w=_pad_rows(conv_b_w, CONV_B_PAD), conv_b_bias=vec(conv_b_bias),
        conv_b_ln_g=vec(conv_b_ln_g), conv_b_ln_b=vec(conv_b_ln_b),
        gmlp_vn_g=vec(gmlp_vn_g), gmlp_vn_b=vec(gmlp_vn_b), gmlp_w_s=gmlp_w_s,
        bs_full=jnp.repeat(jnp.swapaxes(gmlp_b_s, 1, 2), c_c // C_GROUPS, axis=2),
        conv_d_w=_pad_rows(conv_d_w, CONV_D_PAD), kb_w=kb_w, kd_w=kd_w,
        w_br_a=w_br_a.astype(BF16), w_br_b=w_br_b.astype(BF16), w_br_c=w_br_c.astype(BF16),
        w_br_d=w_br_d.astype(BF16), w_o=w_o.astype(BF16))
    wk_t = jnp.swapaxes(wk_pad, 1, 2)
    f1 = (vec(ffn1_norm_pre), vec(ffn1_norm_post), ffn1_w_gu.astype(BF16), ffn1_w_down.astype(BF16))
    f2 = (vec(ffn2_norm_pre), vec(ffn2_norm_post), ffn2_w_gu.astype(BF16), ffn2_w_down.astype(BF16))

    tabs_p = _rope_tabs(jnp.arange(s))
    tabs_s = _rope_tabs(past + jnp.arange(t))
    zero_b = jnp.zeros((b, CONV_B_PAD, c_b), F32)
    zero_d = jnp.zeros((b, CONV_D_PAD, c_d), F32)
    tail_b = CONV_B_PAD - (kb_w - 1)
    tail_d = CONV_D_PAD - (kd_w - 1)
    st_b = jnp.pad(state_conv_b, ((0, 0), (0, 0), (tail_b, 0), (0, 0)))
    st_d = jnp.pad(state_conv_d, ((0, 0), (0, 0), (tail_d, 0), (0, 0)))
    ckr_pad = jnp.pad(cache_k_rope, ((0, 0), (0, 0), (0, 0), (0, LANES - ROPE_DIM))).astype(BF16)
    tq = _pick_tile(s, (256, 128))

    xp = x_prompt.reshape(b * s, d)
    xs = x_sample.reshape(bs * t, d)
    stacked = None
    outs = [[] for _ in range(7)]
    for l in range(depth):
        xp, xs = _ffn(xp, xs, *f1, l)

        qp, kp, vtp, yp, lat_all, kr_all, btp, dtp = _mix_in(
            xp, tabs_p, zero_b, zero_d, None, w, l, b=b, s=s, tq=tq, prompt=True, stacked=stacked)
        stacked = (lat_all, kr_all)
        ap = _attn_prompt(qp.reshape(b, s, -1), kp.reshape(b, s, -1), vtp, tq=tq)

        qs, ys, lats, krs, bts, dts, vgs = _mix_in(
            xs, tabs_s, st_b, st_d, l, w, l, b=bs, s=t, tq=t, prompt=False)
        lats, krs, vgs = (v.reshape(bs, t, -1) for v in (lats, krs, vgs))
        a_s = _attn_sample(qs.reshape(bs, t, -1), cache_kv_latent, ckr_pad, lats, krs, wk_t, w["wv"], l)

        xp, xs = _mix_out(xp, ap.reshape(b * s, -1), yp, xs, a_s.reshape(bs * t, -1), ys, w, l)
        xp, xs = _ffn(xp, xs, *f2, l)

        for lst, val in zip(outs, (btp[:, tail_b:], dtp[:, tail_d:], lats, krs[..., :ROPE_DIM],
                                   bts[:, tail_b:], vgs, dts[:, tail_d:])):
            lst.append(val)
    cb_p, cd_p, lat_s, kr_s, cb_s, vc_s, cd_s = (jnp.stack(o) for o in outs)
    return (xp.reshape(b, s, d), xs.reshape(bs, t, d), stacked[0], stacked[1], cb_p, cd_p,
            lat_s, kr_s, cb_s, vc_s, cd_s)
```

```python
import functools
import math

import jax
import jax.numpy as jnp
from jax import lax
from jax.experimental import pallas as pl
from jax.experimental.pallas import tpu as pltpu

F32 = jnp.float32
BF16 = jnp.bfloat16

CHUNK = 64
A_HEADS = 8
NOPE_DIM = 64
ROPE_DIM = 32
V_DIM = 64
ROPE_THETA = 10000.0
SM_SCALE = (NOPE_DIM + ROPE_DIM) ** -0.5
Q_SCALE = SM_SCALE * math.log2(math.e)
C_GROUPS = 4
GMLP_CHUNK = 128
N_BRANCH = 4
EPS = 1e-6

LANES = 128
SUBLANES = 8
BF16_ROWS = 16
HEAD_PAD = 128
CONV_B_PAD = 32
CONV_D_PAD = 8
VMEM_LIMIT_BYTES = 56 * 1024 * 1024
NEG_BIG = -1e30
ATTN_TILES_PER_STEP = (4, 2)
ROW_TILES = (1024, 512, 256, 128, 64, 32, 16, 8)


def _pick_tile(n, candidates):
    for c in candidates:
        if n % c == 0:
            return c
    return n


def _layer(arr, l):
    nd = arr.ndim - 1
    return pl.BlockSpec((None,) + arr.shape[1:], lambda *_: (l,) + (0,) * nd, pipeline_mode=pl.Buffered(1))


def _whole(arr):
    nd = arr.ndim
    return pl.BlockSpec(arr.shape, lambda *_: (0,) * nd, pipeline_mode=pl.Buffered(1))


def _rms(x, g):
    ms = jnp.mean(x * x, axis=-1, keepdims=True)
    return x * lax.rsqrt(ms + EPS) * g


def _layernorm(x, g, b):
    mu = jnp.mean(x, axis=-1, keepdims=True)
    xc = x - mu
    var = jnp.mean(xc * xc, axis=-1, keepdims=True)
    return xc * lax.rsqrt(var + EPS) * g + b


def _dot(a, b):
    return jnp.dot(a, b, preferred_element_type=F32)


def _dot_nt(a, b):
    return lax.dot_general(a, b, (((1,), (1,)), ((), ())), preferred_element_type=F32)


def _params(sem):
    return pltpu.CompilerParams(dimension_semantics=sem, vmem_limit_bytes=VMEM_LIMIT_BYTES)


def _ffn_kernel(xp_ref, xs_ref, gpre_ref, gpost_ref, wgu_ref, wd_ref, op_ref, os_ref, act_ref, *, d_ff, tf, n_p):
    def ffn(x_ref, o_ref):
        rows = x_ref.shape[0]
        x = x_ref[...]
        n = _rms(x, gpre_ref[...]).astype(BF16)
        for c in range(d_ff // tf):
            gate = _dot(n, wgu_ref[:, c * tf:(c + 1) * tf])
            up = _dot(n, wgu_ref[:, d_ff + c * tf:d_ff + (c + 1) * tf])
            act_ref[0:rows, c * tf:(c + 1) * tf] = (gate * jax.nn.sigmoid(gate) * up).astype(BF16)
        y = _dot(act_ref[0:rows, :], wd_ref[...])
        o_ref[...] = x + 0.5 * _rms(y, gpost_ref[...])

    i = pl.program_id(0)

    @pl.when(i < n_p)
    def _():
        ffn(xp_ref, op_ref)

    @pl.when(i == n_p)
    def _():
        ffn(xs_ref, os_ref)


def _two_stream_specs(tm, n_p, widths_p, arrays_s):
    prompt = [pl.BlockSpec((tm, wd), lambda i: (jnp.minimum(i, n_p - 1), 0)) for wd in widths_p]
    sample = [pl.BlockSpec(a.shape, lambda i: (0, 0)) for a in arrays_s]
    return prompt, sample


def _ffn(xp, xs, gpre, gpost, wgu, wd, l):
    n, d = xp.shape
    d_ff = wd.shape[1]
    tm = _pick_tile(n, ROW_TILES)
    assert xs.shape[0] <= tm
    n_p = n // tm
    tf = _pick_tile(d_ff, (256, 128))
    weights = (gpre, gpost, wgu, wd)
    (spec_p,), (spec_s,) = _two_stream_specs(tm, n_p, (d,), (xs,))
    return pl.pallas_call(
        functools.partial(_ffn_kernel, d_ff=d_ff, tf=tf, n_p=n_p),
        grid=(n_p + 1,),
        in_specs=[spec_p, spec_s] + [_layer(w, l) for w in weights],
        out_specs=(spec_p, spec_s),
        out_shape=(jax.ShapeDtypeStruct(xp.shape, F32), jax.ShapeDtypeStruct(xs.shape, F32)),
        scratch_shapes=[pltpu.VMEM((tm, d_ff), BF16)],
        name="ffn",
        compiler_params=_params(("arbitrary",)),
    )(xp, xs, *weights)


def _rope128(a, c, sneg, spos):
    return (a * c + pltpu.roll(a, LANES - ROPE_DIM // 2, 1) * sneg
            + pltpu.roll(a, ROPE_DIM // 2, 1) * spos)


MIX_IN_WEIGHTS = ("mix_norm_pre", "w_in1", "q_norm", "wq_pad", "kv_norm", "wk_pad", "wv", "conv_b_w",
                  "conv_b_bias", "conv_b_ln_g", "conv_b_ln_b", "gmlp_vn_g", "gmlp_vn_b", "gmlp_w_s", "bs_full",
                  "conv_d_w")
N_MIX_IN_INPUTS = 6 + len(MIX_IN_WEIGHTS)


def _mix_in_kernel(*refs, prompt, n_alias, n_tiles, nt, tq, q_lora, kv_lora, c_b, c_c, c_d, kb_w, kd_w, rb):
    (x_ref, c_ref, sneg_ref, spos_ref, stb_ref, std_ref,
     gpre_ref, win_ref, qn_ref, wq_ref, kvn_ref, wk_ref, wv_ref,
     cbw_ref, cbb_ref, lng_ref, lnb_ref, vng_ref, vnb_ref, ws_ref, bs_ref, cdw_ref) = refs[:N_MIX_IN_INPUTS]
    rest = refs[N_MIX_IN_INPUTS + n_alias:]
    if prompt:
        q_ref, k_ref, vt_ref, y_ref, lat_ref, kr_ref, bt_ref, dt_ref, ha_ref, hb_ref, xpb_ref, xsh_ref, xpd_ref = rest
    else:
        q_ref, y_ref, lat_ref, kr_ref, bt_ref, dt_ref, vg_ref, ha_ref, hb_ref, xpb_ref, xsh_ref, xpd_ref = rest
    i = pl.program_id(0)

    o_kv = q_lora
    o_b = o_kv + kv_lora
    o_c = o_b + 2 * c_b
    o_d = o_c + 2 * c_c
    o_kr = o_d + 3 * c_d
    col_groups = (0, o_kv, o_b, o_c, o_d, o_kr, o_kr + LANES)

    def project(h_ref):
        n = _rms(x_ref[...], gpre_ref[...]).astype(BF16)
        for c0, c1 in zip(col_groups[:-1], col_groups[1:]):
            h_ref[:, c0:c1] = _dot(n, win_ref[:, c0:c1])
            yield

    def consume(h_ref):
        first = (i - 1) % nt == 0
        ctab, sneg, spos = c_ref[...], sneg_ref[...], spos_ref[...]

        qlat = _rms(h_ref[:, 0:o_kv], qn_ref[...]).astype(BF16)
        qa = _dot(qlat, wq_ref[...])
        for h in range(A_HEADS):
            qh = _rope128(qa[:, h * HEAD_PAD:(h + 1) * HEAD_PAD], ctab, sneg, spos) * Q_SCALE
            q_ref[:, h * HEAD_PAD:(h + 1) * HEAD_PAD] = qh.astype(BF16)

        yield
        lat = _rms(h_ref[:, o_kv:o_b], kvn_ref[...])
        lat_ref[...] = lat
        latb = lat.astype(BF16)
        kr = _rope128(h_ref[:, o_kr:o_kr + LANES], ctab, sneg, spos)
        if prompt:
            kr_ref[...] = kr[:, 0:ROPE_DIM]
            kn = _dot(latb, wk_ref[...])
            for h in range(A_HEADS):
                k_ref[:, h * HEAD_PAD:(h + 1) * HEAD_PAD] = (
                    kn[:, h * HEAD_PAD:(h + 1) * HEAD_PAD] + kr).astype(BF16)
            vt_ref[...] = _dot(latb, wv_ref[...]).T.astype(BF16)
        else:
            kr_ref[...] = kr

        yield
        xb = h_ref[:, o_b:o_b + c_b] * jax.nn.sigmoid(h_ref[:, o_b + c_b:o_c])

        @pl.when(first)
        def _():
            xpb_ref[0:CONV_B_PAD, :] = stb_ref[...]
            xpd_ref[0:CONV_D_PAD, :] = std_ref[...]

        @pl.when(jnp.logical_not(first))
        def _():
            xpb_ref[0:CONV_B_PAD, :] = xpb_ref[tq:tq + CONV_B_PAD, :]
            xpd_ref[0:CONV_D_PAD, :] = xpd_ref[tq:tq + CONV_D_PAD, :]

        xpb_ref[CONV_B_PAD:CONV_B_PAD + tq, :] = xb
        bt_ref[...] = xpb_ref[tq:tq + CONV_B_PAD, :]
        n_sh = tq + CONV_B_PAD - SUBLANES
        for r in range(1, SUBLANES):
            xsh_ref[r, 0:n_sh, :] = xpb_ref[r:r + n_sh, :]

        off_b = CONV_B_PAD - (kb_w - 1)
        for blk in range(tq // rb):
            yield
            acc = jnp.broadcast_to(cbb_ref[...], (rb, c_b))
            for kk in range(kb_w):
                a8, r = divmod(off_b + kk, SUBLANES)
                row0 = blk * rb + a8 * SUBLANES
                src = xpb_ref[row0:row0 + rb, :] if r == 0 else xsh_ref[r, row0:row0 + rb, :]
                acc = acc + src * cbw_ref[kk:kk + 1, :]
            yb = _layernorm(acc, lng_ref[...], lnb_ref[...])
            y_ref[blk * rb:(blk + 1) * rb, 0:c_b] = (yb * jax.nn.sigmoid(yb)).astype(BF16)

        yield
        vg = _layernorm(h_ref[:, o_c + c_c:o_d], vng_ref[...], vnb_ref[...])
        if not prompt:
            vg_ref[...] = vg
        vgb = vg.astype(BF16)
        ck = min(tq, GMLP_CHUNK)
        ri = lax.broadcasted_iota(jnp.int32, (ck, ck), 0) // CHUNK
        ci = lax.broadcasted_iota(jnp.int32, (ck, ck), 1) // CHUNK
        lane_grp = lax.broadcasted_iota(jnp.int32, (ck, c_c), 1) // (c_c // C_GROUPS)
        wmix = [jnp.where(ci <= ri, ws_ref[g, 0:ck, 0:ck], 0.0).astype(BF16) for g in range(C_GROUPS)]
        for c in range(tq // ck):
            vc = vgb[c * ck:(c + 1) * ck, :]
            mix = bs_ref[0:ck, :]
            for g in range(C_GROUPS):
                mix = mix + jnp.where(lane_grp == g, _dot(wmix[g], vc), 0.0)
            y_ref[c * ck:(c + 1) * ck, c_b:c_b + c_c] = (
                h_ref[c * ck:(c + 1) * ck, o_c:o_c + c_c] * mix).astype(BF16)

        yield
        xd = h_ref[:, o_d + c_d:o_d + 2 * c_d] * h_ref[:, o_d + 2 * c_d:o_kr]
        xpd_ref[CONV_D_PAD:CONV_D_PAD + tq, :] = xd
        dt_ref[...] = xpd_ref[tq:tq + CONV_D_PAD, :]
        off_d = CONV_D_PAD - (kd_w - 1)
        conv = xd * cdw_ref[kd_w - 1:kd_w, :]
        for kk in range(kd_w - 1):
            conv = conv + xpd_ref[off_d + kk:off_d + kk + tq, :] * cdw_ref[kk:kk + 1, :]
        y_ref[:, c_b + c_c:c_b + c_c + c_d] = (h_ref[:, o_d:o_d + c_d] * conv).astype(BF16)

    bufs = (ha_ref, hb_ref)

    def run(*stages):
        live = list(stages)
        while live:
            for g in list(live):
                if next(g, StopIteration) is StopIteration:
                    live.remove(g)

    @pl.when(i == 0)
    def _():
        run(project(bufs[0]))

    for par in (0, 1):
        @pl.when((i >= 1) & (i < n_tiles) & (i % 2 == par))
        def _(par=par):
            run(project(bufs[par]), consume(bufs[1 - par]))

    @pl.when(i == n_tiles)
    def _():
        run(consume(bufs[(n_tiles - 1) % 2]))


def _mix_in(x, tabs, stb, std, st_layer, w, l, *, b, s, tq, prompt, stacked=None):
    d = x.shape[1]
    depth = w["w_in1"].shape[0]
    q_lora, kv_lora = w["q_norm"].shape[2], w["kv_norm"].shape[2]
    c_b, c_c, c_d = w["conv_b_bias"].shape[2], w["gmlp_vn_g"].shape[2], w["conv_d_w"].shape[2]
    hp = A_HEADS * HEAD_PAD
    rb = min(tq, 64)
    nt = s // tq
    n_tiles = b * nt
    done = lambda i: jnp.maximum(i - 1, 0)
    row_in = pl.BlockSpec((tq, d), lambda i: (jnp.minimum(i, n_tiles - 1), 0))
    row = lambda wd: pl.BlockSpec((tq, wd), lambda i: (done(i), 0))
    per_b = lambda r, wd: pl.BlockSpec((None, r, wd), lambda i: (done(i) // nt, 0, 0))
    if st_layer is None:
        st_spec = per_b
    else:
        st_spec = lambda r, wd: pl.BlockSpec((None, None, r, wd), lambda i: (st_layer, done(i) // nt, 0, 0))
    tab = pl.BlockSpec((tq, LANES), lambda i: (done(i) % nt, 0))
    weights = [w[k] for k in MIX_IN_WEIGHTS]
    sds = jax.ShapeDtypeStruct
    small = [
        (sds((b, CONV_B_PAD, c_b), F32), per_b(CONV_B_PAD, c_b)),
        (sds((b, CONV_D_PAD, c_d), F32), per_b(CONV_D_PAD, c_d)),
    ]
    outs = [(sds((b * s, hp), BF16), row(hp))]
    aliases = {}
    alias_in, alias_specs = [], []
    if prompt:
        vt_spec = pl.BlockSpec((None, None, A_HEADS * V_DIM, tq), lambda i: (done(i) // nt, done(i) % nt, 0, 0))
        stk = lambda wd: pl.BlockSpec((None, None, tq, wd), lambda i: (l, done(i) // nt, done(i) % nt, 0))
        outs += [(sds((b * s, hp), BF16), row(hp)),
                 (sds((b, nt, A_HEADS * V_DIM, tq), BF16), vt_spec),
                 (sds((b * s, c_b + c_c + c_d), BF16), row(c_b + c_c + c_d)),
                 (sds((depth, b, s, kv_lora), F32), stk(kv_lora)),
                 (sds((depth, b, s, ROPE_DIM), F32), stk(ROPE_DIM))]
        outs += small
        if stacked is not None:
            alias_in = list(stacked)
            alias_specs = [pl.BlockSpec(memory_space=pl.ANY)] * 2
            aliases = {N_MIX_IN_INPUTS: 4, N_MIX_IN_INPUTS + 1: 5}
    else:
        outs += [(sds((b * s, c_b + c_c + c_d), BF16), row(c_b + c_c + c_d)),
                 (sds((b * s, kv_lora), F32), row(kv_lora)),
                 (sds((b * s, LANES), F32), row(LANES))]
        outs += small + [(sds((b * s, c_c), F32), row(c_c))]
    out_shape, out_specs = zip(*outs)
    w1 = w["w_in1"].shape[2]
    return pl.pallas_call(
        functools.partial(_mix_in_kernel, prompt=prompt, n_alias=len(alias_in), n_tiles=n_tiles, nt=nt, tq=tq,
                          q_lora=q_lora, kv_lora=kv_lora, c_b=c_b, c_c=c_c, c_d=c_d, kb_w=w["kb_w"],
                          kd_w=w["kd_w"], rb=rb),
        grid=(n_tiles + 1,),
        in_specs=[row_in, tab, tab, tab, st_spec(CONV_B_PAD, c_b), st_spec(CONV_D_PAD, c_d)]
        + [_layer(a, l) for a in weights] + alias_specs,
        out_specs=out_specs,
        out_shape=out_shape,
        input_output_aliases=aliases,
        scratch_shapes=[pltpu.VMEM((tq, w1), F32), pltpu.VMEM((tq, w1), F32),
                        pltpu.VMEM((CONV_B_PAD + tq, c_b), F32),
                        pltpu.VMEM((SUBLANES, CONV_B_PAD + tq, c_b), F32),
                        pltpu.VMEM((CONV_D_PAD + tq, c_d), F32)],
        name="mix_in",
        compiler_params=_params(("arbitrary",)),
    )(x, *tabs, stb, std, *weights, *alias_in)


def _attn_kernel(q_ref, k_ref, vt_ref, o_ref, st0_ref, st1_ref, m_ref, acc_ref, *, tq, nt):
    g = pl.program_id(1)
    kc = lax.broadcasted_iota(jnp.int32, (tq, tq), 0) // CHUNK
    qc = lax.broadcasted_iota(jnp.int32, (tq, tq), 1) // CHUNK
    diag_ok = kc <= qc
    heads = tuple(range(A_HEADS))
    ones = jnp.ones((BF16_ROWS, tq), BF16)

    def score_head(tile, j, st_ref, h):
        k0 = pl.multiple_of(j * tq, tq)
        st_ref[h] = _dot_nt(k_ref[pl.ds(k0, tq), h * HEAD_PAD:(h + 1) * HEAD_PAD],
                            q_ref[tile * tq:(tile + 1) * tq, h * HEAD_PAD:(h + 1) * HEAD_PAD])

    def update_head(tile, j, st_ref, h, masked):
        st = st_ref[h]
        if masked:
            st = jnp.where(diag_ok, st, NEG_BIG)
        m = m_ref[tile, h]
        m_new = jnp.maximum(m, jnp.max(st, axis=0, keepdims=True))
        alpha = jnp.exp2(m - m_new)
        pb = jnp.exp2((st - m_new).astype(BF16))
        vt1 = jnp.concatenate([vt_ref[j, h * V_DIM:(h + 1) * V_DIM, :], ones], axis=0)
        m_ref[tile, h] = m_new
        acc_ref[tile, h] = alpha * acc_ref[tile, h] + _dot(vt1, pb)

    def both(tile_n, j_next, next_ref, tile_c, j, cur_ref, masked=False):
        for h in heads:
            score_head(tile_n, j_next, next_ref, h)
            update_head(tile_c, j, cur_ref, h, masked)

    def finish(tile):
        ot = jnp.concatenate([acc_ref[tile, h, 0:V_DIM, :] / acc_ref[tile, h, V_DIM:V_DIM + 1, :] for h in heads],
                             axis=0)
        o_ref[tile * tq:(tile + 1) * tq, :] = ot.T.astype(BF16)

    m_ref[...] = jnp.full(m_ref.shape, NEG_BIG, F32)
    acc_ref[...] = jnp.zeros(acc_ref.shape, F32)
    cur, nxt = st0_ref, st1_ref
    for h in heads:
        score_head(0, 0, cur, h)
    for tile in range(nt):
        full = nt * g + tile

        def pair(p, _, tile=tile, cur=cur, nxt=nxt):
            j = 2 * p
            both(tile, j + 1, nxt, tile, j, cur)
            both(tile, j + 2, cur, tile, j + 1, nxt)
            return 0

        lax.fori_loop(0, (nt // 2) * g + tile // 2, pair, 0)
        if tile % 2 == 1:
            both(tile, full, nxt, tile, full - 1, cur)
            cur, nxt = nxt, cur
        if tile + 1 < nt:
            both(tile + 1, 0, nxt, tile, full, cur, masked=True)
            cur, nxt = nxt, cur
        else:
            for h in heads:
                update_head(tile, full, cur, h, True)
        finish(tile)


def _attn_prompt(q, k, vt, *, tq):
    b, s, hp = q.shape
    nblk, wv = vt.shape[1], vt.shape[2]
    nt = _pick_tile(s // tq, ATTN_TILES_PER_STEP)
    assert nt % 2 == 0
    return pl.pallas_call(
        functools.partial(_attn_kernel, tq=tq, nt=nt),
        grid=(b, s // (nt * tq)),
        in_specs=[
            pl.BlockSpec((None, nt * tq, hp), lambda i, j: (i, j, 0)),
            pl.BlockSpec((None, s, hp), lambda i, j: (i, 0, 0)),
            pl.BlockSpec((None, nblk, wv, tq), lambda i, j: (i, 0, 0, 0)),
        ],
        out_specs=pl.BlockSpec((None, nt * tq, wv), lambda i, j: (i, j, 0)),
        out_shape=jax.ShapeDtypeStruct((b, s, wv), BF16),
        scratch_shapes=[pltpu.VMEM((A_HEADS, tq, tq), F32), pltpu.VMEM((A_HEADS, tq, tq), F32),
                        pltpu.VMEM((nt, A_HEADS, 1, tq), F32),
                        pltpu.VMEM((nt, A_HEADS, V_DIM + BF16_ROWS, tq), F32)],
        name="attn_prompt",
        compiler_params=_params(("parallel", "parallel")),
    )(q, k, vt)


def _attn_sample_kernel(q_ref, clat_ref, ckr_ref, lat_ref, kr_ref, wkt_ref, wv_ref, o_ref, kn_ref, *, t):
    q = q_ref[...]
    qh = [q[:, h * HEAD_PAD:(h + 1) * HEAD_PAD] for h in range(A_HEADS)]
    qabs = jnp.concatenate(
        [_dot(qh[h], wkt_ref[h * HEAD_PAD:(h + 1) * HEAD_PAD, :]) for h in range(A_HEADS)], axis=0)
    qcat = jnp.concatenate([qabs.astype(BF16), jnp.concatenate(qh, axis=0)], axis=1)
    clat = clat_ref[...].astype(BF16)
    kv_lora = clat.shape[1]
    kc = jnp.concatenate([clat, ckr_ref[...]], axis=1)
    kn_ref[...] = jnp.zeros(kn_ref.shape, BF16)
    kn_ref[0:t, :] = jnp.concatenate([lat_ref[...], kr_ref[...]], axis=1).astype(BF16)
    kn = kn_ref[...]
    s1 = _dot_nt(qcat, kc)
    s2 = _dot_nt(qcat, kn)
    s2 = jnp.where(lax.broadcasted_iota(jnp.int32, s2.shape, 1) < t, s2, NEG_BIG)
    m = jnp.maximum(jnp.max(s1, axis=-1, keepdims=True), jnp.max(s2, axis=-1, keepdims=True))
    p1 = jnp.exp2(s1 - m)
    p2 = jnp.exp2(s2 - m)
    l = jnp.sum(p1, axis=-1, keepdims=True) + jnp.sum(p2, axis=-1, keepdims=True)
    olat = ((_dot(p1.astype(BF16), clat) + _dot(p2.astype(BF16), kn[:, 0:kv_lora])) / l).astype(BF16)
    low_half = lax.broadcasted_iota(jnp.int32, (t, LANES), 1) < V_DIM
    for pair in range(A_HEADS // 2):
        wpair = wv_ref[:, pair * LANES:(pair + 1) * LANES]
        lo = _dot(olat[(2 * pair) * t:(2 * pair + 1) * t, :], wpair)
        hi = _dot(olat[(2 * pair + 1) * t:(2 * pair + 2) * t, :], wpair)
        o_ref[:, pair * LANES:(pair + 1) * LANES] = jnp.where(low_half, lo, hi).astype(BF16)


def _attn_sample(q, cache_lat, cache_kr, lat, kr, wkt, wv, l):
    b, t, hp = q.shape
    past, kv_lora = cache_lat.shape[2], cache_lat.shape[3]
    wvw = wv.shape[2]
    per_b = lambda r, w: pl.BlockSpec((None, r, w), lambda i: (i, 0, 0))
    cache = lambda r, w: pl.BlockSpec((None, None, r, w), lambda i: (l, i, 0, 0))
    return pl.pallas_call(
        functools.partial(_attn_sample_kernel, t=t),
        grid=(b,),
        in_specs=[per_b(t, hp), cache(past, kv_lora), cache(past, LANES), per_b(t, kv_lora),
                  per_b(t, LANES), _layer(wkt, l), _layer(wv, l)],
        out_specs=per_b(t, wvw),
        out_shape=jax.ShapeDtypeStruct((b, t, wvw), BF16),
        scratch_shapes=[pltpu.VMEM((LANES, kv_lora + LANES), BF16)],
        name="attn_sample",
        compiler_params=_params(("parallel",)),
    )(q, cache_lat, cache_kr, lat, kr, wkt, wv)


MIX_OUT_WEIGHTS = ("mix_norm_pre", "mix_norm_post", "w_g", "w_br_a", "w_br_b", "w_br_c", "w_br_d", "w_o")


def _mix_out_kernel(xp_ref, ap_ref, yp_ref, xs_ref, as_ref, ys_ref, gpre_ref, gpost_ref, wg_ref, wba_ref, wbb_ref,
                    wbc_ref, wbd_ref, wo_ref, op_ref, os_ref, *, d, c_b, c_c, c_d, n_p):
    def merge(x_ref, a_ref, y_ref, o_ref):
        x = x_ref[...]
        n = _rms(x, gpre_ref[...]).astype(BF16)
        branches = (
            (a_ref[...], wba_ref),
            (y_ref[:, 0:c_b], wbb_ref),
            (y_ref[:, c_b:c_b + c_c], wbc_ref),
            (y_ref[:, c_b + c_c:c_b + c_c + c_d], wbd_ref),
        )
        merged = jnp.zeros(x.shape, F32)
        for k, (br, w_ref) in enumerate(branches):
            gate = jax.nn.sigmoid(_dot(n, wg_ref[:, k * d:(k + 1) * d]))
            merged = merged + gate * _dot(br, w_ref[...])
        out = _dot(merged.astype(BF16), wo_ref[...])
        o_ref[...] = x + _rms(out, gpost_ref[...])

    i = pl.program_id(0)

    @pl.when(i < n_p)
    def _():
        merge(xp_ref, ap_ref, yp_ref, op_ref)

    @pl.when(i == n_p)
    def _():
        merge(xs_ref, as_ref, ys_ref, os_ref)


def _mix_out(xp, ap, yp, xs, a_s, ys, w, l):
    n, d = xp.shape
    c_b, c_c, c_d = w["w_br_b"].shape[1], w["w_br_c"].shape[1], w["w_br_d"].shape[1]
    tm = _pick_tile(n, ROW_TILES)
    assert xs.shape[0] <= tm
    n_p = n // tm
    weights = [w[k] for k in MIX_OUT_WEIGHTS]
    specs_p, specs_s = _two_stream_specs(tm, n_p, (d, ap.shape[1], yp.shape[1]), (xs, a_s, ys))
    return pl.pallas_call(
        functools.partial(_mix_out_kernel, d=d, c_b=c_b, c_c=c_c, c_d=c_d, n_p=n_p),
        grid=(n_p + 1,),
        in_specs=specs_p + specs_s + [_layer(a_, l) for a_ in weights],
        out_specs=(specs_p[0], specs_s[0]),
        out_shape=(jax.ShapeDtypeStruct(xp.shape, F32), jax.ShapeDtypeStruct(xs.shape, F32)),
        name="mix_out",
        compiler_params=_params(("arbitrary",)),
    )(xp, ap, yp, xs, a_s, ys, *weights)


def _split_w_in_kernel(wt_ref, w1_ref, wg_ref, *, o_kv, o_kr, o_d):
    cols = wt_ref.shape[1]
    body = o_d - o_kr
    w1_ref[:, 0:o_kv] = wt_ref[0:o_kv, :].T.astype(BF16)
    w1_ref[:, o_kv:o_kv + body] = wt_ref[o_kr:o_d, :].T.astype(BF16)
    kr_rows = jnp.concatenate([wt_ref[o_kv:o_kr, :], jnp.zeros((LANES - (o_kr - o_kv), cols), F32)], axis=0)
    w1_ref[:, o_kv + body:o_kv + body + LANES] = kr_rows.T.astype(BF16)
    wg_ref[...] = wt_ref[o_d:, :].T.astype(BF16)


def _split_w_in(w_in, o_kv, o_kr, o_d):
    depth, d, d_in = w_in.shape
    w1 = o_d - (o_kr - o_kv) + LANES
    tr = _pick_tile(d, (256, 128))
    assert o_kv % LANES == 0 and (o_d - o_kr) % LANES == 0 and (d_in - o_d) % LANES == 0 and o_kr % SUBLANES == 0
    w_in_t = jnp.swapaxes(w_in, 1, 2)
    return pl.pallas_call(
        functools.partial(_split_w_in_kernel, o_kv=o_kv, o_kr=o_kr, o_d=o_d),
        grid=(depth, d // tr),
        in_specs=[pl.BlockSpec((None, d_in, tr), lambda l, i: (l, 0, i))],
        out_specs=(pl.BlockSpec((None, tr, w1), lambda l, i: (l, i, 0)),
                   pl.BlockSpec((None, tr, d_in - o_d), lambda l, i: (l, i, 0))),
        out_shape=(jax.ShapeDtypeStruct((depth, d, w1), BF16), jax.ShapeDtypeStruct((depth, d, d_in - o_d), BF16)),
        name="split_w_in",
        compiler_params=_params(("parallel", "parallel")),
    )(w_in_t)


def _rope_tabs(pos):
    half = ROPE_DIM // 2
    inv = jnp.exp(-math.log(ROPE_THETA) * jnp.arange(half, dtype=F32) / half)
    ang = pos.astype(F32)[:, None] * inv[None, :]
    cos, sin = jnp.cos(ang), jnp.sin(ang)
    n = pos.shape[0]
    z = lambda w: jnp.zeros((n, w), F32)
    ctab = jnp.concatenate([cos, cos, jnp.ones((n, NOPE_DIM), F32), z(LANES - ROPE_DIM - NOPE_DIM)], axis=1)
    sneg = jnp.concatenate([-sin, z(LANES - half)], axis=1)
    spos = jnp.concatenate([z(half), sin, z(LANES - 2 * half)], axis=1)
    return ctab, sneg, spos


def _pad_rows(w, rows):
    return jnp.pad(w, ((0, 0), (0, rows - w.shape[1]), (0, 0)))


def kernel(x_prompt, x_sample, cache_kv_latent, cache_k_rope, state_conv_b, state_conv_d, ffn1_norm_pre, ffn1_norm_post, ffn1_w_gu, ffn1_w_down, mix_norm_pre, mix_norm_post, w_in, q_norm, w_uq, kv_norm, w_ukv, conv_b_w, conv_b_bias, conv_b_ln_g, conv_b_ln_b, gmlp_vn_g, gmlp_vn_b, gmlp_w_s, gmlp_b_s, conv_d_w, w_br_a, w_br_b, w_br_c, w_br_d, w_o, ffn2_norm_pre, ffn2_norm_post, ffn2_w_gu, ffn2_w_down):
    b, s, d = x_prompt.shape
    bs, t, _ = x_sample.shape
    depth = w_in.shape[0]
    past = cache_kv_latent.shape[2]
    q_lora, kv_lora = q_norm.shape[1], kv_norm.shape[1]
    c_b, c_c, c_d = conv_b_bias.shape[1], gmlp_vn_g.shape[1], conv_d_w.shape[2]
    kb_w, kd_w = conv_b_w.shape[1], conv_d_w.shape[1]
    assert kb_w - 1 <= CONV_B_PAD and kd_w - 1 <= CONV_D_PAD
    assert s % GMLP_CHUNK == 0 and t <= GMLP_CHUNK and t % 16 == 0 and past % CHUNK == 0 and t <= CHUNK
    assert gmlp_w_s.shape[1] == C_GROUPS and w_ukv.shape[2] == A_HEADS * (NOPE_DIM + V_DIM)

    o_q = q_lora
    o_kv = o_q + kv_lora
    o_kr = o_kv + ROPE_DIM
    o_b = o_kr + 2 * c_b
    o_c = o_b + 2 * c_c
    o_d = o_c + 3 * c_d

    vec = lambda p: p[:, None, :]
    uq = w_uq.reshape(depth, q_lora, A_HEADS, NOPE_DIM + ROPE_DIM)
    ukv = w_ukv.reshape(depth, kv_lora, A_HEADS, NOPE_DIM + V_DIM)
    head_zeros = lambda rows, width: jnp.zeros((depth, rows, A_HEADS, width), F32)
    wk_pad = jnp.concatenate(
        [head_zeros(kv_lora, ROPE_DIM), ukv[..., :NOPE_DIM], head_zeros(kv_lora, HEAD_PAD - NOPE_DIM - ROPE_DIM)],
        axis=3).reshape(depth, kv_lora, A_HEADS * HEAD_PAD).astype(BF16)
    w_in1, w_g = _split_w_in(w_in, o_kv, o_kr, o_d)
    w = dict(
        mix_norm_pre=vec(mix_norm_pre), mix_norm_post=vec(mix_norm_post),
        w_in1=w_in1, w_g=w_g,
        q_norm=vec(q_norm),
        wq_pad=jnp.concatenate(
            [uq[..., NOPE_DIM:], uq[..., :NOPE_DIM], head_zeros(q_lora, HEAD_PAD - NOPE_DIM - ROPE_DIM)],
            axis=3).reshape(depth, q_lora, A_HEADS * HEAD_PAD).astype(BF16),
        kv_norm=vec(kv_norm), wk_pad=wk_pad,
        wv=ukv[..., NOPE_DIM:].reshape(depth, kv_lora, A_HEADS * V_DIM).astype(BF16),
        conv_b_w=_pad_rows(conv_b_w, CONV_B_PAD), conv_b_bias=vec(conv_b_bias),
        conv_b_ln_g=vec(conv_b_ln_g), conv_b_ln_b=vec(conv_b_ln_b),
        gmlp_vn_g=vec(gmlp_vn_g), gmlp_vn_b=vec(gmlp_vn_b), gmlp_w_s=gmlp_w_s,
        bs_full=jnp.repeat(jnp.swapaxes(gmlp_b_s, 1, 2), c_c // C_GROUPS, axis=2),
        conv_d_w=_pad_rows(conv_d_w, CONV_D_PAD), kb_w=kb_w, kd_w=kd_w,
        w_br_a=w_br_a.astype(BF16), w_br_b=w_br_b.astype(BF16), w_br_c=w_br_c.astype(BF16),
        w_br_d=w_br_d.astype(BF16), w_o=w_o.astype(BF16))
    wk_t = jnp.swapaxes(wk_pad, 1, 2)
    f1 = (vec(ffn1_norm_pre), vec(ffn1_norm_post), ffn1_w_gu.astype(BF16), ffn1_w_down.astype(BF16))
    f2 = (vec(ffn2_norm_pre), vec(ffn2_norm_post), ffn2_w_gu.astype(BF16), ffn2_w_down.astype(BF16))

    tabs_p = _rope_tabs(jnp.arange(s))
    tabs_s = _rope_tabs(past + jnp.arange(t))
    zero_b = jnp.zeros((b, CONV_B_PAD, c_b), F32)
    zero_d = jnp.zeros((b, CONV_D_PAD, c_d), F32)
    tail_b = CONV_B_PAD - (kb_w - 1)
    tail_d = CONV_D_PAD - (kd_w - 1)
    st_b = jnp.pad(state_conv_b, ((0, 0), (0, 0), (tail_b, 0), (0, 0)))
    st_d = jnp.pad(state_conv_d, ((0, 0), (0, 0), (tail_d, 0), (0, 0)))
    ckr_pad = jnp.pad(cache_k_rope, ((0, 0), (0, 0), (0, 0), (0, LANES - ROPE_DIM))).astype(BF16)
    tq = _pick_tile(s, (256, 128))

    xp = x_prompt.reshape(b * s, d)
    xs = x_sample.reshape(bs * t, d)
    stacked = None
    outs = [[] for _ in range(7)]
    for l in range(depth):
        xp, xs = _ffn(xp, xs, *f1, l)

        qp, kp, vtp, yp, lat_all, kr_all, btp, dtp = _mix_in(
            xp, tabs_p, zero_b, zero_d, None, w, l, b=b, s=s, tq=tq, prompt=True, stacked=stacked)
        stacked = (lat_all, kr_all)
        ap = _attn_prompt(qp.reshape(b, s, -1), kp.reshape(b, s, -1), vtp, tq=tq)

        qs, ys, lats, krs, bts, dts, vgs = _mix_in(
            xs, tabs_s, st_b, st_d, l, w, l, b=bs, s=t, tq=t, prompt=False)
        lats, krs, vgs = (v.reshape(bs, t, -1) for v in (lats, krs, vgs))
        a_s = _attn_sample(qs.reshape(bs, t, -1), cache_kv_latent, ckr_pad, lats, krs, wk_t, w["wv"], l)

        xp, xs = _mix_out(xp, ap.reshape(b * s, -1), yp, xs, a_s.reshape(bs * t, -1), ys, w, l)
        xp, xs = _ffn(xp, xs, *f2, l)

        for lst, val in zip(outs, (btp[:, tail_b:], dtp[:, tail_d:], lats, krs[..., :ROPE_DIM],
                                   bts[:, tail_b:], vgs, dts[:, tail_d:])):
            lst.append(val)
    cb_p, cd_p, lat_s, kr_s, cb_s, vc_s, cd_s = (jnp.stack(o) for o in outs)
    return (xp.reshape(b, s, d), xs.reshape(bs, t, d), stacked[0], stacked[1], cb_p, cd_p,
            lat_s, kr_s, cb_s, vc_s, cd_s)
```

```python
import functools
import math

import jax
import jax.numpy as jnp
from jax import lax
from jax.experimental import pallas as pl
from jax.experimental.pallas import tpu as pltpu

F32 = jnp.float32
BF16 = jnp.bfloat16

CHUNK = 64
A_HEADS = 8
NOPE_DIM = 64
ROPE_DIM = 32
V_DIM = 64
ROPE_THETA = 10000.0
SM_SCALE = (NOPE_DIM + ROPE_DIM) ** -0.5
Q_SCALE = SM_SCALE * math.log2(math.e)
C_GROUPS = 4
GMLP_CHUNK = 128
N_BRANCH = 4
EPS = 1e-6

LANES = 128
SUBLANES = 8
BF16_ROWS = 16
HEAD_PAD = 128
CONV_B_PAD = 32
CONV_D_PAD = 8
VMEM_LIMIT_BYTES = 56 * 1024 * 1024
NEG_BIG = -1e30
SCORE_LEAD = 2
ATTN_TILES_PER_STEP = (4, 2)
ROW_TILES = (1024, 512, 256, 128, 64, 32, 16, 8)


def _pick_tile(n, candidates):
    for c in candidates:
        if n % c == 0:
            return c
    return n


def _layer(arr, l):
    nd = arr.ndim - 1
    return pl.BlockSpec((None,) + arr.shape[1:], lambda *_: (l,) + (0,) * nd, pipeline_mode=pl.Buffered(1))


def _whole(arr):
    nd = arr.ndim
    return pl.BlockSpec(arr.shape, lambda *_: (0,) * nd, pipeline_mode=pl.Buffered(1))


def _rms(x, g):
    ms = jnp.mean(x * x, axis=-1, keepdims=True)
    return x * lax.rsqrt(ms + EPS) * g


def _layernorm(x, g, b):
    mu = jnp.mean(x, axis=-1, keepdims=True)
    xc = x - mu
    var = jnp.mean(xc * xc, axis=-1, keepdims=True)
    return xc * lax.rsqrt(var + EPS) * g + b


def _dot(a, b):
    return jnp.dot(a, b, preferred_element_type=F32)


def _dot_nt(a, b):
    return lax.dot_general(a, b, (((1,), (1,)), ((), ())), preferred_element_type=F32)


def _params(sem):
    return pltpu.CompilerParams(dimension_semantics=sem, vmem_limit_bytes=VMEM_LIMIT_BYTES)


def _ffn_kernel(xp_ref, xs_ref, gpre_ref, gpost_ref, wgu_ref, wd_ref, op_ref, os_ref, act_ref, *, d_ff, tf, n_p):
    def ffn(x_ref, o_ref):
        rows = x_ref.shape[0]
        x = x_ref[...]
        n = _rms(x, gpre_ref[...]).astype(BF16)
        for c in range(d_ff // tf):
            gate = _dot(n, wgu_ref[:, c * tf:(c + 1) * tf])
            up = _dot(n, wgu_ref[:, d_ff + c * tf:d_ff + (c + 1) * tf])
            act_ref[0:rows, c * tf:(c + 1) * tf] = (gate * jax.nn.sigmoid(gate) * up).astype(BF16)
        y = _dot(act_ref[0:rows, :], wd_ref[...])
        o_ref[...] = x + 0.5 * _rms(y, gpost_ref[...])

    i = pl.program_id(0)

    @pl.when(i < n_p)
    def _():
        ffn(xp_ref, op_ref)

    @pl.when(i == n_p)
    def _():
        ffn(xs_ref, os_ref)


def _two_stream_specs(tm, n_p, widths_p, arrays_s):
    prompt = [pl.BlockSpec((tm, wd), lambda i: (jnp.minimum(i, n_p - 1), 0)) for wd in widths_p]
    sample = [pl.BlockSpec(a.shape, lambda i: (0, 0)) for a in arrays_s]
    return prompt, sample


def _ffn(xp, xs, gpre, gpost, wgu, wd, l):
    n, d = xp.shape
    d_ff = wd.shape[1]
    tm = _pick_tile(n, ROW_TILES)
    assert xs.shape[0] <= tm
    n_p = n // tm
    tf = _pick_tile(d_ff, (256, 128))
    weights = (gpre, gpost, wgu, wd)
    (spec_p,), (spec_s,) = _two_stream_specs(tm, n_p, (d,), (xs,))
    return pl.pallas_call(
        functools.partial(_ffn_kernel, d_ff=d_ff, tf=tf, n_p=n_p),
        grid=(n_p + 1,),
        in_specs=[spec_p, spec_s] + [_layer(w, l) for w in weights],
        out_specs=(spec_p, spec_s),
        out_shape=(jax.ShapeDtypeStruct(xp.shape, F32), jax.ShapeDtypeStruct(xs.shape, F32)),
        scratch_shapes=[pltpu.VMEM((tm, d_ff), BF16)],
        name="ffn",
        compiler_params=_params(("arbitrary",)),
    )(xp, xs, *weights)


def _rope128(a, c, sneg, spos):
    return (a * c + pltpu.roll(a, LANES - ROPE_DIM // 2, 1) * sneg
            + pltpu.roll(a, ROPE_DIM // 2, 1) * spos)


MIX_IN_WEIGHTS = ("mix_norm_pre", "w_in1", "q_norm", "wq_pad", "kv_norm", "wk_pad", "wv", "conv_b_w",
                  "conv_b_bias", "conv_b_ln_g", "conv_b_ln_b", "gmlp_vn_g", "gmlp_vn_b", "gmlp_w_s", "bs_full",
                  "conv_d_w")
N_MIX_IN_INPUTS = 6 + len(MIX_IN_WEIGHTS)


def _mix_in_kernel(*refs, prompt, n_alias, n_tiles, nt, tq, q_lora, kv_lora, c_b, c_c, c_d, kb_w, kd_w, rb):
    (x_ref, c_ref, sneg_ref, spos_ref, stb_ref, std_ref,
     gpre_ref, win_ref, qn_ref, wq_ref, kvn_ref, wk_ref, wv_ref,
     cbw_ref, cbb_ref, lng_ref, lnb_ref, vng_ref, vnb_ref, ws_ref, bs_ref, cdw_ref) = refs[:N_MIX_IN_INPUTS]
    rest = refs[N_MIX_IN_INPUTS + n_alias:]
    if prompt:
        q_ref, k_ref, vt_ref, y_ref, lat_ref, kr_ref, bt_ref, dt_ref, ha_ref, hb_ref, xpb_ref, xsh_ref, xpd_ref = rest
    else:
        q_ref, y_ref, lat_ref, kr_ref, bt_ref, dt_ref, vg_ref, ha_ref, hb_ref, xpb_ref, xsh_ref, xpd_ref = rest
    i = pl.program_id(0)

    o_kv = q_lora
    o_b = o_kv + kv_lora
    o_c = o_b + 2 * c_b
    o_d = o_c + 2 * c_c
    o_kr = o_d + 3 * c_d
    col_groups = (0, o_kv, o_b, o_c, o_d, o_kr, o_kr + LANES)

    def project(h_ref):
        n = _rms(x_ref[...], gpre_ref[...]).astype(BF16)
        for c0, c1 in zip(col_groups[:-1], col_groups[1:]):
            h_ref[:, c0:c1] = _dot(n, win_ref[:, c0:c1])
            yield

    def consume(h_ref):
        first = (i - 1) % nt == 0
        ctab, sneg, spos = c_ref[...], sneg_ref[...], spos_ref[...]

        qlat = _rms(h_ref[:, 0:o_kv], qn_ref[...]).astype(BF16)
        qa = _dot(qlat, wq_ref[...])
        for h in range(A_HEADS):
            qh = _rope128(qa[:, h * HEAD_PAD:(h + 1) * HEAD_PAD], ctab, sneg, spos) * Q_SCALE
            q_ref[:, h * HEAD_PAD:(h + 1) * HEAD_PAD] = qh.astype(BF16)

        yield
        lat = _rms(h_ref[:, o_kv:o_b], kvn_ref[...])
        lat_ref[...] = lat
        latb = lat.astype(BF16)
        kr = _rope128(h_ref[:, o_kr:o_kr + LANES], ctab, sneg, spos)
        if prompt:
            kr_ref[...] = kr[:, 0:ROPE_DIM]
            kn = _dot(latb, wk_ref[...])
            for h in range(A_HEADS):
                k_ref[:, h * HEAD_PAD:(h + 1) * HEAD_PAD] = (
                    kn[:, h * HEAD_PAD:(h + 1) * HEAD_PAD] + kr).astype(BF16)
            vt_ref[...] = _dot(latb, wv_ref[...]).T.astype(BF16)
        else:
            kr_ref[...] = kr

        yield
        xb = h_ref[:, o_b:o_b + c_b] * jax.nn.sigmoid(h_ref[:, o_b + c_b:o_c])

        @pl.when(first)
        def _():
            xpb_ref[0:CONV_B_PAD, :] = stb_ref[...]
            xpd_ref[0:CONV_D_PAD, :] = std_ref[...]

        @pl.when(jnp.logical_not(first))
        def _():
            xpb_ref[0:CONV_B_PAD, :] = xpb_ref[tq:tq + CONV_B_PAD, :]
            xpd_ref[0:CONV_D_PAD, :] = xpd_ref[tq:tq + CONV_D_PAD, :]

        xpb_ref[CONV_B_PAD:CONV_B_PAD + tq, :] = xb
        bt_ref[...] = xpb_ref[tq:tq + CONV_B_PAD, :]
        n_sh = tq + CONV_B_PAD - SUBLANES
        for r in range(1, SUBLANES):
            xsh_ref[r, 0:n_sh, :] = xpb_ref[r:r + n_sh, :]

        off_b = CONV_B_PAD - (kb_w - 1)
        for blk in range(tq // rb):
            yield
            acc = jnp.broadcast_to(cbb_ref[...], (rb, c_b))
            for kk in range(kb_w):
                a8, r = divmod(off_b + kk, SUBLANES)
                row0 = blk * rb + a8 * SUBLANES
                src = xpb_ref[row0:row0 + rb, :] if r == 0 else xsh_ref[r, row0:row0 + rb, :]
                acc = acc + src * cbw_ref[kk:kk + 1, :]
            yb = _layernorm(acc, lng_ref[...], lnb_ref[...])
            y_ref[blk * rb:(blk + 1) * rb, 0:c_b] = (yb * jax.nn.sigmoid(yb)).astype(BF16)

        yield
        vg = _layernorm(h_ref[:, o_c + c_c:o_d], vng_ref[...], vnb_ref[...])
        if not prompt:
            vg_ref[...] = vg
        vgb = vg.astype(BF16)
        ck = min(tq, GMLP_CHUNK)
        ri = lax.broadcasted_iota(jnp.int32, (ck, ck), 0) // CHUNK
        ci = lax.broadcasted_iota(jnp.int32, (ck, ck), 1) // CHUNK
        lane_grp = lax.broadcasted_iota(jnp.int32, (ck, c_c), 1) // (c_c // C_GROUPS)
        wmix = [jnp.where(ci <= ri, ws_ref[g, 0:ck, 0:ck], 0.0).astype(BF16) for g in range(C_GROUPS)]
        for c in range(tq // ck):
            vc = vgb[c * ck:(c + 1) * ck, :]
            mix = bs_ref[0:ck, :]
            for g in range(C_GROUPS):
                mix = mix + jnp.where(lane_grp == g, _dot(wmix[g], vc), 0.0)
            y_ref[c * ck:(c + 1) * ck, c_b:c_b + c_c] = (
                h_ref[c * ck:(c + 1) * ck, o_c:o_c + c_c] * mix).astype(BF16)

        yield
        xd = h_ref[:, o_d + c_d:o_d + 2 * c_d] * h_ref[:, o_d + 2 * c_d:o_kr]
        xpd_ref[CONV_D_PAD:CONV_D_PAD + tq, :] = xd
        dt_ref[...] = xpd_ref[tq:tq + CONV_D_PAD, :]
        off_d = CONV_D_PAD - (kd_w - 1)
        conv = xd * cdw_ref[kd_w - 1:kd_w, :]
        for kk in range(kd_w - 1):
            conv = conv + xpd_ref[off_d + kk:off_d + kk + tq, :] * cdw_ref[kk:kk + 1, :]
        y_ref[:, c_b + c_c:c_b + c_c + c_d] = (h_ref[:, o_d:o_d + c_d] * conv).astype(BF16)

    bufs = (ha_ref, hb_ref)

    def run(*stages):
        live = list(stages)
        while live:
            for g in list(live):
                if next(g, StopIteration) is StopIteration:
                    live.remove(g)

    @pl.when(i == 0)
    def _():
        run(project(bufs[0]))

    for par in (0, 1):
        @pl.when((i >= 1) & (i < n_tiles) & (i % 2 == par))
        def _(par=par):
            run(project(bufs[par]), consume(bufs[1 - par]))

    @pl.when(i == n_tiles)
    def _():
        run(consume(bufs[(n_tiles - 1) % 2]))


def _mix_in(x, tabs, stb, std, st_layer, w, l, *, b, s, tq, prompt, stacked=None):
    d = x.shape[1]
    depth = w["w_in1"].shape[0]
    q_lora, kv_lora = w["q_norm"].shape[2], w["kv_norm"].shape[2]
    c_b, c_c, c_d = w["conv_b_bias"].shape[2], w["gmlp_vn_g"].shape[2], w["conv_d_w"].shape[2]
    hp = A_HEADS * HEAD_PAD
    rb = min(tq, 64)
    nt = s // tq
    n_tiles = b * nt
    done = lambda i: jnp.maximum(i - 1, 0)
    row_in = pl.BlockSpec((tq, d), lambda i: (jnp.minimum(i, n_tiles - 1), 0))
    row = lambda wd: pl.BlockSpec((tq, wd), lambda i: (done(i), 0))
    per_b = lambda r, wd: pl.BlockSpec((None, r, wd), lambda i: (done(i) // nt, 0, 0))
    if st_layer is None:
        st_spec = per_b
    else:
        st_spec = lambda r, wd: pl.BlockSpec((None, None, r, wd), lambda i: (st_layer, done(i) // nt, 0, 0))
    tab = pl.BlockSpec((tq, LANES), lambda i: (done(i) % nt, 0))
    weights = [w[k] for k in MIX_IN_WEIGHTS]
    sds = jax.ShapeDtypeStruct
    small = [
        (sds((b, CONV_B_PAD, c_b), F32), per_b(CONV_B_PAD, c_b)),
        (sds((b, CONV_D_PAD, c_d), F32), per_b(CONV_D_PAD, c_d)),
    ]
    outs = [(sds((b * s, hp), BF16), row(hp))]
    aliases = {}
    alias_in, alias_specs = [], []
    if prompt:
        vt_spec = pl.BlockSpec((None, None, A_HEADS * V_DIM, tq), lambda i: (done(i) // nt, done(i) % nt, 0, 0))
        stk = lambda wd: pl.BlockSpec((None, None, tq, wd), lambda i: (l, done(i) // nt, done(i) % nt, 0))
        outs += [(sds((b * s, hp), BF16), row(hp)),
                 (sds((b, nt, A_HEADS * V_DIM, tq), BF16), vt_spec),
                 (sds((b * s, c_b + c_c + c_d), BF16), row(c_b + c_c + c_d)),
                 (sds((depth, b, s, kv_lora), F32), stk(kv_lora)),
                 (sds((depth, b, s, ROPE_DIM), F32), stk(ROPE_DIM))]
        outs += small
        if stacked is not None:
            alias_in = list(stacked)
            alias_specs = [pl.BlockSpec(memory_space=pl.ANY)] * 2
            aliases = {N_MIX_IN_INPUTS: 4, N_MIX_IN_INPUTS + 1: 5}
    else:
        outs += [(sds((b * s, c_b + c_c + c_d), BF16), row(c_b + c_c + c_d)),
                 (sds((b * s, kv_lora), F32), row(kv_lora)),
                 (sds((b * s, LANES), F32), row(LANES))]
        outs += small + [(sds((b * s, c_c), F32), row(c_c))]
    out_shape, out_specs = zip(*outs)
    w1 = w["w_in1"].shape[2]
    return pl.pallas_call(
        functools.partial(_mix_in_kernel, prompt=prompt, n_alias=len(alias_in), n_tiles=n_tiles, nt=nt, tq=tq,
                          q_lora=q_lora, kv_lora=kv_lora, c_b=c_b, c_c=c_c, c_d=c_d, kb_w=w["kb_w"],
                          kd_w=w["kd_w"], rb=rb),
        grid=(n_tiles + 1,),
        in_specs=[row_in, tab, tab, tab, st_spec(CONV_B_PAD, c_b), st_spec(CONV_D_PAD, c_d)]
        + [_layer(a, l) for a in weights] + alias_specs,
        out_specs=out_specs,
        out_shape=out_shape,
        input_output_aliases=aliases,
        scratch_shapes=[pltpu.VMEM((tq, w1), F32), pltpu.VMEM((tq, w1), F32),
                        pltpu.VMEM((CONV_B_PAD + tq, c_b), F32),
                        pltpu.VMEM((SUBLANES, CONV_B_PAD + tq, c_b), F32),
                        pltpu.VMEM((CONV_D_PAD + tq, c_d), F32)],
        name="mix_in",
        compiler_params=_params(("arbitrary",)),
    )(x, *tabs, stb, std, *weights, *alias_in)


def _attn_kernel(q_ref, k_ref, vt_ref, o_ref, st0_ref, st1_ref, m_ref, acc_ref, *, tq, nt):
    g = pl.program_id(1)
    kc = lax.broadcasted_iota(jnp.int32, (tq, tq), 0) // CHUNK
    qc = lax.broadcasted_iota(jnp.int32, (tq, tq), 1) // CHUNK
    diag_ok = kc <= qc
    heads = tuple(range(A_HEADS))
    ones = jnp.ones((BF16_ROWS, tq), BF16)

    def score_head(tile, j, st_ref, h):
        k0 = pl.multiple_of(j * tq, tq)
        st_ref[h] = _dot_nt(k_ref[pl.ds(k0, tq), h * HEAD_PAD:(h + 1) * HEAD_PAD],
                            q_ref[tile * tq:(tile + 1) * tq, h * HEAD_PAD:(h + 1) * HEAD_PAD])

    def update_head(tile, j, st_ref, h, masked):
        st = st_ref[h]
        if masked:
            st = jnp.where(diag_ok, st, NEG_BIG)
        m = m_ref[tile, h]
        m_new = jnp.maximum(m, jnp.max(st, axis=0, keepdims=True))
        alpha = jnp.exp2(m - m_new)
        pb = jnp.exp2((st - m_new).astype(BF16))
        vt1 = jnp.concatenate([vt_ref[j, h * V_DIM:(h + 1) * V_DIM, :], ones], axis=0)
        m_ref[tile, h] = m_new
        acc_ref[tile, h] = alpha * acc_ref[tile, h] + _dot(vt1, pb)

    def both(tile_n, j_next, next_ref, tile_c, j, cur_ref, masked=False):
        for h in range(SCORE_LEAD):
            score_head(tile_n, j_next, next_ref, h)
        for h in heads:
            if h + SCORE_LEAD < A_HEADS:
                score_head(tile_n, j_next, next_ref, h + SCORE_LEAD)
            update_head(tile_c, j, cur_ref, h, masked)

    def finish(tile):
        ot = jnp.concatenate([acc_ref[tile, h, 0:V_DIM, :] / acc_ref[tile, h, V_DIM:V_DIM + 1, :] for h in heads],
                             axis=0)
        o_ref[tile * tq:(tile + 1) * tq, :] = ot.T.astype(BF16)

    m_ref[...] = jnp.full(m_ref.shape, NEG_BIG, F32)
    acc_ref[...] = jnp.zeros(acc_ref.shape, F32)
    cur, nxt = st0_ref, st1_ref
    for h in heads:
        score_head(0, 0, cur, h)
    for tile in range(nt):
        full = nt * g + tile

        def pair(p, _, tile=tile, cur=cur, nxt=nxt):
            j = 2 * p
            both(tile, j + 1, nxt, tile, j, cur)
            both(tile, j + 2, cur, tile, j + 1, nxt)
            return 0

        lax.fori_loop(0, (nt // 2) * g + tile // 2, pair, 0)
        if tile % 2 == 1:
            both(tile, full, nxt, tile, full - 1, cur)
            cur, nxt = nxt, cur
        if tile + 1 < nt:
            both(tile + 1, 0, nxt, tile, full, cur, masked=True)
            cur, nxt = nxt, cur
        else:
            for h in heads:
                update_head(tile, full, cur, h, True)
        finish(tile)


def _attn_prompt(q, k, vt, *, tq):
    b, s, hp = q.shape
    nblk, wv = vt.shape[1], vt.shape[2]
    nt = _pick_tile(s // tq, ATTN_TILES_PER_STEP)
    assert nt % 2 == 0
    return pl.pallas_call(
        functools.partial(_attn_kernel, tq=tq, nt=nt),
        grid=(b, s // (nt * tq)),
        in_specs=[
            pl.BlockSpec((None, nt * tq, hp), lambda i, j: (i, j, 0)),
            pl.BlockSpec((None, s, hp), lambda i, j: (i, 0, 0)),
            pl.BlockSpec((None, nblk, wv, tq), lambda i, j: (i, 0, 0, 0)),
        ],
        out_specs=pl.BlockSpec((None, nt * tq, wv), lambda i, j: (i, j, 0)),
        out_shape=jax.ShapeDtypeStruct((b, s, wv), BF16),
        scratch_shapes=[pltpu.VMEM((A_HEADS, tq, tq), F32), pltpu.VMEM((A_HEADS, tq, tq), F32),
                        pltpu.VMEM((nt, A_HEADS, 1, tq), F32),
                        pltpu.VMEM((nt, A_HEADS, V_DIM + BF16_ROWS, tq), F32)],
        name="attn_prompt",
        compiler_params=_params(("parallel", "parallel")),
    )(q, k, vt)


def _attn_sample_kernel(q_ref, clat_ref, ckr_ref, lat_ref, kr_ref, wkt_ref, wv_ref, o_ref, kn_ref, *, t):
    q = q_ref[...]
    qh = [q[:, h * HEAD_PAD:(h + 1) * HEAD_PAD] for h in range(A_HEADS)]
    qabs = jnp.concatenate(
        [_dot(qh[h], wkt_ref[h * HEAD_PAD:(h + 1) * HEAD_PAD, :]) for h in range(A_HEADS)], axis=0)
    qcat = jnp.concatenate([qabs.astype(BF16), jnp.concatenate(qh, axis=0)], axis=1)
    clat = clat_ref[...].astype(BF16)
    kv_lora = clat.shape[1]
    kc = jnp.concatenate([clat, ckr_ref[...]], axis=1)
    kn_ref[...] = jnp.zeros(kn_ref.shape, BF16)
    kn_ref[0:t, :] = jnp.concatenate([lat_ref[...], kr_ref[...]], axis=1).astype(BF16)
    kn = kn_ref[...]
    s1 = _dot_nt(qcat, kc)
    s2 = _dot_nt(qcat, kn)
    s2 = jnp.where(lax.broadcasted_iota(jnp.int32, s2.shape, 1) < t, s2, NEG_BIG)
    m = jnp.maximum(jnp.max(s1, axis=-1, keepdims=True), jnp.max(s2, axis=-1, keepdims=True))
    p1 = jnp.exp2(s1 - m)
    p2 = jnp.exp2(s2 - m)
    l = jnp.sum(p1, axis=-1, keepdims=True) + jnp.sum(p2, axis=-1, keepdims=True)
    olat = ((_dot(p1.astype(BF16), clat) + _dot(p2.astype(BF16), kn[:, 0:kv_lora])) / l).astype(BF16)
    low_half = lax.broadcasted_iota(jnp.int32, (t, LANES), 1) < V_DIM
    for pair in range(A_HEADS // 2):
        wpair = wv_ref[:, pair * LANES:(pair + 1) * LANES]
        lo = _dot(olat[(2 * pair) * t:(2 * pair + 1) * t, :], wpair)
        hi = _dot(olat[(2 * pair + 1) * t:(2 * pair + 2) * t, :], wpair)
        o_ref[:, pair * LANES:(pair + 1) * LANES] = jnp.where(low_half, lo, hi).astype(BF16)


def _attn_sample(q, cache_lat, cache_kr, lat, kr, wkt, wv, l):
    b, t, hp = q.shape
    past, kv_lora = cache_lat.shape[2], cache_lat.shape[3]
    wvw = wv.shape[2]
    per_b = lambda r, w: pl.BlockSpec((None, r, w), lambda i: (i, 0, 0))
    cache = lambda r, w: pl.BlockSpec((None, None, r, w), lambda i: (l, i, 0, 0))
    return pl.pallas_call(
        functools.partial(_attn_sample_kernel, t=t),
        grid=(b,),
        in_specs=[per_b(t, hp), cache(past, kv_lora), cache(past, LANES), per_b(t, kv_lora),
                  per_b(t, LANES), _layer(wkt, l), _layer(wv, l)],
        out_specs=per_b(t, wvw),
        out_shape=jax.ShapeDtypeStruct((b, t, wvw), BF16),
        scratch_shapes=[pltpu.VMEM((LANES, kv_lora + LANES), BF16)],
        name="attn_sample",
        compiler_params=_params(("parallel",)),
    )(q, cache_lat, cache_kr, lat, kr, wkt, wv)


MIX_OUT_WEIGHTS = ("mix_norm_pre", "mix_norm_post", "w_g", "w_br_a", "w_br_b", "w_br_c", "w_br_d", "w_o")


def _mix_out_kernel(xp_ref, ap_ref, yp_ref, xs_ref, as_ref, ys_ref, gpre_ref, gpost_ref, wg_ref, wba_ref, wbb_ref,
                    wbc_ref, wbd_ref, wo_ref, op_ref, os_ref, *, d, c_b, c_c, c_d, n_p):
    def merge(x_ref, a_ref, y_ref, o_ref):
        x = x_ref[...]
        n = _rms(x, gpre_ref[...]).astype(BF16)
        branches = (
            (a_ref[...], wba_ref),
            (y_ref[:, 0:c_b], wbb_ref),
            (y_ref[:, c_b:c_b + c_c], wbc_ref),
            (y_ref[:, c_b + c_c:c_b + c_c + c_d], wbd_ref),
        )
        merged = jnp.zeros(x.shape, F32)
        for k, (br, w_ref) in enumerate(branches):
            gate = jax.nn.sigmoid(_dot(n, wg_ref[:, k * d:(k + 1) * d]))
            merged = merged + gate * _dot(br, w_ref[...])
        out = _dot(merged.astype(BF16), wo_ref[...])
        o_ref[...] = x + _rms(out, gpost_ref[...])

    i = pl.program_id(0)

    @pl.when(i < n_p)
    def _():
        merge(xp_ref, ap_ref, yp_ref, op_ref)

    @pl.when(i == n_p)
    def _():
        merge(xs_ref, as_ref, ys_ref, os_ref)


def _mix_out(xp, ap, yp, xs, a_s, ys, w, l):
    n, d = xp.shape
    c_b, c_c, c_d = w["w_br_b"].shape[1], w["w_br_c"].shape[1], w["w_br_d"].shape[1]
    tm = _pick_tile(n, ROW_TILES)
    assert xs.shape[0] <= tm
    n_p = n // tm
    weights = [w[k] for k in MIX_OUT_WEIGHTS]
    specs_p, specs_s = _two_stream_specs(tm, n_p, (d, ap.shape[1], yp.shape[1]), (xs, a_s, ys))
    return pl.pallas_call(
        functools.partial(_mix_out_kernel, d=d, c_b=c_b, c_c=c_c, c_d=c_d, n_p=n_p),
        grid=(n_p + 1,),
        in_specs=specs_p + specs_s + [_layer(a_, l) for a_ in weights],
        out_specs=(specs_p[0], specs_s[0]),
        out_shape=(jax.ShapeDtypeStruct(xp.shape, F32), jax.ShapeDtypeStruct(xs.shape, F32)),
        name="mix_out",
        compiler_params=_params(("arbitrary",)),
    )(xp, ap, yp, xs, a_s, ys, *weights)


def _split_w_in_kernel(wt_ref, w1_ref, wg_ref, *, o_kv, o_kr, o_d):
    cols = wt_ref.shape[1]
    body = o_d - o_kr
    w1_ref[:, 0:o_kv] = wt_ref[0:o_kv, :].T.astype(BF16)
    w1_ref[:, o_kv:o_kv + body] = wt_ref[o_kr:o_d, :].T.astype(BF16)
    kr_rows = jnp.concatenate([wt_ref[o_kv:o_kr, :], jnp.zeros((LANES - (o_kr - o_kv), cols), F32)], axis=0)
    w1_ref[:, o_kv + body:o_kv + body + LANES] = kr_rows.T.astype(BF16)
    wg_ref[...] = wt_ref[o_d:, :].T.astype(BF16)


def _split_w_in(w_in, o_kv, o_kr, o_d):
    depth, d, d_in = w_in.shape
    w1 = o_d - (o_kr - o_kv) + LANES
    tr = _pick_tile(d, (256, 128))
    assert o_kv % LANES == 0 and (o_d - o_kr) % LANES == 0 and (d_in - o_d) % LANES == 0 and o_kr % SUBLANES == 0
    w_in_t = jnp.swapaxes(w_in, 1, 2)
    return pl.pallas_call(
        functools.partial(_split_w_in_kernel, o_kv=o_kv, o_kr=o_kr, o_d=o_d),
        grid=(depth, d // tr),
        in_specs=[pl.BlockSpec((None, d_in, tr), lambda l, i: (l, 0, i))],
        out_specs=(pl.BlockSpec((None, tr, w1), lambda l, i: (l, i, 0)),
                   pl.BlockSpec((None, tr, d_in - o_d), lambda l, i: (l, i, 0))),
        out_shape=(jax.ShapeDtypeStruct((depth, d, w1), BF16), jax.ShapeDtypeStruct((depth, d, d_in - o_d), BF16)),
        name="split_w_in",
        compiler_params=_params(("parallel", "parallel")),
    )(w_in_t)


def _rope_tabs(pos):
    half = ROPE_DIM // 2
    inv = jnp.exp(-math.log(ROPE_THETA) * jnp.arange(half, dtype=F32) / half)
    ang = pos.astype(F32)[:, None] * inv[None, :]
    cos, sin = jnp.cos(ang), jnp.sin(ang)
    n = pos.shape[0]
    z = lambda w: jnp.zeros((n, w), F32)
    ctab = jnp.concatenate([cos, cos, jnp.ones((n, NOPE_DIM), F32), z(LANES - ROPE_DIM - NOPE_DIM)], axis=1)
    sneg = jnp.concatenate([-sin, z(LANES - half)], axis=1)
    spos = jnp.concatenate([z(half), sin, z(LANES - 2 * half)], axis=1)
    return ctab, sneg, spos


def _pad_rows(w, rows):
    return jnp.pad(w, ((0, 0), (0, rows - w.shape[1]), (0, 0)))


def kernel(x_prompt, x_sample, cache_kv_latent, cache_k_rope, state_conv_b, state_conv_d, ffn1_norm_pre, ffn1_norm_post, ffn1_w_gu, ffn1_w_down, mix_norm_pre, mix_norm_post, w_in, q_norm, w_uq, kv_norm, w_ukv, conv_b_w, conv_b_bias, conv_b_ln_g, conv_b_ln_b, gmlp_vn_g, gmlp_vn_b, gmlp_w_s, gmlp_b_s, conv_d_w, w_br_a, w_br_b, w_br_c, w_br_d, w_o, ffn2_norm_pre, ffn2_norm_post, ffn2_w_gu, ffn2_w_down):
    b, s, d = x_prompt.shape
    bs, t, _ = x_sample.shape
    depth = w_in.shape[0]
    past = cache_kv_latent.shape[2]
    q_lora, kv_lora = q_norm.shape[1], kv_norm.shape[1]
    c_b, c_c, c_d = conv_b_bias.shape[1], gmlp_vn_g.shape[1], conv_d_w.shape[2]
    kb_w, kd_w = conv_b_w.shape[1], conv_d_w.shape[1]
    assert kb_w - 1 <= CONV_B_PAD and kd_w - 1 <= CONV_D_PAD
    assert s % GMLP_CHUNK == 0 and t <= GMLP_CHUNK and t % 16 == 0 and past % CHUNK == 0 and t <= CHUNK
    assert gmlp_w_s.shape[1] == C_GROUPS and w_ukv.shape[2] == A_HEADS * (NOPE_DIM + V_DIM)

    o_q = q_lora
    o_kv = o_q + kv_lora
    o_kr = o_kv + ROPE_DIM
    o_b = o_kr + 2 * c_b
    o_c = o_b + 2 * c_c
    o_d = o_c + 3 * c_d

    vec = lambda p: p[:, None, :]
    uq = w_uq.reshape(depth, q_lora, A_HEADS, NOPE_DIM + ROPE_DIM)
    ukv = w_ukv.reshape(depth, kv_lora, A_HEADS, NOPE_DIM + V_DIM)
    head_zeros = lambda rows, width: jnp.zeros((depth, rows, A_HEADS, width), F32)
    wk_pad = jnp.concatenate(
        [head_zeros(kv_lora, ROPE_DIM), ukv[..., :NOPE_DIM], head_zeros(kv_lora, HEAD_PAD - NOPE_DIM - ROPE_DIM)],
        axis=3).reshape(depth, kv_lora, A_HEADS * HEAD_PAD).astype(BF16)
    w_in1, w_g = _split_w_in(w_in, o_kv, o_kr, o_d)
    w = dict(
        mix_norm_pre=vec(mix_norm_pre), mix_norm_post=vec(mix_norm_post),
        w_in1=w_in1, w_g=w_g,
        q_norm=vec(q_norm),
        wq_pad=jnp.concatenate(
            [uq[..., NOPE_DIM:], uq[..., :NOPE_DIM], head_zeros(q_lora, HEAD_PAD - NOPE_DIM - ROPE_DIM)],
            axis=3).reshape(depth, q_lora, A_HEADS * HEAD_PAD).astype(BF16),
        kv_norm=vec(kv_norm), wk_pad=wk_pad,
        wv=ukv[..., NOPE_DIM:].reshape(depth, kv_lora, A_HEADS * V_DIM).astype(BF16),
        conv_b_w=_pad_rows(conv_b_w, CONV_B_PAD), conv_b_bias=vec(conv_b_bias),
        conv_b_ln_g=vec(conv_b_ln_g), conv_b_ln_b=vec(conv_b_ln_b),
        gmlp_vn_g=vec(gmlp_vn_g), gmlp_vn_b=vec(gmlp_vn_b), gmlp_w_s=gmlp_w_s,
        bs_full=jnp.repeat(jnp.swapaxes(gmlp_b_s, 1, 2), c_c // C_GROUPS, axis=2),
        conv_d_w=_pad_rows(conv_d_w, CONV_D_PAD), kb_w=kb_w, kd_w=kd_w,
        w_br_a=w_br_a.astype(BF16), w_br_b=w_br_b.astype(BF16), w_br_c=w_br_c.astype(BF16),
        w_br_d=w_br_d.astype(BF16), w_o=w_o.astype(BF16))
    wk_t = jnp.swapaxes(wk_pad, 1, 2)
    f1 = (vec(ffn1_norm_pre), vec(ffn1_norm_post), ffn1_w_gu.astype(BF16), ffn1_w_down.astype(BF16))
    f2 = (vec(ffn2_norm_pre), vec(ffn2_norm_post), ffn2_w_gu.astype(BF16), ffn2_w_down.astype(BF16))

    tabs_p = _rope_tabs(jnp.arange(s))
    tabs_s = _rope_tabs(past + jnp.arange(t))
    zero_b = jnp.zeros((b, CONV_B_PAD, c_b), F32)
    zero_d = jnp.zeros((b, CONV_D_PAD, c_d), F32)
    tail_b = CONV_B_PAD - (kb_w - 1)
    tail_d = CONV_D_PAD - (kd_w - 1)
    st_b = jnp.pad(state_conv_b, ((0, 0), (0, 0), (tail_b, 0), (0, 0)))
    st_d = jnp.pad(state_conv_d, ((0, 0), (0, 0), (tail_d, 0), (0, 0)))
    ckr_pad = jnp.pad(cache_k_rope, ((0, 0), (0, 0), (0, 0), (0, LANES - ROPE_DIM))).astype(BF16)
    tq = _pick_tile(s, (256, 128))

    xp = x_prompt.reshape(b * s, d)
    xs = x_sample.reshape(bs * t, d)
    stacked = None
    outs = [[] for _ in range(7)]
    for l in range(depth):
        xp, xs = _ffn(xp, xs, *f1, l)

        qp, kp, vtp, yp, lat_all, kr_all, btp, dtp = _mix_in(
            xp, tabs_p, zero_b, zero_d, None, w, l, b=b, s=s, tq=tq, prompt=True, stacked=stacked)
        stacked = (lat_all, kr_all)
        ap = _attn_prompt(qp.reshape(b, s, -1), kp.reshape(b, s, -1), vtp, tq=tq)

        qs, ys, lats, krs, bts, dts, vgs = _mix_in(
            xs, tabs_s, st_b, st_d, l, w, l, b=bs, s=t, tq=t, prompt=False)
        lats, krs, vgs = (v.reshape(bs, t, -1) for v in (lats, krs, vgs))
        a_s = _attn_sample(qs.reshape(bs, t, -1), cache_kv_latent, ckr_pad, lats, krs, wk_t, w["wv"], l)

        xp, xs = _mix_out(xp, ap.reshape(b * s, -1), yp, xs, a_s.reshape(bs * t, -1), ys, w, l)
        xp, xs = _ffn(xp, xs, *f2, l)

        for lst, val in zip(outs, (btp[:, tail_b:], dtp[:, tail_d:], lats, krs[..., :ROPE_DIM],
                                   bts[:, tail_b:], vgs, dts[:, tail_d:])):
            lst.append(val)
    cb_p, cd_p, lat_s, kr_s, cb_s, vc_s, cd_s = (jnp.stack(o) for o in outs)
    return (xp.reshape(b, s, d), xs.reshape(bs, t, d), stacked[0], stacked[1], cb_p, cd_p,
            lat_s, kr_s, cb_s, vc_s, cd_s)
```

```python
import functools
import math

import jax
import jax.numpy as jnp
from jax import lax
from jax.experimental import pallas as pl
from jax.experimental.pallas import tpu as pltpu

F32 = jnp.float32
BF16 = jnp.bfloat16

CHUNK = 64
A_HEADS = 8
NOPE_DIM = 64
ROPE_DIM = 32
V_DIM = 64
ROPE_THETA = 10000.0
SM_SCALE = (NOPE_DIM + ROPE_DIM) ** -0.5
Q_SCALE = SM_SCALE * math.log2(math.e)
C_GROUPS = 4
GMLP_CHUNK = 128
N_BRANCH = 4
EPS = 1e-6

LANES = 128
SUBLANES = 8
BF16_ROWS = 16
HEAD_PAD = 128
CONV_B_PAD = 32
CONV_D_PAD = 8
VMEM_LIMIT_BYTES = 56 * 1024 * 1024
NEG_BIG = -1e30
SCORE_LEAD = 2
ATTN_TILES_PER_STEP = (4, 2)
ROW_TILES = (1024, 512, 256, 128, 64, 32, 16, 8)


def _pick_tile(n, candidates):
    for c in candidates:
        if n % c == 0:
            return c
    return n


def _layer(arr, l):
    nd = arr.ndim - 1
    return pl.BlockSpec((None,) + arr.shape[1:], lambda *_: (l,) + (0,) * nd, pipeline_mode=pl.Buffered(1))


def _whole(arr):
    nd = arr.ndim
    return pl.BlockSpec(arr.shape, lambda *_: (0,) * nd, pipeline_mode=pl.Buffered(1))


def _rms(x, g):
    ms = jnp.mean(x * x, axis=-1, keepdims=True)
    return x * lax.rsqrt(ms + EPS) * g


def _layernorm(x, g, b):
    mu = jnp.mean(x, axis=-1, keepdims=True)
    xc = x - mu
    var = jnp.mean(xc * xc, axis=-1, keepdims=True)
    return xc * lax.rsqrt(var + EPS) * g + b


def _dot(a, b):
    return jnp.dot(a, b, preferred_element_type=F32)


def _dot_nt(a, b):
    return lax.dot_general(a, b, (((1,), (1,)), ((), ())), preferred_element_type=F32)


def _params(sem):
    return pltpu.CompilerParams(dimension_semantics=sem, vmem_limit_bytes=VMEM_LIMIT_BYTES)


def _ffn_kernel(xp_ref, xs_ref, gpre_ref, gpost_ref, wgu_ref, wd_ref, op_ref, os_ref, act_ref, *, d_ff, tf, n_p):
    def ffn(x_ref, o_ref):
        rows = x_ref.shape[0]
        x = x_ref[...]
        n = _rms(x, gpre_ref[...]).astype(BF16)
        for c in range(d_ff // tf):
            gate = _dot(n, wgu_ref[:, c * tf:(c + 1) * tf])
            up = _dot(n, wgu_ref[:, d_ff + c * tf:d_ff + (c + 1) * tf])
            act_ref[0:rows, c * tf:(c + 1) * tf] = (gate * jax.nn.sigmoid(gate) * up).astype(BF16)
        y = _dot(act_ref[0:rows, :], wd_ref[...])
        o_ref[...] = x + 0.5 * _rms(y, gpost_ref[...])

    i = pl.program_id(0)

    @pl.when(i < n_p)
    def _():
        ffn(xp_ref, op_ref)

    @pl.when(i == n_p)
    def _():
        ffn(xs_ref, os_ref)


def _two_stream_specs(tm, n_p, widths_p, arrays_s):
    prompt = [pl.BlockSpec((tm, wd), lambda i: (jnp.minimum(i, n_p - 1), 0)) for wd in widths_p]
    sample = [pl.BlockSpec(a.shape, lambda i: (0, 0)) for a in arrays_s]
    return prompt, sample


def _ffn(xp, xs, gpre, gpost, wgu, wd, l):
    n, d = xp.shape
    d_ff = wd.shape[1]
    tm = _pick_tile(n, ROW_TILES)
    assert xs.shape[0] <= tm
    n_p = n // tm
    tf = _pick_tile(d_ff, (256, 128))
    weights = (gpre, gpost, wgu, wd)
    (spec_p,), (spec_s,) = _two_stream_specs(tm, n_p, (d,), (xs,))
    return pl.pallas_call(
        functools.partial(_ffn_kernel, d_ff=d_ff, tf=tf, n_p=n_p),
        grid=(n_p + 1,),
        in_specs=[spec_p, spec_s] + [_layer(w, l) for w in weights],
        out_specs=(spec_p, spec_s),
        out_shape=(jax.ShapeDtypeStruct(xp.shape, F32), jax.ShapeDtypeStruct(xs.shape, F32)),
        scratch_shapes=[pltpu.VMEM((tm, d_ff), BF16)],
        name="ffn",
        compiler_params=_params(("arbitrary",)),
    )(xp, xs, *weights)


def _rope128(a, c, sneg, spos):
    return (a * c + pltpu.roll(a, LANES - ROPE_DIM // 2, 1) * sneg
            + pltpu.roll(a, ROPE_DIM // 2, 1) * spos)


MIX_IN_WEIGHTS = ("mix_norm_pre", "w_in1", "q_norm", "wq_pad", "kv_norm", "wk_pad", "wv", "conv_b_w",
                  "conv_b_bias", "conv_b_ln_g", "conv_b_ln_b", "gmlp_vn_g", "gmlp_vn_b", "gmlp_w_s", "bs_full",
                  "conv_d_w")
N_MIX_IN_INPUTS = 6 + len(MIX_IN_WEIGHTS)


def _mix_in_kernel(*refs, prompt, n_alias, n_tiles, nt, tq, q_lora, kv_lora, c_b, c_c, c_d, kb_w, kd_w, rb):
    (x_ref, c_ref, sneg_ref, spos_ref, stb_ref, std_ref,
     gpre_ref, win_ref, qn_ref, wq_ref, kvn_ref, wk_ref, wv_ref,
     cbw_ref, cbb_ref, lng_ref, lnb_ref, vng_ref, vnb_ref, ws_ref, bs_ref, cdw_ref) = refs[:N_MIX_IN_INPUTS]
    rest = refs[N_MIX_IN_INPUTS + n_alias:]
    if prompt:
        q_ref, k_ref, vt_ref, y_ref, lat_ref, kr_ref, bt_ref, dt_ref, ha_ref, hb_ref, xpb_ref, xsh_ref, xpd_ref = rest
    else:
        q_ref, y_ref, lat_ref, kr_ref, bt_ref, dt_ref, vg_ref, ha_ref, hb_ref, xpb_ref, xsh_ref, xpd_ref = rest
    i = pl.program_id(0)

    o_kv = q_lora
    o_b = o_kv + kv_lora
    o_c = o_b + 2 * c_b
    o_d = o_c + 2 * c_c
    o_kr = o_d + 3 * c_d
    col_groups = (0, o_kv, o_b, o_c, o_d, o_kr, o_kr + LANES)

    def project(h_ref):
        n = _rms(x_ref[...], gpre_ref[...]).astype(BF16)
        for c0, c1 in zip(col_groups[:-1], col_groups[1:]):
            h_ref[:, c0:c1] = _dot(n, win_ref[:, c0:c1])
            yield

    def consume(h_ref):
        first = (i - 1) % nt == 0
        ctab, sneg, spos = c_ref[...], sneg_ref[...], spos_ref[...]

        qlat = _rms(h_ref[:, 0:o_kv], qn_ref[...]).astype(BF16)
        qa = _dot(qlat, wq_ref[...])
        for h in range(A_HEADS):
            qh = _rope128(qa[:, h * HEAD_PAD:(h + 1) * HEAD_PAD], ctab, sneg, spos) * Q_SCALE
            q_ref[:, h * HEAD_PAD:(h + 1) * HEAD_PAD] = qh.astype(BF16)

        yield
        lat = _rms(h_ref[:, o_kv:o_b], kvn_ref[...])
        lat_ref[...] = lat
        latb = lat.astype(BF16)
        kr = _rope128(h_ref[:, o_kr:o_kr + LANES], ctab, sneg, spos)
        if prompt:
            kr_ref[...] = kr[:, 0:ROPE_DIM]
            kn = _dot(latb, wk_ref[...])
            for h in range(A_HEADS):
                k_ref[:, h * HEAD_PAD:(h + 1) * HEAD_PAD] = (
                    kn[:, h * HEAD_PAD:(h + 1) * HEAD_PAD] + kr).astype(BF16)
            vt_ref[...] = _dot(latb, wv_ref[...]).T.astype(BF16)
        else:
            kr_ref[...] = kr

        yield
        xb = h_ref[:, o_b:o_b + c_b] * jax.nn.sigmoid(h_ref[:, o_b + c_b:o_c])

        @pl.when(first)
        def _():
            xpb_ref[0:CONV_B_PAD, :] = stb_ref[...]
            xpd_ref[0:CONV_D_PAD, :] = std_ref[...]

        @pl.when(jnp.logical_not(first))
        def _():
            xpb_ref[0:CONV_B_PAD, :] = xpb_ref[tq:tq + CONV_B_PAD, :]
            xpd_ref[0:CONV_D_PAD, :] = xpd_ref[tq:tq + CONV_D_PAD, :]

        xpb_ref[CONV_B_PAD:CONV_B_PAD + tq, :] = xb
        bt_ref[...] = xpb_ref[tq:tq + CONV_B_PAD, :]
        n_sh = tq + CONV_B_PAD - SUBLANES
        for r in range(1, SUBLANES):
            xsh_ref[r, 0:n_sh, :] = xpb_ref[r:r + n_sh, :]

        off_b = CONV_B_PAD - (kb_w - 1)
        for blk in range(tq // rb):
            yield
            acc = jnp.broadcast_to(cbb_ref[...], (rb, c_b))
            for kk in range(kb_w):
                a8, r = divmod(off_b + kk, SUBLANES)
                row0 = blk * rb + a8 * SUBLANES
                src = xpb_ref[row0:row0 + rb, :] if r == 0 else xsh_ref[r, row0:row0 + rb, :]
                acc = acc + src * cbw_ref[kk:kk + 1, :]
            yb = _layernorm(acc, lng_ref[...], lnb_ref[...])
            y_ref[blk * rb:(blk + 1) * rb, 0:c_b] = (yb * jax.nn.sigmoid(yb)).astype(BF16)

        yield
        vg = _layernorm(h_ref[:, o_c + c_c:o_d], vng_ref[...], vnb_ref[...])
        if not prompt:
            vg_ref[...] = vg
        vgb = vg.astype(BF16)
        ck = min(tq, GMLP_CHUNK)
        ri = lax.broadcasted_iota(jnp.int32, (ck, ck), 0) // CHUNK
        ci = lax.broadcasted_iota(jnp.int32, (ck, ck), 1) // CHUNK
        lane_grp = lax.broadcasted_iota(jnp.int32, (ck, c_c), 1) // (c_c // C_GROUPS)
        wmix = [jnp.where(ci <= ri, ws_ref[g, 0:ck, 0:ck], 0.0).astype(BF16) for g in range(C_GROUPS)]
        for c in range(tq // ck):
            vc = vgb[c * ck:(c + 1) * ck, :]
            mix = bs_ref[0:ck, :]
            for g in range(C_GROUPS):
                mix = mix + jnp.where(lane_grp == g, _dot(wmix[g], vc), 0.0)
            y_ref[c * ck:(c + 1) * ck, c_b:c_b + c_c] = (
                h_ref[c * ck:(c + 1) * ck, o_c:o_c + c_c] * mix).astype(BF16)

        yield
        xd = h_ref[:, o_d + c_d:o_d + 2 * c_d] * h_ref[:, o_d + 2 * c_d:o_kr]
        xpd_ref[CONV_D_PAD:CONV_D_PAD + tq, :] = xd
        dt_ref[...] = xpd_ref[tq:tq + CONV_D_PAD, :]
        off_d = CONV_D_PAD - (kd_w - 1)
        conv = xd * cdw_ref[kd_w - 1:kd_w, :]
        for kk in range(kd_w - 1):
            conv = conv + xpd_ref[off_d + kk:off_d + kk + tq, :] * cdw_ref[kk:kk + 1, :]
        y_ref[:, c_b + c_c:c_b + c_c + c_d] = (h_ref[:, o_d:o_d + c_d] * conv).astype(BF16)

    bufs = (ha_ref, hb_ref)

    def run(*stages):
        live = list(stages)
        while live:
            for g in list(live):
                if next(g, StopIteration) is StopIteration:
                    live.remove(g)

    @pl.when(i == 0)
    def _():
        run(project(bufs[0]))

    for par in (0, 1):
        @pl.when((i >= 1) & (i < n_tiles) & (i % 2 == par))
        def _(par=par):
            run(project(bufs[par]), consume(bufs[1 - par]))

    @pl.when(i == n_tiles)
    def _():
        run(consume(bufs[(n_tiles - 1) % 2]))


def _mix_in(x, tabs, stb, std, st_layer, w, l, *, b, s, tq, prompt, stacked=None):
    d = x.shape[1]
    depth = w["w_in1"].shape[0]
    q_lora, kv_lora = w["q_norm"].shape[2], w["kv_norm"].shape[2]
    c_b, c_c, c_d = w["conv_b_bias"].shape[2], w["gmlp_vn_g"].shape[2], w["conv_d_w"].shape[2]
    hp = A_HEADS * HEAD_PAD
    rb = min(tq, 64)
    nt = s // tq
    n_tiles = b * nt
    done = lambda i: jnp.maximum(i - 1, 0)
    row_in = pl.BlockSpec((tq, d), lambda i: (jnp.minimum(i, n_tiles - 1), 0))
    row = lambda wd: pl.BlockSpec((tq, wd), lambda i: (done(i), 0))
    per_b = lambda r, wd: pl.BlockSpec((None, r, wd), lambda i: (done(i) // nt, 0, 0))
    if st_layer is None:
        st_spec = per_b
    else:
        st_spec = lambda r, wd: pl.BlockSpec((None, None, r, wd), lambda i: (st_layer, done(i) // nt, 0, 0))
    tab = pl.BlockSpec((tq, LANES), lambda i: (done(i) % nt, 0))
    weights = [w[k] for k in MIX_IN_WEIGHTS]
    sds = jax.ShapeDtypeStruct
    small = [
        (sds((b, CONV_B_PAD, c_b), F32), per_b(CONV_B_PAD, c_b)),
        (sds((b, CONV_D_PAD, c_d), F32), per_b(CONV_D_PAD, c_d)),
    ]
    outs = [(sds((b * s, hp), BF16), row(hp))]
    aliases = {}
    alias_in, alias_specs = [], []
    if prompt:
        vt_spec = pl.BlockSpec((None, None, A_HEADS * V_DIM, tq), lambda i: (done(i) // nt, done(i) % nt, 0, 0))
        stk = lambda wd: pl.BlockSpec((None, None, tq, wd), lambda i: (l, done(i) // nt, done(i) % nt, 0))
        outs += [(sds((b * s, hp), BF16), row(hp)),
                 (sds((b, nt, A_HEADS * V_DIM, tq), BF16), vt_spec),
                 (sds((b * s, c_b + c_c + c_d), BF16), row(c_b + c_c + c_d)),
                 (sds((depth, b, s, kv_lora), F32), stk(kv_lora)),
                 (sds((depth, b, s, ROPE_DIM), F32), stk(ROPE_DIM))]
        outs += small
        if stacked is not None:
            alias_in = list(stacked)
            alias_specs = [pl.BlockSpec(memory_space=pl.ANY)] * 2
            aliases = {N_MIX_IN_INPUTS: 4, N_MIX_IN_INPUTS + 1: 5}
    else:
        outs += [(sds((b * s, c_b + c_c + c_d), BF16), row(c_b + c_c + c_d)),
                 (sds((b * s, kv_lora), F32), row(kv_lora)),
                 (sds((b * s, LANES), F32), row(LANES))]
        outs += small + [(sds((b * s, c_c), F32), row(c_c))]
    out_shape, out_specs = zip(*outs)
    w1 = w["w_in1"].shape[2]
    return pl.pallas_call(
        functools.partial(_mix_in_kernel, prompt=prompt, n_alias=len(alias_in), n_tiles=n_tiles, nt=nt, tq=tq,
                          q_lora=q_lora, kv_lora=kv_lora, c_b=c_b, c_c=c_c, c_d=c_d, kb_w=w["kb_w"],
                          kd_w=w["kd_w"], rb=rb),
        grid=(n_tiles + 1,),
        in_specs=[row_in, tab, tab, tab, st_spec(CONV_B_PAD, c_b), st_spec(CONV_D_PAD, c_d)]
        + [_layer(a, l) for a in weights] + alias_specs,
        out_specs=out_specs,
        out_shape=out_shape,
        input_output_aliases=aliases,
        scratch_shapes=[pltpu.VMEM((tq, w1), F32), pltpu.VMEM((tq, w1), F32),
                        pltpu.VMEM((CONV_B_PAD + tq, c_b), F32),
                        pltpu.VMEM((SUBLANES, CONV_B_PAD + tq, c_b), F32),
                        pltpu.VMEM((CONV_D_PAD + tq, c_d), F32)],
        name="mix_in",
        compiler_params=_params(("arbitrary",)),
    )(x, *tabs, stb, std, *weights, *alias_in)


def _attn_kernel(q_ref, k_ref, vt_ref, o_ref, st0_ref, st1_ref, m_ref, acc_ref, *, tq, nt):
    g = pl.program_id(1)
    kc = lax.broadcasted_iota(jnp.int32, (tq, tq), 0) // CHUNK
    qc = lax.broadcasted_iota(jnp.int32, (tq, tq), 1) // CHUNK
    diag_ok = kc <= qc
    heads = tuple(range(A_HEADS))
    ones = jnp.ones((BF16_ROWS, tq), BF16)

    def score_head(tile, j, st_ref, h):
        k0 = pl.multiple_of(j * tq, tq)
        st_ref[h] = _dot_nt(k_ref[pl.ds(k0, tq), h * HEAD_PAD:(h + 1) * HEAD_PAD],
                            q_ref[tile * tq:(tile + 1) * tq, h * HEAD_PAD:(h + 1) * HEAD_PAD])

    def update_head(tile, j, st_ref, h, masked):
        st = st_ref[h]
        if masked:
            st = jnp.where(diag_ok, st, NEG_BIG)
        m = m_ref[tile, h]
        m_new = jnp.maximum(m, jnp.max(st, axis=0, keepdims=True))
        alpha = jnp.exp2(m - m_new)
        pb = jnp.exp2((st - m_new).astype(BF16))
        vt1 = jnp.concatenate([vt_ref[j, h * V_DIM:(h + 1) * V_DIM, :], ones], axis=0)
        m_ref[tile, h] = m_new
        acc_ref[tile, h] = alpha * acc_ref[tile, h] + _dot(vt1, pb)

    def both(tile_n, j_next, next_ref, tile_c, j, cur_ref, masked=False):
        for h in range(SCORE_LEAD):
            score_head(tile_n, j_next, next_ref, h)
        for h in heads:
            if h + SCORE_LEAD < A_HEADS:
                score_head(tile_n, j_next, next_ref, h + SCORE_LEAD)
            update_head(tile_c, j, cur_ref, h, masked)

    def finish(tile):
        ot = jnp.concatenate([acc_ref[tile, h, 0:V_DIM, :] / acc_ref[tile, h, V_DIM:V_DIM + 1, :] for h in heads],
                             axis=0)
        o_ref[tile * tq:(tile + 1) * tq, :] = ot.T.astype(BF16)

    m_ref[...] = jnp.full(m_ref.shape, NEG_BIG, F32)
    acc_ref[...] = jnp.zeros(acc_ref.shape, F32)
    cur, nxt = st0_ref, st1_ref
    for h in heads:
        score_head(0, 0, cur, h)
    for tile in range(nt):
        full = nt * g + tile

        def pair(p, _, tile=tile, cur=cur, nxt=nxt):
            j = 2 * p
            both(tile, j + 1, nxt, tile, j, cur)
            both(tile, j + 2, cur, tile, j + 1, nxt)
            return 0

        lax.fori_loop(0, (nt // 2) * g + tile // 2, pair, 0)
        if tile % 2 == 1:
            both(tile, full, nxt, tile, full - 1, cur)
            cur, nxt = nxt, cur
        if tile + 1 < nt:
            both(tile + 1, 0, nxt, tile, full, cur, masked=True)
            cur, nxt = nxt, cur
        else:
            for h in heads:
                update_head(tile, full, cur, h, True)
        finish(tile)


def _attn_prompt(q, k, vt, *, tq):
    b, s, hp = q.shape
    nblk, wv = vt.shape[1], vt.shape[2]
    nt = _pick_tile(s // tq, ATTN_TILES_PER_STEP)
    assert nt % 2 == 0
    return pl.pallas_call(
        functools.partial(_attn_kernel, tq=tq, nt=nt),
        grid=(b, s // (nt * tq)),
        in_specs=[
            pl.BlockSpec((None, nt * tq, hp), lambda i, j: (i, j, 0)),
            pl.BlockSpec((None, s, hp), lambda i, j: (i, 0, 0)),
            pl.BlockSpec((None, nblk, wv, tq), lambda i, j: (i, 0, 0, 0)),
        ],
        out_specs=pl.BlockSpec((None, nt * tq, wv), lambda i, j: (i, j, 0)),
        out_shape=jax.ShapeDtypeStruct((b, s, wv), BF16),
        scratch_shapes=[pltpu.VMEM((A_HEADS, tq, tq), F32), pltpu.VMEM((A_HEADS, tq, tq), F32),
                        pltpu.VMEM((nt, A_HEADS, 1, tq), F32),
                        pltpu.VMEM((nt, A_HEADS, V_DIM + BF16_ROWS, tq), F32)],
        name="attn_prompt",
        compiler_params=_params(("parallel", "parallel")),
    )(q, k, vt)


def _attn_sample_kernel(q_ref, clat_ref, ckr_ref, lat_ref, kr_ref, wkt_ref, wv_ref, o_ref, kn_ref, *, t):
    q = q_ref[...]
    qh = [q[:, h * HEAD_PAD:(h + 1) * HEAD_PAD] for h in range(A_HEADS)]
    qabs = jnp.concatenate(
        [_dot(qh[h], wkt_ref[h * HEAD_PAD:(h + 1) * HEAD_PAD, :]) for h in range(A_HEADS)], axis=0)
    qcat = jnp.concatenate([qabs.astype(BF16), jnp.concatenate(qh, axis=0)], axis=1)
    clat = clat_ref[...].astype(BF16)
    kv_lora = clat.shape[1]
    kc = jnp.concatenate([clat, ckr_ref[...]], axis=1)
    kn_ref[...] = jnp.zeros(kn_ref.shape, BF16)
    kn_ref[0:t, :] = jnp.concatenate([lat_ref[...], kr_ref[...]], axis=1).astype(BF16)
    kn = kn_ref[...]
    s1 = _dot_nt(qcat, kc)
    s2 = _dot_nt(qcat, kn)
    s2 = jnp.where(lax.broadcasted_iota(jnp.int32, s2.shape, 1) < t, s2, NEG_BIG)
    m = jnp.maximum(jnp.max(s1, axis=-1, keepdims=True), jnp.max(s2, axis=-1, keepdims=True))
    p1 = jnp.exp2(s1 - m)
    p2 = jnp.exp2(s2 - m)
    l = jnp.sum(p1, axis=-1, keepdims=True) + jnp.sum(p2, axis=-1, keepdims=True)
    olat = ((_dot(p1.astype(BF16), clat) + _dot(p2.astype(BF16), kn[:, 0:kv_lora])) / l).astype(BF16)
    low_half = lax.broadcasted_iota(jnp.int32, (t, LANES), 1) < V_DIM
    for pair in range(A_HEADS // 2):
        wpair = wv_ref[:, pair * LANES:(pair + 1) * LANES]
        lo = _dot(olat[(2 * pair) * t:(2 * pair + 1) * t, :], wpair)
        hi = _dot(olat[(2 * pair + 1) * t:(2 * pair + 2) * t, :], wpair)
        o_ref[:, pair * LANES:(pair + 1) * LANES] = jnp.where(low_half, lo, hi).astype(BF16)


def _attn_sample(q, cache_lat, cache_kr, lat, kr, wkt, wv, l):
    b, t, hp = q.shape
    past, kv_lora = cache_lat.shape[2], cache_lat.shape[3]
    wvw = wv.shape[2]
    per_b = lambda r, w: pl.BlockSpec((None, r, w), lambda i: (i, 0, 0))
    cache = lambda r, w: pl.BlockSpec((None, None, r, w), lambda i: (l, i, 0, 0))
    return pl.pallas_call(
        functools.partial(_attn_sample_kernel, t=t),
        grid=(b,),
        in_specs=[per_b(t, hp), cache(past, kv_lora), cache(past, LANES), per_b(t, kv_lora),
                  per_b(t, LANES), _layer(wkt, l), _layer(wv, l)],
        out_specs=per_b(t, wvw),
        out_shape=jax.ShapeDtypeStruct((b, t, wvw), BF16),
        scratch_shapes=[pltpu.VMEM((LANES, kv_lora + LANES), BF16)],
        name="attn_sample",
        compiler_params=_params(("parallel",)),
    )(q, cache_lat, cache_kr, lat, kr, wkt, wv)


MIX_OUT_WEIGHTS = ("mix_norm_pre", "mix_norm_post", "w_g", "w_br_a", "w_br_b", "w_br_c", "w_br_d", "w_o")


def _mix_out_kernel(xp_ref, ap_ref, yp_ref, xs_ref, as_ref, ys_ref, gpre_ref, gpost_ref, wg_ref, wba_ref, wbb_ref,
                    wbc_ref, wbd_ref, wo_ref, op_ref, os_ref, *, d, c_b, c_c, c_d, n_p):
    def merge(x_ref, a_ref, y_ref, o_ref):
        x = x_ref[...]
        n = _rms(x, gpre_ref[...]).astype(BF16)
        branches = (
            (a_ref[...], wba_ref),
            (y_ref[:, 0:c_b], wbb_ref),
            (y_ref[:, c_b:c_b + c_c], wbc_ref),
            (y_ref[:, c_b + c_c:c_b + c_c + c_d], wbd_ref),
        )
        merged = jnp.zeros(x.shape, F32)
        gate_pre = _dot(n, wg_ref[:, 0:d])
        for k, (br, w_ref) in enumerate(branches):
            nxt = _dot(n, wg_ref[:, (k + 1) * d:(k + 2) * d]) if k + 1 < len(branches) else None
            merged = merged + jax.nn.sigmoid(gate_pre) * _dot(br, w_ref[...])
            gate_pre = nxt
        out = _dot(merged.astype(BF16), wo_ref[...])
        o_ref[...] = x + _rms(out, gpost_ref[...])

    i = pl.program_id(0)

    @pl.when(i < n_p)
    def _():
        merge(xp_ref, ap_ref, yp_ref, op_ref)

    @pl.when(i == n_p)
    def _():
        merge(xs_ref, as_ref, ys_ref, os_ref)


def _mix_out(xp, ap, yp, xs, a_s, ys, w, l):
    n, d = xp.shape
    c_b, c_c, c_d = w["w_br_b"].shape[1], w["w_br_c"].shape[1], w["w_br_d"].shape[1]
    tm = _pick_tile(n, ROW_TILES)
    assert xs.shape[0] <= tm
    n_p = n // tm
    weights = [w[k] for k in MIX_OUT_WEIGHTS]
    specs_p, specs_s = _two_stream_specs(tm, n_p, (d, ap.shape[1], yp.shape[1]), (xs, a_s, ys))
    return pl.pallas_call(
        functools.partial(_mix_out_kernel, d=d, c_b=c_b, c_c=c_c, c_d=c_d, n_p=n_p),
        grid=(n_p + 1,),
        in_specs=specs_p + specs_s + [_layer(a_, l) for a_ in weights],
        out_specs=(specs_p[0], specs_s[0]),
        out_shape=(jax.ShapeDtypeStruct(xp.shape, F32), jax.ShapeDtypeStruct(xs.shape, F32)),
        name="mix_out",
        compiler_params=_params(("arbitrary",)),
    )(xp, ap, yp, xs, a_s, ys, *weights)


def _split_w_in_kernel(wt_ref, w1_ref, wg_ref, *, o_kv, o_kr, o_d):
    cols = wt_ref.shape[1]
    body = o_d - o_kr
    w1_ref[:, 0:o_kv] = wt_ref[0:o_kv, :].T.astype(BF16)
    w1_ref[:, o_kv:o_kv + body] = wt_ref[o_kr:o_d, :].T.astype(BF16)
    kr_rows = jnp.concatenate([wt_ref[o_kv:o_kr, :], jnp.zeros((LANES - (o_kr - o_kv), cols), F32)], axis=0)
    w1_ref[:, o_kv + body:o_kv + body + LANES] = kr_rows.T.astype(BF16)
    wg_ref[...] = wt_ref[o_d:, :].T.astype(BF16)


def _split_w_in(w_in, o_kv, o_kr, o_d):
    depth, d, d_in = w_in.shape
    w1 = o_d - (o_kr - o_kv) + LANES
    tr = _pick_tile(d, (256, 128))
    assert o_kv % LANES == 0 and (o_d - o_kr) % LANES == 0 and (d_in - o_d) % LANES == 0 and o_kr % SUBLANES == 0
    w_in_t = jnp.swapaxes(w_in, 1, 2)
    return pl.pallas_call(
        functools.partial(_split_w_in_kernel, o_kv=o_kv, o_kr=o_kr, o_d=o_d),
        grid=(depth, d // tr),
        in_specs=[pl.BlockSpec((None, d_in, tr), lambda l, i: (l, 0, i))],
        out_specs=(pl.BlockSpec((None, tr, w1), lambda l, i: (l, i, 0)),
                   pl.BlockSpec((None, tr, d_in - o_d), lambda l, i: (l, i, 0))),
        out_shape=(jax.ShapeDtypeStruct((depth, d, w1), BF16), jax.ShapeDtypeStruct((depth, d, d_in - o_d), BF16)),
        name="split_w_in",
        compiler_params=_params(("parallel", "parallel")),
    )(w_in_t)


def _rope_tabs(pos):
    half = ROPE_DIM // 2
    inv = jnp.exp(-math.log(ROPE_THETA) * jnp.arange(half, dtype=F32) / half)
    ang = pos.astype(F32)[:, None] * inv[None, :]
    cos, sin = jnp.cos(ang), jnp.sin(ang)
    n = pos.shape[0]
    z = lambda w: jnp.zeros((n, w), F32)
    ctab = jnp.concatenate([cos, cos, jnp.ones((n, NOPE_DIM), F32), z(LANES - ROPE_DIM - NOPE_DIM)], axis=1)
    sneg = jnp.concatenate([-sin, z(LANES - half)], axis=1)
    spos = jnp.concatenate([z(half), sin, z(LANES - 2 * half)], axis=1)
    return ctab, sneg, spos


def _pad_rows(w, rows):
    return jnp.pad(w, ((0, 0), (0, rows - w.shape[1]), (0, 0)))


def kernel(x_prompt, x_sample, cache_kv_latent, cache_k_rope, state_conv_b, state_conv_d, ffn1_norm_pre, ffn1_norm_post, ffn1_w_gu, ffn1_w_down, mix_norm_pre, mix_norm_post, w_in, q_norm, w_uq, kv_norm, w_ukv, conv_b_w, conv_b_bias, conv_b_ln_g, conv_b_ln_b, gmlp_vn_g, gmlp_vn_b, gmlp_w_s, gmlp_b_s, conv_d_w, w_br_a, w_br_b, w_br_c, w_br_d, w_o, ffn2_norm_pre, ffn2_norm_post, ffn2_w_gu, ffn2_w_down):
    b, s, d = x_prompt.shape
    bs, t, _ = x_sample.shape
    depth = w_in.shape[0]
    past = cache_kv_latent.shape[2]
    q_lora, kv_lora = q_norm.shape[1], kv_norm.shape[1]
    c_b, c_c, c_d = conv_b_bias.shape[1], gmlp_vn_g.shape[1], conv_d_w.shape[2]
    kb_w, kd_w = conv_b_w.shape[1], conv_d_w.shape[1]
    assert kb_w - 1 <= CONV_B_PAD and kd_w - 1 <= CONV_D_PAD
    assert s % GMLP_CHUNK == 0 and t <= GMLP_CHUNK and t % 16 == 0 and past % CHUNK == 0 and t <= CHUNK
    assert gmlp_w_s.shape[1] == C_GROUPS and w_ukv.shape[2] == A_HEADS * (NOPE_DIM + V_DIM)

    o_q = q_lora
    o_kv = o_q + kv_lora
    o_kr = o_kv + ROPE_DIM
    o_b = o_kr + 2 * c_b
    o_c = o_b + 2 * c_c
    o_d = o_c + 3 * c_d

    vec = lambda p: p[:, None, :]
    uq = w_uq.reshape(depth, q_lora, A_HEADS, NOPE_DIM + ROPE_DIM)
    ukv = w_ukv.reshape(depth, kv_lora, A_HEADS, NOPE_DIM + V_DIM)
    head_zeros = lambda rows, width: jnp.zeros((depth, rows, A_HEADS, width), F32)
    wk_pad = jnp.concatenate(
        [head_zeros(kv_lora, ROPE_DIM), ukv[..., :NOPE_DIM], head_zeros(kv_lora, HEAD_PAD - NOPE_DIM - ROPE_DIM)],
        axis=3).reshape(depth, kv_lora, A_HEADS * HEAD_PAD).astype(BF16)
    w_in1, w_g = _split_w_in(w_in, o_kv, o_kr, o_d)
    w = dict(
        mix_norm_pre=vec(mix_norm_pre), mix_norm_post=vec(mix_norm_post),
        w_in1=w_in1, w_g=w_g,
        q_norm=vec(q_norm),
        wq_pad=jnp.concatenate(
            [uq[..., NOPE_DIM:], uq[..., :NOPE_DIM], head_zeros(q_lora, HEAD_PAD - NOPE_DIM - ROPE_DIM)],
            axis=3).reshape(depth, q_lora, A_HEADS * HEAD_PAD).astype(BF16),
        kv_norm=vec(kv_norm), wk_pad=wk_pad,
        wv=ukv[..., NOPE_DIM:].reshape(depth, kv_lora, A_HEADS * V_DIM).astype(BF16),
        conv_b_w=_pad_rows(conv_b_w, CONV_B_PAD), conv_b_bias=vec(conv_b_bias),
        conv_b_ln_g=vec(conv_b_ln_g), conv_b_ln_b=vec(conv_b_ln_b),
        gmlp_vn_g=vec(gmlp_vn_g), gmlp_vn_b=vec(gmlp_vn_b), gmlp_w_s=gmlp_w_s,
        bs_full=jnp.repeat(jnp.swapaxes(gmlp_b_s, 1, 2), c_c // C_GROUPS, axis=2),
        conv_d_w=_pad_rows(conv_d_w, CONV_D_PAD), kb_w=kb_w, kd_w=kd_w,
        w_br_a=w_br_a.astype(BF16), w_br_b=w_br_b.astype(BF16), w_br_c=w_br_c.astype(BF16),
        w_br_d=w_br_d.astype(BF16), w_o=w_o.astype(BF16))
    wk_t = jnp.swapaxes(wk_pad, 1, 2)
    f1 = (vec(ffn1_norm_pre), vec(ffn1_norm_post), ffn1_w_gu.astype(BF16), ffn1_w_down.astype(BF16))
    f2 = (vec(ffn2_norm_pre), vec(ffn2_norm_post), ffn2_w_gu.astype(BF16), ffn2_w_down.astype(BF16))

    tabs_p = _rope_tabs(jnp.arange(s))
    tabs_s = _rope_tabs(past + jnp.arange(t))
    zero_b = jnp.zeros((b, CONV_B_PAD, c_b), F32)
    zero_d = jnp.zeros((b, CONV_D_PAD, c_d), F32)
    tail_b = CONV_B_PAD - (kb_w - 1)
    tail_d = CONV_D_PAD - (kd_w - 1)
    st_b = jnp.pad(state_conv_b, ((0, 0), (0, 0), (tail_b, 0), (0, 0)))
    st_d = jnp.pad(state_conv_d, ((0, 0), (0, 0), (tail_d, 0), (0, 0)))
    ckr_pad = jnp.pad(cache_k_rope, ((0, 0), (0, 0), (0, 0), (0, LANES - ROPE_DIM))).astype(BF16)
    tq = _pick_tile(s, (256, 128))

    xp = x_prompt.reshape(b * s, d)
    xs = x_sample.reshape(bs * t, d)
    stacked = None
    outs = [[] for _ in range(7)]
    for l in range(depth):
        xp, xs = _ffn(xp, xs, *f1, l)

        qp, kp, vtp, yp, lat_all, kr_all, btp, dtp = _mix_in(
            xp, tabs_p, zero_b, zero_d, None, w, l, b=b, s=s, tq=tq, prompt=True, stacked=stacked)
        stacked = (lat_all, kr_all)
        ap = _attn_prompt(qp.reshape(b, s, -1), kp.reshape(b, s, -1), vtp, tq=tq)

        qs, ys, lats, krs, bts, dts, vgs = _mix_in(
            xs, tabs_s, st_b, st_d, l, w, l, b=bs, s=t, tq=t, prompt=False)
        lats, krs, vgs = (v.reshape(bs, t, -1) for v in (lats, krs, vgs))
        a_s = _attn_sample(qs.reshape(bs, t, -1), cache_kv_latent, ckr_pad, lats, krs, wk_t, w["wv"], l)

        xp, xs = _mix_out(xp, ap.reshape(b * s, -1), yp, xs, a_s.reshape(bs * t, -1), ys, w, l)
        xp, xs = _ffn(xp, xs, *f2, l)

        for lst, val in zip(outs, (btp[:, tail_b:], dtp[:, tail_d:], lats, krs[..., :ROPE_DIM],
                                   bts[:, tail_b:], vgs, dts[:, tail_d:])):
            lst.append(val)
    cb_p, cd_p, lat_s, kr_s, cb_s, vc_s, cd_s = (jnp.stack(o) for o in outs)
    return (xp.reshape(b, s, d), xs.reshape(bs, t, d), stacked[0], stacked[1], cb_p, cd_p,
            lat_s, kr_s, cb_s, vc_s, cd_s)
```
